```python
import math
import jax
import jax.numpy as jnp
from jax import lax
import numpy as np


D_MODEL = 1024
BATCH = 32
SEQ = 2048
DEPTH = 2

GRID_W = 64
CTX_LEN = 256
EPS = 1e-6
N_MOD = 6
SSD_HEADS = 4
SSD_HEAD_DIM = 64
SSD_WIDTH = SSD_HEADS * SSD_HEAD_DIM
SSD_GROUPS = 2
SSD_STATE = 64
SSD_CONV = 5
SSD_CHUNK = 128
SSD_XBC = SSD_WIDTH + 2 * SSD_GROUPS * SSD_STATE
SSD_IN = SSD_WIDTH + SSD_XBC + 2 * SSD_HEADS
DIFF_HEADS = 4
DIFF_HEAD_DIM = 64
DIFF_WIDTH = DIFF_HEADS * 2 * DIFF_HEAD_DIM
DIFF_IN = 3 * DIFF_WIDTH
Q_BLOCK = 128
ROPE_THETA = 10000.0
S5_GROUP = 16
S5_WIDTH = 256
S5_GROUPS = S5_WIDTH // S5_GROUP
S5_STATE = 64
MIX_WIDTH = SSD_WIDTH + DIFF_WIDTH + S5_WIDTH
IN_WIDTH = SSD_IN + DIFF_IN + S5_WIDTH
FFN_DIM = 2816
FFN_CONV = 3
F32 = jnp.float32

kernel_name = 'hybrid_ssd_diffattn_s5_prefix_dit_block'


def flip(t):
    return jnp.flip(t, axis=1)


def ident(t):
    return t


def rmsnorm(x, w):
    xf = x.astype(F32)
    y = xf * lax.rsqrt(jnp.mean(xf * xf, axis=-1, keepdims=True) + EPS)
    return (y * w.astype(F32)).astype(x.dtype)


def modulate(h, shift, scale):
    return h * (1.0 + scale) + shift


def dwconv(x, w, b):
    k = w.shape[0]
    y = lax.conv_general_dilated(x, w[:, None, :].astype(x.dtype), window_strides=(1,),
                                 padding=[((k - 1) // 2, k // 2)],
                                 dimension_numbers=('NWC', 'WIO', 'NWC'),
                                 feature_group_count=x.shape[-1])
    return y + b.astype(x.dtype)


def axial_rope_tables(n_tokens):
    rows = n_tokens // GRID_W
    r, col = jnp.meshgrid(jnp.arange(rows, dtype=F32), jnp.arange(GRID_W, dtype=F32), indexing='ij')
    quarter = DIFF_HEAD_DIM // 4
    inv_freq = ROPE_THETA ** (-jnp.arange(quarter, dtype=F32) / quarter)
    ang_r = r.reshape(-1, 1) * inv_freq
    ang_c = col.reshape(-1, 1) * inv_freq
    return jnp.cos(ang_r), jnp.sin(ang_r), jnp.cos(ang_c), jnp.sin(ang_c)


def rotate(x, cos, sin):
    x1, x2 = jnp.split(x, 2, axis=-1)
    return jnp.concatenate([x1 * cos - x2 * sin, x2 * cos + x1 * sin], axis=-1)


def apply_axial_rope(x, cos_r, sin_r, cos_c, sin_c):
    def bc(t):
        return t[None, :, None, None, :]
    xf = x.astype(F32)
    x_row, x_col = jnp.split(xf, 2, axis=-1)
    out = jnp.concatenate([rotate(x_row, bc(cos_r), bc(sin_r)),
                           rotate(x_col, bc(cos_c), bc(sin_c))], axis=-1)
    return out.astype(x.dtype)


def ssd_scan(xs, dt, a, bm, cm, h0):
    bsz, n, nh, hp = xs.shape
    q = SSD_CHUNK
    nc = n // q
    xdt = (xs * dt[..., None]).reshape(bsz, nc, q, nh, hp)
    bm = bm.reshape(bsz, nc, q, nh, SSD_STATE)
    cm = cm.reshape(bsz, nc, q, nh, SSD_STATE)
    a_cs = jnp.cumsum((dt * a).reshape(bsz, nc, q, nh), axis=2).transpose(0, 3, 1, 2)
    tri = jnp.tril(jnp.ones((q, q), bool))
    decay_in = jnp.exp(jnp.where(tri, a_cs[..., :, None] - a_cs[..., None, :], -jnp.inf))
    scores = jnp.einsum('bclhn,bcshn->bhcls', cm, bm) * decay_in
    y_diag = jnp.einsum('bhcls,bcshp->bclhp', scores, xdt)
    decay_out = jnp.exp(a_cs[..., -1:] - a_cs).transpose(0, 2, 3, 1)
    states = jnp.einsum('bclhn,bclhp->bchpn', bm, xdt * decay_out[..., None])
    states = jnp.concatenate([h0[:, None], states], axis=1)
    chunk_cs = jnp.cumsum(jnp.pad(a_cs[..., -1], ((0, 0), (0, 0), (1, 0))), axis=-1)
    tri_c = jnp.tril(jnp.ones((nc + 1, nc + 1), bool))
    decay_chunk = jnp.exp(jnp.where(tri_c, chunk_cs[..., :, None] - chunk_cs[..., None, :], -jnp.inf))
    states = jnp.einsum('bhzc,bchpn->bzhpn', decay_chunk, states)
    y_off = jnp.einsum('bclhn,bchpn->bclhp', cm, states[:, :-1]) * jnp.exp(a_cs).transpose(0, 2, 3, 1)[..., None]
    return (y_diag + y_off).reshape(bsz, n, nh, hp), states[:, -1]


def ssd_prep(p, conv_w, conv_b, dt_bias):
    bsz, n, _ = p.shape
    z = p[..., :SSD_WIDTH]
    xbc = jax.nn.silu(dwconv(p[..., SSD_WIDTH:SSD_WIDTH + SSD_XBC], conv_w, conv_b)).astype(F32)
    dt_raw = p[..., SSD_WIDTH + SSD_XBC:].astype(F32).reshape(bsz, n, 2, SSD_HEADS)
    xs = xbc[..., :SSD_WIDTH].reshape(bsz, n, SSD_HEADS, SSD_HEAD_DIM)
    gn = SSD_GROUPS * SSD_STATE
    rep = SSD_HEADS // SSD_GROUPS
    bm = jnp.repeat(xbc[..., SSD_WIDTH:SSD_WIDTH + gn].reshape(bsz, n, SSD_GROUPS, SSD_STATE), rep, axis=2)
    cm = jnp.repeat(xbc[..., SSD_WIDTH + gn:].reshape(bsz, n, SSD_GROUPS, SSD_STATE), rep, axis=2)
    dt = jax.nn.softplus(dt_raw + dt_bias.astype(F32))
    return z, xs, bm, cm, dt


def ssd_gate(y, z, norm_w):
    bsz, n = y.shape[:2]
    return rmsnorm(y.reshape(bsz, n, SSD_WIDTH) * jax.nn.silu(z.astype(F32)), norm_w)


def ssd_mixer(p_c, p_l, conv_w, conv_b, dt_bias, a_log, d_skip, norm_w, need_ctx):
    out_dtype = p_l.dtype
    a = -jnp.exp(a_log.astype(F32))
    z_c, x_c, b_c, c_c, dt_c = ssd_prep(p_c, conv_w, conv_b, dt_bias)
    z_l, x_l, b_l, c_l, dt_l = ssd_prep(p_l, conv_w, conv_b, dt_bias)
    d = d_skip.astype(F32)[:, None]
    y_l = d * x_l
    y_c = d * x_c if need_ctx else None
    h0 = jnp.zeros((p_l.shape[0], SSD_HEADS, SSD_HEAD_DIM, SSD_STATE), F32)
    for direction in range(2):
        o = flip if direction else ident
        yc_d, hc_d = ssd_scan(o(x_c), o(dt_c[:, :, direction]), a[direction], o(b_c), o(c_c), h0)
        yl_d, _ = ssd_scan(o(x_l), o(dt_l[:, :, direction]), a[direction], o(b_l), o(c_l), hc_d)
        y_l = y_l + o(yl_d)
        if need_ctx:
            y_c = y_c + o(yc_d)
    out_l = ssd_gate(y_l, z_l, norm_w).astype(out_dtype)
    out_c = ssd_gate(y_c, z_c, norm_w).astype(out_dtype) if need_ctx else None
    return out_c, out_l


def diff_attend(q, k, v, lam):
    s = jnp.einsum('bqhcd,bshcd->bhcqs', q, k).astype(F32) * (DIFF_HEAD_DIM ** -0.5)
    p = jax.nn.softmax(s, axis=-1)
    w = p[:, :, 0] - lam * p[:, :, 1]
    return jnp.einsum('bhqs,bshe->bqhe', w.astype(v.dtype), v)


def diff_attn_mixer(p_c, p_l, lam_q1, lam_k1, lam_q2, lam_k2, subln_w, lam_init, need_ctx):
    bsz, n, _ = p_l.shape

    def split(p):
        m = p.shape[1]
        q = p[..., :DIFF_WIDTH].reshape(bsz, m, DIFF_HEADS, 2, DIFF_HEAD_DIM)
        k = p[..., DIFF_WIDTH:2 * DIFF_WIDTH].reshape(bsz, m, DIFF_HEADS, 2, DIFF_HEAD_DIM)
        v = p[..., 2 * DIFF_WIDTH:].reshape(bsz, m, DIFF_HEADS, 2 * DIFF_HEAD_DIM)
        return q, k, v

    def head_out(o):
        return (rmsnorm(o, subln_w) * (1.0 - lam_init)).reshape(bsz, o.shape[1], DIFF_WIDTH)

    q_c, k_c, v_c = split(p_c)
    q_l, k_l, v_l = split(p_l)
    tabs = axial_rope_tables(n)
    q_l = apply_axial_rope(q_l, *tabs)
    k_l = apply_axial_rope(k_l, *tabs)
    lam = (jnp.exp(jnp.sum(lam_q1.astype(F32) * lam_k1.astype(F32)))
           - jnp.exp(jnp.sum(lam_q2.astype(F32) * lam_k2.astype(F32))) + lam_init)
    k_all = jnp.concatenate([k_c, k_l], axis=1)
    v_all = jnp.concatenate([v_c, v_l], axis=1)
    nb = n // Q_BLOCK
    q_blocks = q_l.reshape(bsz, nb, Q_BLOCK, DIFF_HEADS, 2, DIFF_HEAD_DIM).swapaxes(0, 1)
    o_l = lax.map(lambda qb: diff_attend(qb, k_all, v_all, lam), q_blocks)
    o_l = o_l.swapaxes(0, 1).reshape(bsz, n, DIFF_HEADS, 2 * DIFF_HEAD_DIM)
    out_l = head_out(o_l)
    out_c = head_out(diff_attend(q_c, k_c, v_c, lam)) if need_ctx else None
    return out_c, out_l


def s5_discretise(lam_re, lam_im, log_step, b_re, b_im):
    step = jnp.exp(log_step.astype(F32))[:, None]
    lr, li = lam_re.astype(F32), lam_im.astype(F32)
    mag = jnp.exp(lr * step)
    ab_re = mag * jnp.cos(li * step)
    ab_im = mag * jnp.sin(li * step)
    den = lr * lr + li * li
    nr, ni = ab_re - 1.0, ab_im
    coef_re = ((nr * lr + ni * li) / den)[..., None]
    coef_im = ((ni * lr - nr * li) / den)[..., None]
    br, bi = b_re.astype(F32), b_im.astype(F32)
    return ab_re, ab_im, coef_re * br - coef_im * bi, coef_re * bi + coef_im * br


def s5_combine(e1, e2):
    a1r, a1i, b1r, b1i = e1
    a2r, a2i, b2r, b2i = e2
    return (a2r * a1r - a2i * a1i, a2r * a1i + a2i * a1r,
            a2r * b1r - a2i * b1i + b2r, a2r * b1i + a2i * b1r + b2i)


def s5_scan(u, ab_re, ab_im, bb_re, bb_im, h0_re, h0_im):
    n = u.shape[1]
    bu_re = jnp.einsum('blgc,gnc->blgn', u, bb_re)
    bu_im = jnp.einsum('blgc,gnc->blgn', u, bb_im)
    bu_re = bu_re.at[:, 0].add(ab_re * h0_re - ab_im * h0_im)
    bu_im = bu_im.at[:, 0].add(ab_re * h0_im + ab_im * h0_re)
    a_re = jnp.broadcast_to(ab_re, (1, n) + ab_re.shape)
    a_im = jnp.broadcast_to(ab_im, (1, n) + ab_im.shape)
    _, _, h_re, h_im = lax.associative_scan(s5_combine, (a_re, a_im, bu_re, bu_im), axis=1)
    return h_re, h_im


def s5_readout(h_re, h_im, c_re, c_im):
    return (jnp.einsum('blgn,gcn->blgc', h_re, c_re.astype(F32))
            - jnp.einsum('blgn,gcn->blgc', h_im, c_im.astype(F32)))


def s5_mixer(u_c, u_l, lam_re, lam_im, log_step, b_re, b_im, c_re, c_im, d_skip, glu_w, glu_b, need_ctx):
    out_dtype = u_l.dtype
    bsz = u_l.shape[0]

    def groups(u):
        return u.astype(F32).reshape(bsz, u.shape[1], S5_GROUPS, S5_GROUP)

    def finish(y):
        y = jax.nn.gelu(y.reshape(bsz, y.shape[1], S5_WIDTH))
        return (y * jax.nn.sigmoid(y @ glu_w.astype(F32) + glu_b.astype(F32))).astype(out_dtype)

    uc, ul = groups(u_c), groups(u_l)
    d = d_skip.astype(F32)
    y_l = d * ul
    y_c = d * uc if need_ctx else None
    zero = jnp.zeros((bsz, S5_GROUPS, S5_STATE), F32)
    for direction in range(2):
        o = flip if direction else ident
        ab_re, ab_im, bb_re, bb_im = s5_discretise(lam_re[direction], lam_im[direction], log_step[direction],
                                                   b_re[direction], b_im[direction])
        hc_re, hc_im = s5_scan(o(uc), ab_re, ab_im, bb_re, bb_im, zero, zero)
        hl_re, hl_im = s5_scan(o(ul), ab_re, ab_im, bb_re, bb_im, hc_re[:, -1], hc_im[:, -1])
        y_l = y_l + o(s5_readout(hl_re, hl_im, c_re[direction], c_im[direction]))
        if need_ctx:
            y_c = y_c + o(s5_readout(hc_re, hc_im, c_re[direction], c_im[direction]))
    out_l = finish(y_l)
    out_c = finish(y_c) if need_ctx else None
    return out_c, out_l


def token_mixer(h_c, h_l, w_in, w_out, ssd_p, diff_p, s5_p, lam_init, need_ctx):
    p_c = h_c @ w_in
    p_l = h_l @ w_in
    o1, o2 = SSD_IN, SSD_IN + DIFF_IN
    a_c, a_l = ssd_mixer(p_c[..., :o1], p_l[..., :o1], *ssd_p, need_ctx)
    b_c, b_l = diff_attn_mixer(p_c[..., o1:o2], p_l[..., o1:o2], *diff_p, lam_init, need_ctx)
    s_c, s_l = s5_mixer(p_c[..., o2:], p_l[..., o2:], *s5_p, need_ctx)
    y_l = jnp.concatenate([a_l, b_l, s_l], axis=-1) @ w_out
    y_c = jnp.concatenate([a_c, b_c, s_c], axis=-1) @ w_out if need_ctx else None
    return y_c, y_l


def conv_ffn(h, w_gate, w_up, conv_w, conv_b, w_down):
    g = dwconv(h @ w_gate, conv_w, conv_b)
    return (jax.nn.silu(g) * (h @ w_up)) @ w_down


def setup_inputs(seed: int = 0) -> dict:
    key = jax.random.key(seed)
    ks = jax.random.split(key, 40)
    L = DEPTH

    def nrm(i, shape, scale):
        return scale * jax.random.normal(ks[i], shape, F32)

    def unif(i, shape, lo, hi):
        return jax.random.uniform(ks[i], shape, F32, lo, hi)

    dt0 = jnp.exp(unif(14, (L, 2, SSD_HEADS), math.log(1e-3), math.log(1e-1)))
    return {
        'x': nrm(0, (BATCH, SEQ, D_MODEL), 1.0),
        'c': nrm(1, (BATCH, D_MODEL), 1.0),
        'ctx': nrm(2, (BATCH, CTX_LEN, D_MODEL), 1.0),
        'c_ctx': nrm(3, (D_MODEL,), 1.0),
        'mod_w': nrm(4, (L, D_MODEL, N_MOD * D_MODEL), D_MODEL ** -0.5),
        'mod_b': nrm(5, (L, N_MOD * D_MODEL), 0.02),
        'mix_norm_pre': 1.0 + nrm(6, (L, D_MODEL), 0.02),
        'mix_norm_post': 1.0 + nrm(7, (L, D_MODEL), 0.02),
        'ffn_norm_pre': 1.0 + nrm(8, (L, D_MODEL), 0.02),
        'ffn_norm_post': 1.0 + nrm(9, (L, D_MODEL), 0.02),
        'w_in': nrm(10, (L, D_MODEL, IN_WIDTH), D_MODEL ** -0.5),
        'w_out': nrm(11, (L, MIX_WIDTH, D_MODEL), MIX_WIDTH ** -0.5),
        'ssd_conv_w': nrm(12, (L, SSD_CONV, SSD_XBC), SSD_CONV ** -0.5),
        'ssd_conv_b': nrm(13, (L, SSD_XBC), 0.02),
        'ssd_dt_bias': dt0 + jnp.log(-jnp.expm1(-dt0)),
        'ssd_a_log': jnp.log(unif(15, (L, 2, SSD_HEADS), 1.0, 16.0)),
        'ssd_d': 1.0 + nrm(16, (L, SSD_HEADS), 0.1),
        'ssd_norm_w': 1.0 + nrm(17, (L, SSD_WIDTH), 0.02),
        'diff_lam_q1': nrm(18, (L, DIFF_HEAD_DIM), 0.1),
        'diff_lam_k1': nrm(19, (L, DIFF_HEAD_DIM), 0.1),
        'diff_lam_q2': nrm(20, (L, DIFF_HEAD_DIM), 0.1),
        'diff_lam_k2': nrm(21, (L, DIFF_HEAD_DIM), 0.1),
        'diff_subln_w': 1.0 + nrm(22, (L, 2 * DIFF_HEAD_DIM), 0.02),
        's5_lam_re': -0.5 + nrm(23, (L, 2, S5_GROUPS, S5_STATE), 0.01),
        's5_lam_im': jnp.pi * jnp.arange(S5_STATE, dtype=F32) + nrm(24, (L, 2, S5_GROUPS, S5_STATE), 0.01),
        's5_log_step': unif(25, (L, 2, S5_GROUPS), math.log(1e-3), math.log(1e-1)),
        's5_b_re': nrm(26, (L, 2, S5_GROUPS, S5_STATE, S5_GROUP), (2 * S5_GROUP) ** -0.5),
        's5_b_im': nrm(27, (L, 2, S5_GROUPS, S5_STATE, S5_GROUP), (2 * S5_GROUP) ** -0.5),
        's5_c_re': nrm(28, (L, 2, S5_GROUPS, S5_GROUP, S5_STATE), (2 * S5_STATE) ** -0.5),
        's5_c_im': nrm(29, (L, 2, S5_GROUPS, S5_GROUP, S5_STATE), (2 * S5_STATE) ** -0.5),
        's5_d': nrm(30, (L, S5_GROUPS, S5_GROUP), 1.0),
        's5_glu_w': nrm(31, (L, S5_WIDTH, S5_WIDTH), S5_WIDTH ** -0.5),
        's5_glu_b': nrm(32, (L, S5_WIDTH), 0.02),
        'ffn_w_gate': nrm(33, (L, D_MODEL, FFN_DIM), D_MODEL ** -0.5),
        'ffn_w_up': nrm(34, (L, D_MODEL, FFN_DIM), D_MODEL ** -0.5),
        'ffn_conv_w': nrm(35, (L, FFN_CONV, FFN_DIM), FFN_CONV ** -0.5),
        'ffn_conv_b': nrm(36, (L, FFN_DIM), 0.02),
        'ffn_w_down': nrm(37, (L, FFN_DIM, D_MODEL), FFN_DIM ** -0.5),
    }


def reference(x, c, ctx, c_ctx, mod_w, mod_b, mix_norm_pre, mix_norm_post, ffn_norm_pre, ffn_norm_post,
              w_in, w_out, ssd_conv_w, ssd_conv_b, ssd_dt_bias, ssd_a_log, ssd_d, ssd_norm_w,
              diff_lam_q1, diff_lam_k1, diff_lam_q2, diff_lam_k2, diff_subln_w,
              s5_lam_re, s5_lam_im, s5_log_step, s5_b_re, s5_b_im, s5_c_re, s5_c_im, s5_d, s5_glu_w, s5_glu_b,
              ffn_w_gate, ffn_w_up, ffn_conv_w, ffn_conv_b, ffn_w_down):
    for layer in range(DEPTH):
        need_ctx = layer < DEPTH - 1
        lam_init = 0.8 - 0.6 * math.exp(-0.3 * layer)
        mod_l = (jax.nn.silu(c) @ mod_w[layer] + mod_b[layer])[:, None, :]
        mod_c = jax.nn.silu(c_ctx) @ mod_w[layer] + mod_b[layer]
        sh_a, sc_a, g_a, sh_f, sc_f, g_f = jnp.split(mod_l, N_MOD, axis=-1)
        csh_a, csc_a, cg_a, csh_f, csc_f, cg_f = jnp.split(mod_c, N_MOD, axis=-1)

        h_l = modulate(rmsnorm(x, mix_norm_pre[layer]), sh_a, sc_a)
        h_c = modulate(rmsnorm(ctx, mix_norm_pre[layer]), csh_a, csc_a)
        ssd_p = (ssd_conv_w[layer], ssd_conv_b[layer], ssd_dt_bias[layer], ssd_a_log[layer],
                 ssd_d[layer], ssd_norm_w[layer])
        diff_p = (diff_lam_q1[layer], diff_lam_k1[layer], diff_lam_q2[layer], diff_lam_k2[layer],
                  diff_subln_w[layer])
        s5_p = (s5_lam_re[layer], s5_lam_im[layer], s5_log_step[layer], s5_b_re[layer], s5_b_im[layer],
                s5_c_re[layer], s5_c_im[layer], s5_d[layer], s5_glu_w[layer], s5_glu_b[layer])
        y_c, y_l = token_mixer(h_c, h_l, w_in[layer], w_out[layer], ssd_p, diff_p, s5_p, lam_init, need_ctx)
        x = x + g_a * rmsnorm(y_l, mix_norm_post[layer])

        ffn_p = (ffn_w_gate[layer], ffn_w_up[layer], ffn_conv_w[layer], ffn_conv_b[layer], ffn_w_down[layer])
        f_l = conv_ffn(modulate(rmsnorm(x, ffn_norm_pre[layer]), sh_f, sc_f), *ffn_p)
        x = x + g_f * rmsnorm(f_l, ffn_norm_post[layer])

        if need_ctx:
            ctx = ctx + cg_a * rmsnorm(y_c, mix_norm_post[layer])
            f_c = conv_ffn(modulate(rmsnorm(ctx, ffn_norm_pre[layer]), csh_f, csc_f), *ffn_p)
            ctx = ctx + cg_f * rmsnorm(f_c, ffn_norm_post[layer])
    return x
```

```python
import functools
import math

import numpy as np
import jax
import jax.numpy as jnp
from jax import lax
from jax.experimental import pallas as pl
from jax.experimental.pallas import tpu as pltpu

F32 = jnp.float32
BF16 = jnp.bfloat16
HI = lax.Precision.HIGHEST

D_MODEL = 1024
N_LAYERS = 2
GRID_W = 64
EPS = 1e-6
N_MOD = 6
SSD_HEADS = 4
SSD_HEAD_DIM = 64
SSD_WIDTH = 256
SSD_GROUPS = 2
SSD_STATE = 64
SSD_CONV = 5
SSD_CHUNK = 128
SSD_XBC = 512
SSD_IN = 776
DIFF_HEADS = 4
DIFF_HEAD_DIM = 64
DIFF_WIDTH = 512
DIFF_IN = 1536
ROPE_THETA = 10000.0
S5_GROUP = 16
S5_WIDTH = 256
S5_GROUPS = 16
S5_STATE = 64
S5_T = 16
FFN_DIM = 2816
FFN_CHUNK = 256
FFN_CONV = 3

LANES = 128
SUBLANES = 8
W_IN_PAD = 2688
VMEM_LIMIT = 56 * 1024 * 1024


def _cparams(n_axes):
    return pltpu.CompilerParams(dimension_semantics=("parallel",) * n_axes,
                                vmem_limit_bytes=VMEM_LIMIT)


def _rms(x, w):
    return x * lax.rsqrt(jnp.mean(x * x, axis=-1, keepdims=True) + EPS) * w


def _silu(x):
    return x * jax.nn.sigmoid(x)


def _row_tile(n):
    return min(512, n)


def _mod_kernel(c_ref, w_ref, b_ref, o_ref):
    a = _silu(c_ref[...])
    o_ref[0] = jnp.dot(a, w_ref[0], preferred_element_type=F32, precision=HI) + b_ref[0]


def _modulation(cc, mod_w, mod_b):
    nl, d, n = mod_w.shape
    r = cc.shape[0]
    tn = 512
    return pl.pallas_call(
        _mod_kernel,
        grid=(nl, n // tn),
        in_specs=[pl.BlockSpec((r, d), lambda l, j: (0, 0)),
                  pl.BlockSpec((1, d, tn), lambda l, j: (l, 0, j)),
                  pl.BlockSpec((1, 1, tn), lambda l, j: (l, 0, j))],
        out_specs=pl.BlockSpec((1, r, tn), lambda l, j: (l, 0, j)),
        out_shape=jax.ShapeDtypeStruct((nl, r, n), F32),
        compiler_params=_cparams(2),
        name="modulation",
    )(cc, mod_w, mod_b.reshape(nl, 1, n))


def _in_proj_kernel(*refs, rope):
    if rope:
        (x_ref, sh_ref, sc_ref, nw_ref, w_ref, cos_ref, sin_ref,
         zx_ref, dt_ref, q_ref, k_ref, v_ref, u_ref) = refs
    else:
        (x_ref, sh_ref, sc_ref, nw_ref, w_ref,
         zx_ref, dt_ref, q_ref, k_ref, v_ref, u_ref) = refs
    h = _rms(x_ref[0], nw_ref[...]) * (1.0 + sc_ref[0]) + sh_ref[0]
    hb = h.astype(BF16)

    def mm(lo, hi):
        return jnp.dot(hb, w_ref[:, lo:hi], preferred_element_type=F32)

    def rot(p):
        lane = lax.broadcasted_iota(jnp.int32, p.shape, 1)
        partner = jnp.where(lane % 32 < 16, pltpu.roll(p, LANES - 16, 1), pltpu.roll(p, 16, 1))
        return p * cos_ref[...] + partner * sin_ref[...]

    o_zx = SSD_WIDTH + SSD_XBC
    zx_ref[0] = mm(0, o_zx)
    for j in range(DIFF_WIDTH // LANES):
        q = mm(o_zx + j * LANES, o_zx + (j + 1) * LANES) * (DIFF_HEAD_DIM ** -0.5)
        k = mm(o_zx + DIFF_WIDTH + j * LANES, o_zx + DIFF_WIDTH + (j + 1) * LANES)
        if rope:
            q, k = rot(q), rot(k)
        q_ref[0, :, j * LANES:(j + 1) * LANES] = q.astype(BF16)
        k_ref[0, :, j * LANES:(j + 1) * LANES] = k.astype(BF16)
    o_v = o_zx + 2 * DIFF_WIDTH
    v_ref[0] = mm(o_v, o_v + DIFF_WIDTH).astype(BF16)
    o_u = o_v + DIFF_WIDTH
    u_ref[0] = mm(o_u, o_u + S5_WIDTH)
    dt_ref[0] = mm(o_u + S5_WIDTH, W_IN_PAD)


def _in_proj(x, shift, scale, norm_w, w_merged, rope_tabs):
    bsz, n, d = x.shape
    tm = _row_tile(n)
    rope = rope_tabs is not None
    row = lambda b, i: (b, i, 0)
    per_b = lambda b, i: (b, 0, 0)
    const = lambda b, i: (0, 0)
    in_specs = [pl.BlockSpec((1, tm, d), row),
                pl.BlockSpec((1, 1, d), per_b),
                pl.BlockSpec((1, 1, d), per_b),
                pl.BlockSpec((1, d), const),
                pl.BlockSpec((d, W_IN_PAD), const)]
    args = [x, shift, scale, norm_w.reshape(1, d), w_merged]
    if rope:
        in_specs += [pl.BlockSpec((tm, LANES), lambda b, i: (i, 0))] * 2
        args += list(rope_tabs)
    widths = (SSD_WIDTH + SSD_XBC, LANES, DIFF_WIDTH, DIFF_WIDTH, DIFF_WIDTH, S5_WIDTH)
    dtypes = (F32, F32, BF16, BF16, BF16, F32)
    return pl.pallas_call(
        functools.partial(_in_proj_kernel, rope=rope),
        grid=(bsz, n // tm),
        in_specs=in_specs,
        out_specs=[pl.BlockSpec((1, tm, w), row) for w in widths],
        out_shape=[jax.ShapeDtypeStruct((bsz, n, w), t) for w, t in zip(widths, dtypes)],
        compiler_params=_cparams(2),
        name="in_proj_rope" if rope else "in_proj",
    )(*args)


def _merge_w_in(w):
    o1, o2 = SSD_IN, SSD_IN + DIFF_IN
    parts = [w[:, :SSD_WIDTH + SSD_XBC], w[:, o1:o2], w[:, o2:], w[:, SSD_WIDTH + SSD_XBC:o1]]
    m = jnp.concatenate(parts, axis=1)
    return jnp.pad(m, ((0, 0), (0, W_IN_PAD - m.shape[1]))).astype(BF16)


def _rope_tables(n):
    t = jnp.arange(n, dtype=jnp.int32)
    pos = jnp.stack([(t // GRID_W).astype(F32), (t % GRID_W).astype(F32)], axis=1)
    quarter = DIFF_HEAD_DIM // 4
    inv_freq = ROPE_THETA ** (-jnp.arange(quarter, dtype=F32) / quarter)
    lane = np.arange(LANES)
    axis = (lane % DIFF_HEAD_DIM) // (DIFF_HEAD_DIM // 2)
    freq = lane % quarter
    ang = pos[:, axis] * inv_freq[freq][None, :]
    sign = np.where(lane % (2 * quarter) < quarter, -1.0, 1.0).astype(np.float32)
    return jnp.cos(ang), jnp.sin(ang) * sign[None, :]


def _ssd_chunk(xact_ref, dt_ref, dta_ref, state_ref, y_ref, c, direction):
    q = SSD_CHUNK
    r0 = pl.multiple_of(c * q, q)
    xa = xact_ref[pl.ds(r0, q), :]
    dt = dt_ref[pl.ds(r0, q), :]
    dta = dta_ref[pl.ds(r0, q), :]
    rows = lax.broadcasted_iota(jnp.int32, (q, q), 0)
    cols = lax.broadcasted_iota(jnp.int32, (q, q), 1)
    mask = (rows >= cols) if direction == 0 else (rows <= cols)
    acs = jnp.dot(mask.astype(F32), dta, preferred_element_type=F32, precision=HI)
    acs_t = acs.T
    bm = xa[:, SSD_WIDTH:SSD_WIDTH + LANES]
    cm = xa[:, SSD_WIDTH + LANES:]
    bm_t = bm.T
    last = q - 1 if direction == 0 else 0
    ys = []
    cb = [None] * SSD_GROUPS
    for h in range(SSD_HEADS):
        g = h // (SSD_HEADS // SSD_GROUPS)
        j = direction * SSD_HEADS + h
        cg = cm[:, g * SSD_STATE:(g + 1) * SSD_STATE].astype(BF16)
        if cb[g] is None:
            bg = bm[:, g * SSD_STATE:(g + 1) * SSD_STATE].astype(BF16)
            cb[g] = lax.dot_general(cg, bg, (((1,), (1,)), ((), ())), preferred_element_type=F32)
        col = acs[:, j:j + 1]
        row = acs_t[j:j + 1, :]
        decay = jnp.exp(jnp.where(mask, col - row, -jnp.inf))
        xdt = (xa[:, h * SSD_HEAD_DIM:(h + 1) * SSD_HEAD_DIM] * dt[:, j:j + 1]).astype(BF16)
        y_diag = jnp.dot((cb[g] * decay).astype(BF16), xdt, preferred_element_type=F32)
        state = state_ref[j]
        y_off = jnp.dot(cg, state.astype(BF16), preferred_element_type=F32) * jnp.exp(col)
        ys.append(y_diag + y_off)
        total = acs_t[j:j + 1, last:last + 1]
        b_dec = bm_t[g * SSD_STATE:(g + 1) * SSD_STATE, :] * jnp.exp(total - row)
        state_ref[j] = jnp.exp(total) * state + jnp.dot(b_dec.astype(BF16), xdt,
                                                        preferred_element_type=F32)
    y_ref[pl.ds(r0, q), :] += jnp.concatenate(ys, axis=-1)


def _ssd_kernel(*refs, n_ctx, n_lat, need_ctx):
    (zx_l, dt_l, zx_c, dt_c, cw_ref, cb_ref, dtb_ref, alog_ref, d_ref, nw_ref) = refs[:10]
    outs = refs[10:12] if need_ctx else (refs[10], None)
    pad_ref, xact_ref, dts_ref, dta_ref, y_ref, state_ref = refs[-6:]
    out_l, out_c = outs
    piece = 256
    halo = (SSD_CONV - 1) // 2

    for seg_ref, n, base in ((zx_c, n_ctx, 0), (zx_l, n_lat, n_ctx)):
        pad_ref[0:SUBLANES, :] = jnp.zeros((SUBLANES, SSD_XBC), F32)
        pad_ref[SUBLANES + n:2 * SUBLANES + n, :] = jnp.zeros((SUBLANES, SSD_XBC), F32)
        pad_ref[SUBLANES:SUBLANES + n, :] = seg_ref[0, :, SSD_WIDTH:]
        for t0 in range(0, n, piece):
            m = min(piece, n - t0)
            acc = jnp.broadcast_to(cb_ref[...], (m, SSD_XBC))
            for k in range(SSD_CONV):
                lo = SUBLANES + t0 + k - halo
                acc = acc + cw_ref[k:k + 1, :] * pad_ref[lo:lo + m, :]
            xact_ref[base + t0:base + t0 + m, :] = _silu(acc)
    a_neg = -jnp.exp(alog_ref[...])
    for seg_ref, n, base in ((dt_c, n_ctx, 0), (dt_l, n_lat, n_ctx)):
        dt = jax.nn.softplus(seg_ref[0] + dtb_ref[...])
        dts_ref[base:base + n, :] = dt
        dta_ref[base:base + n, :] = dt * a_neg
    n_tot = n_ctx + n_lat
    for t0 in range(0, n_tot, piece):
        y_ref[t0:t0 + piece, :] = xact_ref[t0:t0 + piece, 0:SSD_WIDTH] * d_ref[...]
    state_ref[...] = jnp.zeros(state_ref.shape, F32)

    ncc, nct = n_ctx // SSD_CHUNK, n_tot // SSD_CHUNK

    def step(i, carry):
        _ssd_chunk(xact_ref, dts_ref, dta_ref, state_ref, y_ref, i, 0)
        cb_idx = jnp.where(i < ncc, ncc - 1 - i, nct - 1 - (i - ncc))
        _ssd_chunk(xact_ref, dts_ref, dta_ref, state_ref, y_ref, cb_idx, 1)
        return carry

    lax.fori_loop(0, nct, step, 0)

    for seg_ref, o_ref, n, base in ((zx_c, out_c, n_ctx, 0), (zx_l, out_l, n_lat, n_ctx)):
        if o_ref is None:
            continue
        for t0 in range(0, n, piece):
            m = min(piece, n - t0)
            v = y_ref[base + t0:base + t0 + m, :] * _silu(seg_ref[0, t0:t0 + m, 0:SSD_WIDTH])
            o_ref[0, t0:t0 + m, :] = _rms(v, nw_ref[...]).astype(BF16)


def _ssd(zx_l, dt_l, zx_c, dt_c, conv_w, conv_b, dt_bias, a_log, d_skip, norm_w, need_ctx):
    bsz, n_lat, wzx = zx_l.shape
    n_ctx = zx_c.shape[1]
    n_tot = n_ctx + n_lat
    pad8 = lambda v: jnp.pad(v.reshape(1, -1), ((0, 0), (0, LANES - v.size)))
    per_b = lambda b: (b, 0, 0)
    const = lambda b: (0, 0)
    params = [conv_w, conv_b.reshape(1, -1), pad8(dt_bias), pad8(a_log),
              jnp.repeat(d_skip, SSD_HEAD_DIM).reshape(1, -1), norm_w.reshape(1, -1)]
    in_specs = [pl.BlockSpec((1, n_lat, wzx), per_b), pl.BlockSpec((1, n_lat, LANES), per_b),
                pl.BlockSpec((1, n_ctx, wzx), per_b), pl.BlockSpec((1, n_ctx, LANES), per_b)]
    in_specs += [pl.BlockSpec(p.shape, const) for p in params]
    out_specs = [pl.BlockSpec((1, n_lat, SSD_WIDTH), per_b)]
    out_shape = [jax.ShapeDtypeStruct((bsz, n_lat, SSD_WIDTH), BF16)]
    if need_ctx:
        out_specs.append(pl.BlockSpec((1, n_ctx, SSD_WIDTH), per_b))
        out_shape.append(jax.ShapeDtypeStruct((bsz, n_ctx, SSD_WIDTH), BF16))
    scratch = [pltpu.VMEM((max(n_lat, n_ctx) + 2 * SUBLANES, SSD_XBC), F32),
               pltpu.VMEM((n_tot, SSD_XBC), F32),
               pltpu.VMEM((n_tot, LANES), F32),
               pltpu.VMEM((n_tot, LANES), F32),
               pltpu.VMEM((n_tot, SSD_WIDTH), F32),
               pltpu.VMEM((2 * SSD_HEADS, SSD_STATE, SSD_HEAD_DIM), F32)]
    outs = pl.pallas_call(
        functools.partial(_ssd_kernel, n_ctx=n_ctx, n_lat=n_lat, need_ctx=need_ctx),
        grid=(bsz,),
        in_specs=in_specs,
        out_specs=out_specs,
        out_shape=out_shape,
        scratch_shapes=scratch,
        compiler_params=_cparams(1),
        name="ssd_mixer",
    )(zx_l, dt_l, zx_c, dt_c, *params)
    return (outs[0], outs[1]) if need_ctx else (outs[0], None)


def _attn_kernel(*refs, n_kv, lam_init):
    q_ref = refs[0]
    k_refs = refs[1:1 + n_kv]
    v_refs = refs[1 + n_kv:1 + 2 * n_kv]
    lq1, lk1, lq2, lk2, sw_ref, o_ref = refs[1 + 2 * n_kv:]
    q = q_ref[0]
    lane = lax.broadcasted_iota(jnp.int32, q.shape, 1)
    lam = (jnp.exp(jnp.sum(lq1[...] * lk1[...], axis=-1, keepdims=True))
           - jnp.exp(jnp.sum(lq2[...] * lk2[...], axis=-1, keepdims=True)) + lam_init)
    probs = []
    for comp in range(2):
        own = (lane >= DIFF_HEAD_DIM) if comp else (lane < DIFF_HEAD_DIM)
        qc = jnp.where(own, q, jnp.zeros_like(q))
        s = [lax.dot_general(qc, k_ref[0], (((1,), (1,)), ((), ())), preferred_element_type=F32)
             for k_ref in k_refs]
        m = functools.reduce(jnp.maximum, [jnp.max(p, axis=-1, keepdims=True) for p in s])
        e = [jnp.exp(p - m) for p in s]
        l = functools.reduce(jnp.add, [jnp.sum(p, axis=-1, keepdims=True) for p in e])
        probs.append((e, 1.0 / l))
    r0 = probs[0][1]
    r1 = probs[1][1] * lam
    o = None
    for i in range(n_kv):
        w = probs[0][0][i] * r0 - probs[1][0][i] * r1
        part = jnp.dot(w.astype(BF16), v_refs[i][0], preferred_element_type=F32)
        o = part if o is None else o + part
    o_ref[0] = (_rms(o, sw_ref[...]) * (1.0 - lam_init)).astype(BF16)


def _attention(q, ks, vs, lam_params, subln_w, lam_init):
    bsz, n, _ = q.shape
    tq = _row_tile(n)
    n_kv = len(ks)
    hw = 2 * DIFF_HEAD_DIM
    in_specs = [pl.BlockSpec((1, tq, hw), lambda b, h, i: (b, i, h))]
    in_specs += [pl.BlockSpec((1, a.shape[1], hw), lambda b, h, i: (b, 0, h)) for a in (*ks, *vs)]
    in_specs += [pl.BlockSpec((1, DIFF_HEAD_DIM), lambda b, h, i: (0, 0))] * 4
    in_specs += [pl.BlockSpec((1, hw), lambda b, h, i: (0, 0))]
    return pl.pallas_call(
        functools.partial(_attn_kernel, n_kv=n_kv, lam_init=lam_init),
        grid=(bsz, DIFF_HEADS, n // tq),
        in_specs=in_specs,
        out_specs=pl.BlockSpec((1, tq, hw), lambda b, h, i: (b, i, h)),
        out_shape=jax.ShapeDtypeStruct((bsz, n, DIFF_WIDTH), BF16),
        compiler_params=_cparams(3),
        name="diff_attention",
    )(q, *ks, *vs, *[p.reshape(1, -1) for p in lam_params], subln_w.reshape(1, -1))


def _s5_matrices(lam_re, lam_im, log_step, b_re, b_im, c_re, c_im):
    t = S5_T
    step = jnp.exp(log_step.astype(F32))[..., None]
    lr, li = lam_re.astype(F32), lam_im.astype(F32)
    mag = jnp.exp(lr * step)
    a = (mag * jnp.cos(li * step)) + 1j * (mag * jnp.sin(li * step))
    lam = lr + 1j * li
    bb = ((a - 1.0) / lam)[..., None] * (b_re.astype(F32) + 1j * b_im.astype(F32))
    cc = c_re.astype(F32) + 1j * c_im.astype(F32)
    pw = [jnp.ones_like(a)]
    for _ in range(t):
        pw.append(pw[-1] * a)
    pw = jnp.stack(pw, axis=0)
    kern = jnp.real(jnp.einsum('dgcn,tdgn,dgnk->dgtck', cc, pw[:t], bb))
    s_idx = np.arange(t)[:, None]
    t_idx = np.arange(t)[None, :]
    lag = t_idx - s_idx
    kf = jnp.where((lag >= 0)[None, :, :, None, None], kern[0][:, np.abs(lag)], 0.0)
    kb = jnp.where((lag <= 0)[None, :, :, None, None], kern[1][:, np.abs(lag)], 0.0)
    k_full = (kf + kb).transpose(0, 1, 4, 2, 3).reshape(S5_GROUPS, t * S5_GROUP, t * S5_GROUP)
    wf = pw[t - 1 - np.arange(t), 0][..., None] * bb[0][None]
    wb = pw[np.arange(t), 1][..., None] * bb[1][None]
    def rows_sc(w):
        return w.transpose(1, 0, 3, 2).reshape(S5_GROUPS, t * S5_GROUP, S5_STATE)
    w_state = jnp.concatenate([rows_sc(jnp.real(wf)), rows_sc(jnp.real(wb)),
                               rows_sc(jnp.imag(wf)), rows_sc(jnp.imag(wb))], axis=-1)
    cf = cc[0][None] * pw[1 + np.arange(t), 0][:, :, None, :]
    cbk = cc[1][None] * pw[t - np.arange(t), 1][:, :, None, :]
    def cols_tc(w):
        return w.transpose(1, 3, 0, 2).reshape(S5_GROUPS, S5_STATE, t * S5_GROUP)
    c_off = jnp.concatenate([cols_tc(jnp.real(cf)), cols_tc(jnp.real(cbk)),
                             cols_tc(-jnp.imag(cf)), cols_tc(-jnp.imag(cbk))], axis=1)
    at = pw[t]
    a_rows = jnp.stack([jnp.concatenate([jnp.real(at[0]), jnp.real(at[1])], axis=-1),
                        jnp.concatenate([jnp.imag(at[0]), jnp.imag(at[1])], axis=-1)], axis=1)
    a_chunk = jnp.pad(a_rows, ((0, 0), (0, SUBLANES - 2), (0, 0)))
    return k_full.astype(BF16), w_state.astype(BF16), c_off.astype(BF16), a_chunk.astype(F32)


def _s5_kernel(u_ref, kf_ref, ws_ref, co_ref, a_ref, y_ref, e_ref, h_ref, *, ncc):
    nct, bsz, w = u_ref.shape[1:]
    u = u_ref[0].reshape(nct * bsz, w)
    e_ref[...] = jnp.dot(u, ws_ref[0], preferred_element_type=F32).reshape(nct, bsz, w)
    ar = a_ref[0, 0:1, :]
    ai = a_ref[0, 1:2, :]
    half = S5_STATE
    lane = lax.broadcasted_iota(jnp.int32, (bsz, LANES), 1)
    is_f = lane < half

    def step(i, carry):
        h_re, h_im = carry
        jf = i
        jb = jnp.where(i < ncc, ncc - 1 - i, nct - 1 - (i - ncc))
        h_ref[jf, :, 0:half] = h_re[:, 0:half]
        h_ref[jb, :, half:LANES] = h_re[:, half:]
        h_ref[jf, :, LANES:LANES + half] = h_im[:, 0:half]
        h_ref[jb, :, LANES + half:] = h_im[:, half:]
        e_re = jnp.where(is_f, e_ref[jf, :, 0:LANES], e_ref[jb, :, 0:LANES])
        e_im = jnp.where(is_f, e_ref[jf, :, LANES:], e_ref[jb, :, LANES:])
        return (ar * h_re - ai * h_im + e_re, ar * h_im + ai * h_re + e_im)

    zero = jnp.zeros((bsz, LANES), F32)
    lax.fori_loop(0, nct, step, (zero, zero))
    hs = h_ref[...].reshape(nct * bsz, w).astype(BF16)
    y = (jnp.dot(u, kf_ref[0], preferred_element_type=F32)
         + jnp.dot(hs, co_ref[0], preferred_element_type=F32))
    y_ref[0] = y.reshape(nct, bsz, w)


def _s5_to_chunks(u):
    bsz, n, _ = u.shape
    u = u.reshape(bsz, n // S5_T, S5_T, S5_GROUPS, S5_GROUP).transpose(3, 1, 0, 2, 4)
    return u.reshape(S5_GROUPS, n // S5_T, bsz, S5_T * S5_GROUP)


def _s5_from_chunks(y, bsz):
    nchunk = y.shape[1]
    y = y.reshape(S5_GROUPS, nchunk, bsz, S5_T, S5_GROUP).transpose(2, 1, 3, 0, 4)
    return y.reshape(bsz, nchunk * S5_T, S5_WIDTH)


def _s5(u_l, u_c, mats):
    k_full, w_state, c_off, a_chunk = mats
    bsz = u_l.shape[0]
    ncc = u_c.shape[1] // S5_T
    u = jnp.concatenate([_s5_to_chunks(u_c), _s5_to_chunks(u_l)], axis=1).astype(BF16)
    nct, w = u.shape[1], u.shape[3]
    blk = lambda g: (g, 0, 0, 0)
    mat = lambda g: (g, 0, 0)
    y = pl.pallas_call(
        functools.partial(_s5_kernel, ncc=ncc),
        grid=(S5_GROUPS,),
        in_specs=[pl.BlockSpec((1, nct, bsz, w), blk),
                  pl.BlockSpec((1, w, w), mat), pl.BlockSpec((1, w, w), mat),
                  pl.BlockSpec((1, w, w), mat), pl.BlockSpec((1, SUBLANES, LANES), mat)],
        out_specs=pl.BlockSpec((1, nct, bsz, w), blk),
        out_shape=jax.ShapeDtypeStruct((S5_GROUPS, nct, bsz, w), F32),
        scratch_shapes=[pltpu.VMEM((nct, bsz, w), F32), pltpu.VMEM((nct, bsz, w), F32)],
        compiler_params=_cparams(1),
        name="s5_mixer",
    )(u, k_full, w_state, c_off, a_chunk)
    return _s5_from_chunks(y[:, ncc:], bsz), _s5_from_chunks(y[:, :ncc], bsz)


def _out_proj_kernel(a_ref, b_ref, ys_ref, u_ref, x_ref, gate_ref, d_ref, gw_ref, gb_ref,
                     w_ref, nw_ref, o_ref):
    y = jax.nn.gelu(ys_ref[0] + d_ref[...] * u_ref[0], approximate=True)
    gate = jnp.dot(y.astype(BF16), gw_ref[...], preferred_element_type=F32) + gb_ref[...]
    s = (y * jax.nn.sigmoid(gate)).astype(BF16)
    o1, o2 = SSD_WIDTH, SSD_WIDTH + DIFF_WIDTH
    mix = (jnp.dot(a_ref[0], w_ref[0:o1, :], preferred_element_type=F32)
           + jnp.dot(b_ref[0], w_ref[o1:o2, :], preferred_element_type=F32)
           + jnp.dot(s, w_ref[o2:, :], preferred_element_type=F32))
    o_ref[0] = x_ref[0] + gate_ref[0] * _rms(mix, nw_ref[...])


def _out_proj(a, b, ys, u, x, gate, s5_d, glu_w, glu_b, w_out, norm_w):
    bsz, n, d = x.shape
    tm = _row_tile(n)
    row = lambda bi, i: (bi, i, 0)
    per_b = lambda bi, i: (bi, 0, 0)
    const = lambda bi, i: (0, 0)
    return pl.pallas_call(
        _out_proj_kernel,
        grid=(bsz, n // tm),
        in_specs=[pl.BlockSpec((1, tm, SSD_WIDTH), row), pl.BlockSpec((1, tm, DIFF_WIDTH), row),
                  pl.BlockSpec((1, tm, S5_WIDTH), row), pl.BlockSpec((1, tm, S5_WIDTH), row),
                  pl.BlockSpec((1, tm, d), row), pl.BlockSpec((1, 1, d), per_b),
                  pl.BlockSpec((1, S5_WIDTH), const), pl.BlockSpec((S5_WIDTH, S5_WIDTH), const),
                  pl.BlockSpec((1, S5_WIDTH), const), pl.BlockSpec((d, d), const),
                  pl.BlockSpec((1, d), const)],
        out_specs=pl.BlockSpec((1, tm, d), row),
        out_shape=jax.ShapeDtypeStruct((bsz, n, d), F32),
        compiler_params=_cparams(2),
        name="out_proj",
    )(a, b, ys, u, x, gate, s5_d.reshape(1, -1), glu_w.astype(BF16), glu_b.reshape(1, -1),
      w_out.astype(BF16), norm_w.reshape(1, -1))


def _ffn_kernel(x_ref, xp_ref, xn_ref, sh_ref, sc_ref, gate_ref, nw1_ref, nw2_ref,
                wg_ref, wu_ref, cw_ref, cb_ref, wd_ref, o_ref, acc_ref):
    i = pl.program_id(1)
    is_first = i == 0
    is_last = i == pl.num_programs(1) - 1
    tm = x_ref.shape[1]

    def pre(xx):
        return (_rms(xx, nw1_ref[...]) * (1.0 + sc_ref[0]) + sh_ref[0]).astype(BF16)

    x = x_ref[0]
    h = pre(x)
    hp = pre(xp_ref[0])
    hn = pre(xn_ref[0])
    row = lax.broadcasted_iota(jnp.int32, (tm, 1), 0)
    acc_ref[...] = jnp.zeros(acc_ref.shape, F32)

    def chunk(j, carry):
        wg = wg_ref[j]
        g = jnp.dot(h, wg, preferred_element_type=F32)
        gp = jnp.dot(hp, wg, preferred_element_type=F32)[SUBLANES - 1:SUBLANES, :]
        gn = jnp.dot(hn, wg, preferred_element_type=F32)[0:1, :]
        gp = jnp.where(is_first, 0.0, gp)
        gn = jnp.where(is_last, 0.0, gn)
        g_prev = jnp.where(row == 0, gp, pltpu.roll(g, 1, 0))
        g_next = jnp.where(row == tm - 1, gn, pltpu.roll(g, tm - 1, 0))
        cw = cw_ref[j]
        conv = cw[0:1, :] * g_prev + cw[1:2, :] * g + cw[2:3, :] * g_next + cb_ref[j]
        act = _silu(conv) * jnp.dot(h, wu_ref[j], preferred_element_type=F32)
        acc_ref[...] += jnp.dot(act.astype(BF16), wd_ref[j], preferred_element_type=F32)
        return carry

    lax.fori_loop(0, wg_ref.shape[0], chunk, 0)
    o_ref[0] = x + gate_ref[0] * _rms(acc_ref[...], nw2_ref[...])


def _ffn(x, shift, scale, gate, norm_pre, norm_post, w_gate, w_up, conv_w, conv_b, w_down):
    bsz, n, d = x.shape
    tm = _row_tile(n)
    f = w_gate.shape[1]
    nc = f // FFN_CHUNK
    nb8 = n // SUBLANES
    tb8 = tm // SUBLANES
    row = lambda b, i: (b, i, 0)
    per_b = lambda b, i: (b, 0, 0)
    const2 = lambda b, i: (0, 0)
    const3 = lambda b, i: (0, 0, 0)
    wg3 = w_gate.astype(BF16).reshape(d, nc, FFN_CHUNK).transpose(1, 0, 2)
    wu3 = w_up.astype(BF16).reshape(d, nc, FFN_CHUNK).transpose(1, 0, 2)
    wd3 = w_down.astype(BF16).reshape(nc, FFN_CHUNK, d)
    cw3 = jnp.pad(conv_w, ((0, SUBLANES - FFN_CONV), (0, 0))).reshape(SUBLANES, nc, FFN_CHUNK).transpose(1, 0, 2)
    cb3 = conv_b.reshape(nc, 1, FFN_CHUNK)
    return pl.pallas_call(
        _ffn_kernel,
        grid=(bsz, n // tm),
        in_specs=[pl.BlockSpec((1, tm, d), row),
                  pl.BlockSpec((1, SUBLANES, d), lambda b, i: (b, jnp.maximum(i * tb8 - 1, 0), 0)),
                  pl.BlockSpec((1, SUBLANES, d), lambda b, i: (b, jnp.minimum((i + 1) * tb8, nb8 - 1), 0)),
                  pl.BlockSpec((1, 1, d), per_b), pl.BlockSpec((1, 1, d), per_b),
                  pl.BlockSpec((1, 1, d), per_b),
                  pl.BlockSpec((1, d), const2), pl.BlockSpec((1, d), const2),
                  pl.BlockSpec((nc, d, FFN_CHUNK), const3), pl.BlockSpec((nc, d, FFN_CHUNK), const3),
                  pl.BlockSpec((nc, SUBLANES, FFN_CHUNK), const3), pl.BlockSpec((nc, 1, FFN_CHUNK), const3),
                  pl.BlockSpec((nc, FFN_CHUNK, d), const3)],
        out_specs=pl.BlockSpec((1, tm, d), row),
        out_shape=jax.ShapeDtypeStruct((bsz, n, d), F32),
        scratch_shapes=[pltpu.VMEM((tm, d), F32)],
        compiler_params=_cparams(2),
        name="conv_ffn",
    )(x, x, x, shift, scale, gate, norm_pre.reshape(1, -1), norm_post.reshape(1, -1),
      wg3, wu3, cw3, cb3, wd3)


def kernel(x, c, ctx, c_ctx, mod_w, mod_b, mix_norm_pre, mix_norm_post, ffn_norm_pre, ffn_norm_post, w_in, w_out, ssd_conv_w, ssd_conv_b, ssd_dt_bias, ssd_a_log, ssd_d, ssd_norm_w, diff_lam_q1, diff_lam_k1, diff_lam_q2, diff_lam_k2, diff_subln_w, s5_lam_re, s5_lam_im, s5_log_step, s5_b_re, s5_b_im, s5_c_re, s5_c_im, s5_d, s5_glu_w, s5_glu_b, ffn_w_gate, ffn_w_up, ffn_conv_w, ffn_conv_b, ffn_w_down):
    bsz, n_lat, d = x.shape
    n_layers = mod_w.shape[0]
    rope_tabs = _rope_tables(n_lat)
    cc = jnp.concatenate([c, jnp.broadcast_to(c_ctx[None, :], (SUBLANES, d))], axis=0)
    mod = _modulation(cc, mod_w, mod_b)

    for layer in range(n_layers):
        need_ctx = layer < n_layers - 1
        lam_init = 0.8 - 0.6 * math.exp(-0.3 * layer)
        mod_l = mod[layer, :bsz].reshape(bsz, N_MOD, 1, d)
        mod_c = jnp.broadcast_to(mod[layer, bsz].reshape(1, N_MOD, 1, d), (bsz, N_MOD, 1, d))
        sh_a, sc_a, g_a, sh_f, sc_f, g_f = (mod_l[:, i] for i in range(N_MOD))
        csh_a, csc_a, cg_a, csh_f, csc_f, cg_f = (mod_c[:, i] for i in range(N_MOD))

        w_merged = _merge_w_in(w_in[layer])
        zx_l, dt_l, q_l, k_l, v_l, u_l = _in_proj(x, sh_a, sc_a, mix_norm_pre[layer], w_merged, rope_tabs)
        zx_c, dt_c, q_c, k_c, v_c, u_c = _in_proj(ctx, csh_a, csc_a, mix_norm_pre[layer], w_merged, None)

        a_l, a_c = _ssd(zx_l, dt_l, zx_c, dt_c, ssd_conv_w[layer], ssd_conv_b[layer],
                        ssd_dt_bias[layer], ssd_a_log[layer], ssd_d[layer], ssd_norm_w[layer], need_ctx)

        lam_params = (diff_lam_q1[layer], diff_lam_k1[layer], diff_lam_q2[layer], diff_lam_k2[layer])
        b_l = _attention(q_l, (k_c, k_l), (v_c, v_l), lam_params, diff_subln_w[layer], lam_init)

        mats = _s5_matrices(s5_lam_re[layer], s5_lam_im[layer], s5_log_step[layer], s5_b_re[layer],
                            s5_b_im[layer], s5_c_re[layer], s5_c_im[layer])
        ys_l, ys_c = _s5(u_l, u_c, mats)

        mix_p = (s5_d[layer], s5_glu_w[layer], s5_glu_b[layer], w_out[layer], mix_norm_post[layer])
        ffn_p = (ffn_norm_pre[layer], ffn_norm_post[layer], ffn_w_gate[layer], ffn_w_up[layer],
                 ffn_conv_w[layer], ffn_conv_b[layer], ffn_w_down[layer])
        x = _out_proj(a_l, b_l, ys_l, u_l, x, g_a, *mix_p)
        x = _ffn(x, sh_f, sc_f, g_f, *ffn_p)
        if need_ctx:
            b_c = _attention(q_c, (k_c,), (v_c,), lam_params, diff_subln_w[layer], lam_init)
            ctx = _out_proj(a_c, b_c, ys_c, u_c, ctx, cg_a, *mix_p)
            ctx = _ffn(ctx, csh_f, csc_f, cg_f, *ffn_p)
    return x
```

```python
import functools
import math

import numpy as np
import jax
import jax.numpy as jnp
from jax import lax
from jax.experimental import pallas as pl
from jax.experimental.pallas import tpu as pltpu

F32 = jnp.float32
BF16 = jnp.bfloat16
HI = lax.Precision.HIGHEST

D_MODEL = 1024
N_LAYERS = 2
GRID_W = 64
EPS = 1e-6
N_MOD = 6
SSD_HEADS = 4
SSD_HEAD_DIM = 64
SSD_WIDTH = 256
SSD_GROUPS = 2
SSD_STATE = 64
SSD_CONV = 5
SSD_CHUNK = 128
SSD_XBC = 512
SSD_IN = 776
DIFF_HEADS = 4
DIFF_HEAD_DIM = 64
DIFF_WIDTH = 512
DIFF_IN = 1536
ROPE_THETA = 10000.0
ATTN_Q_TILE = 1024
ATTN_ROW_SPLIT = 256
S5_GROUP = 16
S5_WIDTH = 256
S5_GROUPS = 16
S5_STATE = 64
S5_T = 16
FFN_DIM = 2816
FFN_CHUNK = 256
FFN_CONV = 3

LANES = 128
SUBLANES = 8
W_IN_PAD = 2688
VMEM_LIMIT = 56 * 1024 * 1024


def _cparams(n_axes):
    return pltpu.CompilerParams(dimension_semantics=("parallel",) * n_axes,
                                vmem_limit_bytes=VMEM_LIMIT)


def _rms(x, w):
    return x * lax.rsqrt(jnp.mean(x * x, axis=-1, keepdims=True) + EPS) * w


def _silu(x):
    return x * jax.nn.sigmoid(x)


def _row_tile(n):
    return min(512, n)


def _mod_kernel(c_ref, w_ref, b_ref, o_ref):
    a = _silu(c_ref[...])
    o_ref[0] = jnp.dot(a, w_ref[0], preferred_element_type=F32, precision=HI) + b_ref[0]


def _modulation(cc, mod_w, mod_b):
    nl, d, n = mod_w.shape
    r = cc.shape[0]
    tn = 512
    return pl.pallas_call(
        _mod_kernel,
        grid=(nl, n // tn),
        in_specs=[pl.BlockSpec((r, d), lambda l, j: (0, 0)),
                  pl.BlockSpec((1, d, tn), lambda l, j: (l, 0, j)),
                  pl.BlockSpec((1, 1, tn), lambda l, j: (l, 0, j))],
        out_specs=pl.BlockSpec((1, r, tn), lambda l, j: (l, 0, j)),
        out_shape=jax.ShapeDtypeStruct((nl, r, n), F32),
        compiler_params=_cparams(2),
        name="modulation",
    )(cc, mod_w, mod_b.reshape(nl, 1, n))


def _in_proj_kernel(*refs, rope):
    if rope:
        (x_ref, sh_ref, sc_ref, nw_ref, w_ref, cos_ref, sin_ref,
         zx_ref, dt_ref, q_ref, k_ref, v_ref, u_ref) = refs
    else:
        (x_ref, sh_ref, sc_ref, nw_ref, w_ref,
         zx_ref, dt_ref, q_ref, k_ref, v_ref, u_ref) = refs
    h = _rms(x_ref[0], nw_ref[...]) * (1.0 + sc_ref[0]) + sh_ref[0]
    hb = h.astype(BF16)

    def mm(lo, hi):
        return jnp.dot(hb, w_ref[:, lo:hi], preferred_element_type=F32)

    def rot(p):
        lane = lax.broadcasted_iota(jnp.int32, p.shape, 1)
        partner = jnp.where(lane % 32 < 16, pltpu.roll(p, LANES - 16, 1), pltpu.roll(p, 16, 1))
        return p * cos_ref[...] + partner * sin_ref[...]

    o_zx = SSD_WIDTH + SSD_XBC
    zx_ref[0] = mm(0, o_zx)
    for j in range(DIFF_WIDTH // LANES):
        q = mm(o_zx + j * LANES, o_zx + (j + 1) * LANES) * (DIFF_HEAD_DIM ** -0.5)
        k = mm(o_zx + DIFF_WIDTH + j * LANES, o_zx + DIFF_WIDTH + (j + 1) * LANES)
        if rope:
            q, k = rot(q), rot(k)
        q_ref[0, :, j * LANES:(j + 1) * LANES] = q.astype(BF16)
        k_ref[0, j * LANES:(j + 1) * LANES, :] = k.T.astype(BF16)
    o_v = o_zx + 2 * DIFF_WIDTH
    v_ref[0] = mm(o_v, o_v + DIFF_WIDTH).astype(BF16)
    o_u = o_v + DIFF_WIDTH
    u_ref[0] = mm(o_u, o_u + S5_WIDTH)
    dt_ref[0] = mm(o_u + S5_WIDTH, W_IN_PAD)


def _in_proj(x, shift, scale, norm_w, w_merged, rope_tabs):
    bsz, n, d = x.shape
    tm = _row_tile(n)
    rope = rope_tabs is not None
    row = lambda b, i: (b, i, 0)
    per_b = lambda b, i: (b, 0, 0)
    const = lambda b, i: (0, 0)
    in_specs = [pl.BlockSpec((1, tm, d), row),
                pl.BlockSpec((1, 1, d), per_b),
                pl.BlockSpec((1, 1, d), per_b),
                pl.BlockSpec((1, d), const),
                pl.BlockSpec((d, W_IN_PAD), const)]
    args = [x, shift, scale, norm_w.reshape(1, d), w_merged]
    if rope:
        in_specs += [pl.BlockSpec((tm, LANES), lambda b, i: (i, 0))] * 2
        args += list(rope_tabs)
    widths = (SSD_WIDTH + SSD_XBC, LANES, DIFF_WIDTH, DIFF_WIDTH, DIFF_WIDTH, S5_WIDTH)
    dtypes = (F32, F32, BF16, BF16, BF16, F32)
    out_specs = [pl.BlockSpec((1, tm, w), row) for w in widths]
    out_shape = [jax.ShapeDtypeStruct((bsz, n, w), t) for w, t in zip(widths, dtypes)]
    out_specs[3] = pl.BlockSpec((1, DIFF_WIDTH, tm), lambda b, i: (b, 0, i))
    out_shape[3] = jax.ShapeDtypeStruct((bsz, DIFF_WIDTH, n), BF16)
    return pl.pallas_call(
        functools.partial(_in_proj_kernel, rope=rope),
        grid=(bsz, n // tm),
        in_specs=in_specs,
        out_specs=out_specs,
        out_shape=out_shape,
        compiler_params=_cparams(2),
        name="in_proj_rope" if rope else "in_proj",
    )(*args)


def _merge_w_in(w):
    o1, o2 = SSD_IN, SSD_IN + DIFF_IN
    parts = [w[:, :SSD_WIDTH + SSD_XBC], w[:, o1:o2], w[:, o2:], w[:, SSD_WIDTH + SSD_XBC:o1]]
    m = jnp.concatenate(parts, axis=1)
    return jnp.pad(m, ((0, 0), (0, W_IN_PAD - m.shape[1]))).astype(BF16)


def _rope_tables(n):
    t = jnp.arange(n, dtype=jnp.int32)
    pos = jnp.stack([(t // GRID_W).astype(F32), (t % GRID_W).astype(F32)], axis=1)
    quarter = DIFF_HEAD_DIM // 4
    inv_freq = ROPE_THETA ** (-jnp.arange(quarter, dtype=F32) / quarter)
    lane = np.arange(LANES)
    axis = (lane % DIFF_HEAD_DIM) // (DIFF_HEAD_DIM // 2)
    freq = lane % quarter
    ang = pos[:, axis] * inv_freq[freq][None, :]
    sign = np.where(lane % (2 * quarter) < quarter, -1.0, 1.0).astype(np.float32)
    return jnp.cos(ang), jnp.sin(ang) * sign[None, :]


def _ssd_chunk(xact_ref, dt_ref, dta_ref, state_ref, y_ref, c, direction):
    q = SSD_CHUNK
    r0 = pl.multiple_of(c * q, q)
    xa = xact_ref[pl.ds(r0, q), :]
    dt = dt_ref[pl.ds(r0, q), :]
    dta = dta_ref[pl.ds(r0, q), :]
    rows = lax.broadcasted_iota(jnp.int32, (q, q), 0)
    cols = lax.broadcasted_iota(jnp.int32, (q, q), 1)
    mask = (rows >= cols) if direction == 0 else (rows <= cols)
    acs = jnp.dot(mask.astype(F32), dta, preferred_element_type=F32, precision=HI)
    acs_t = acs.T
    bm = xa[:, SSD_WIDTH:SSD_WIDTH + LANES]
    cm = xa[:, SSD_WIDTH + LANES:]
    bm_t = bm.T
    last = q - 1 if direction == 0 else 0
    ys = []
    cb = [None] * SSD_GROUPS
    for h in range(SSD_HEADS):
        g = h // (SSD_HEADS // SSD_GROUPS)
        j = direction * SSD_HEADS + h
        cg = cm[:, g * SSD_STATE:(g + 1) * SSD_STATE].astype(BF16)
        if cb[g] is None:
            bg = bm[:, g * SSD_STATE:(g + 1) * SSD_STATE].astype(BF16)
            cb[g] = lax.dot_general(cg, bg, (((1,), (1,)), ((), ())), preferred_element_type=F32)
        col = acs[:, j:j + 1]
        row = acs_t[j:j + 1, :]
        decay = jnp.exp(jnp.where(mask, col - row, -jnp.inf))
        xdt = (xa[:, h * SSD_HEAD_DIM:(h + 1) * SSD_HEAD_DIM] * dt[:, j:j + 1]).astype(BF16)
        y_diag = jnp.dot((cb[g] * decay).astype(BF16), xdt, preferred_element_type=F32)
        state = state_ref[j]
        y_off = jnp.dot(cg, state.astype(BF16), preferred_element_type=F32) * jnp.exp(col)
        ys.append(y_diag + y_off)
        total = acs_t[j:j + 1, last:last + 1]
        b_dec = bm_t[g * SSD_STATE:(g + 1) * SSD_STATE, :] * jnp.exp(total - row)
        state_ref[j] = jnp.exp(total) * state + jnp.dot(b_dec.astype(BF16), xdt,
                                                        preferred_element_type=F32)
    y_ref[pl.ds(r0, q), :] += jnp.concatenate(ys, axis=-1)


def _ssd_kernel(*refs, n_ctx, n_lat, need_ctx):
    (zx_l, dt_l, zx_c, dt_c, cw_ref, cb_ref, dtb_ref, alog_ref, d_ref, nw_ref) = refs[:10]
    outs = refs[10:12] if need_ctx else (refs[10], None)
    pad_ref, xact_ref, dts_ref, dta_ref, y_ref, state_ref = refs[-6:]
    out_l, out_c = outs
    piece = 256
    halo = (SSD_CONV - 1) // 2

    for seg_ref, n, base in ((zx_c, n_ctx, 0), (zx_l, n_lat, n_ctx)):
        pad_ref[0:SUBLANES, :] = jnp.zeros((SUBLANES, SSD_XBC), F32)
        pad_ref[SUBLANES + n:2 * SUBLANES + n, :] = jnp.zeros((SUBLANES, SSD_XBC), F32)
        pad_ref[SUBLANES:SUBLANES + n, :] = seg_ref[0, :, SSD_WIDTH:]
        for t0 in range(0, n, piece):
            m = min(piece, n - t0)
            acc = jnp.broadcast_to(cb_ref[...], (m, SSD_XBC))
            for k in range(SSD_CONV):
                lo = SUBLANES + t0 + k - halo
                acc = acc + cw_ref[k:k + 1, :] * pad_ref[lo:lo + m, :]
            xact_ref[base + t0:base + t0 + m, :] = _silu(acc)
    a_neg = -jnp.exp(alog_ref[...])
    for seg_ref, n, base in ((dt_c, n_ctx, 0), (dt_l, n_lat, n_ctx)):
        dt = jax.nn.softplus(seg_ref[0] + dtb_ref[...])
        dts_ref[base:base + n, :] = dt
        dta_ref[base:base + n, :] = dt * a_neg
    n_tot = n_ctx + n_lat
    for t0 in range(0, n_tot, piece):
        y_ref[t0:t0 + piece, :] = xact_ref[t0:t0 + piece, 0:SSD_WIDTH] * d_ref[...]
    state_ref[...] = jnp.zeros(state_ref.shape, F32)

    ncc, nct = n_ctx // SSD_CHUNK, n_tot // SSD_CHUNK

    def step(i, carry):
        _ssd_chunk(xact_ref, dts_ref, dta_ref, state_ref, y_ref, i, 0)
        cb_idx = jnp.where(i < ncc, ncc - 1 - i, nct - 1 - (i - ncc))
        _ssd_chunk(xact_ref, dts_ref, dta_ref, state_ref, y_ref, cb_idx, 1)
        return carry

    lax.fori_loop(0, nct, step, 0)

    for seg_ref, o_ref, n, base in ((zx_c, out_c, n_ctx, 0), (zx_l, out_l, n_lat, n_ctx)):
        if o_ref is None:
            continue
        for t0 in range(0, n, piece):
            m = min(piece, n - t0)
            v = y_ref[base + t0:base + t0 + m, :] * _silu(seg_ref[0, t0:t0 + m, 0:SSD_WIDTH])
            o_ref[0, t0:t0 + m, :] = _rms(v, nw_ref[...]).astype(BF16)


def _ssd(zx_l, dt_l, zx_c, dt_c, conv_w, conv_b, dt_bias, a_log, d_skip, norm_w, need_ctx):
    bsz, n_lat, wzx = zx_l.shape
    n_ctx = zx_c.shape[1]
    n_tot = n_ctx + n_lat
    pad8 = lambda v: jnp.pad(v.reshape(1, -1), ((0, 0), (0, LANES - v.size)))
    per_b = lambda b: (b, 0, 0)
    const = lambda b: (0, 0)
    params = [conv_w, conv_b.reshape(1, -1), pad8(dt_bias), pad8(a_log),
              jnp.repeat(d_skip, SSD_HEAD_DIM).reshape(1, -1), norm_w.reshape(1, -1)]
    in_specs = [pl.BlockSpec((1, n_lat, wzx), per_b), pl.BlockSpec((1, n_lat, LANES), per_b),
                pl.BlockSpec((1, n_ctx, wzx), per_b), pl.BlockSpec((1, n_ctx, LANES), per_b)]
    in_specs += [pl.BlockSpec(p.shape, const) for p in params]
    out_specs = [pl.BlockSpec((1, n_lat, SSD_WIDTH), per_b)]
    out_shape = [jax.ShapeDtypeStruct((bsz, n_lat, SSD_WIDTH), BF16)]
    if need_ctx:
        out_specs.append(pl.BlockSpec((1, n_ctx, SSD_WIDTH), per_b))
        out_shape.append(jax.ShapeDtypeStruct((bsz, n_ctx, SSD_WIDTH), BF16))
    scratch = [pltpu.VMEM((max(n_lat, n_ctx) + 2 * SUBLANES, SSD_XBC), F32),
               pltpu.VMEM((n_tot, SSD_XBC), F32),
               pltpu.VMEM((n_tot, LANES), F32),
               pltpu.VMEM((n_tot, LANES), F32),
               pltpu.VMEM((n_tot, SSD_WIDTH), F32),
               pltpu.VMEM((2 * SSD_HEADS, SSD_STATE, SSD_HEAD_DIM), F32)]
    outs = pl.pallas_call(
        functools.partial(_ssd_kernel, n_ctx=n_ctx, n_lat=n_lat, need_ctx=need_ctx),
        grid=(bsz,),
        in_specs=in_specs,
        out_specs=out_specs,
        out_shape=out_shape,
        scratch_shapes=scratch,
        compiler_params=_cparams(1),
        name="ssd_mixer",
    )(zx_l, dt_l, zx_c, dt_c, *params)
    return (outs[0], outs[1]) if need_ctx else (outs[0], None)


def _attn_kernel(*refs, n_kv, lam_init):
    q_ref = refs[0]
    k_refs = refs[1:1 + n_kv]
    v_refs = refs[1 + n_kv:1 + 2 * n_kv]
    lq1, lk1, lq2, lk2, sw_ref, o_ref = refs[1 + 2 * n_kv:-4]
    s_refs, e_refs = refs[-4:-2], refs[-2:]
    q = q_ref[0]
    hw = q.shape[1]
    lane = lax.broadcasted_iota(jnp.int32, q.shape, 1)
    lam = (jnp.exp(jnp.sum(lq1[...] * lk1[...], axis=-1, keepdims=True))
           - jnp.exp(jnp.sum(lq2[...] * lk2[...], axis=-1, keepdims=True)) + lam_init)
    offs = np.cumsum([0] + [k_ref.shape[2] for k_ref in k_refs])
    v_ext = [jnp.concatenate([v_ref[0], jnp.ones(v_ref.shape[1:], BF16)], axis=-1) for v_ref in v_refs]
    tq = q.shape[0]
    halves = [(0, tq)] if tq < 2 * ATTN_ROW_SPLIT else [(0, tq // 2), (tq // 2, tq)]
    for comp, s_ref in enumerate(s_refs):
        own = (lane >= DIFF_HEAD_DIM) if comp else (lane < DIFF_HEAD_DIM)
        qc = jnp.where(own, q, jnp.zeros_like(q))
        for i, k_ref in enumerate(k_refs):
            s_ref[:, offs[i]:offs[i + 1]] = jnp.dot(qc, k_ref[0], preferred_element_type=F32)
    outs = []
    for s_ref, e_ref in zip(s_refs, e_refs):
        parts = []
        for lo, hi in halves:
            s = s_ref[lo:hi, :]
            e_ref[lo:hi, :] = jnp.exp(s - jnp.max(s, axis=-1, keepdims=True)).astype(BF16)
            ov = None
            for i in range(n_kv):
                part = jnp.dot(e_ref[lo:hi, offs[i]:offs[i + 1]], v_ext[i], preferred_element_type=F32)
                ov = part if ov is None else ov + part
            parts.append(ov[:, 0:hw] * (1.0 / ov[:, hw:hw + 1]))
        outs.append(parts)
    for (lo, hi), o0, o1 in zip(halves, *outs):
        o = o0 - lam * o1
        o_ref[0, lo:hi, :] = (_rms(o, sw_ref[...]) * (1.0 - lam_init)).astype(BF16)


def _attention(q, ks, vs, lam_params, subln_w, lam_init):
    bsz, n, _ = q.shape
    tq = min(ATTN_Q_TILE, n)
    n_kv = len(ks)
    n_keys = sum(a.shape[2] for a in ks)
    hw = 2 * DIFF_HEAD_DIM
    in_specs = [pl.BlockSpec((1, tq, hw), lambda b, h, i: (b, i, h))]
    in_specs += [pl.BlockSpec((1, hw, a.shape[2]), lambda b, h, i: (b, h, 0)) for a in ks]
    in_specs += [pl.BlockSpec((1, a.shape[1], hw), lambda b, h, i: (b, 0, h)) for a in vs]
    in_specs += [pl.BlockSpec((1, DIFF_HEAD_DIM), lambda b, h, i: (0, 0))] * 4
    in_specs += [pl.BlockSpec((1, hw), lambda b, h, i: (0, 0))]
    return pl.pallas_call(
        functools.partial(_attn_kernel, n_kv=n_kv, lam_init=lam_init),
        grid=(bsz, DIFF_HEADS, n // tq),
        in_specs=in_specs,
        out_specs=pl.BlockSpec((1, tq, hw), lambda b, h, i: (b, i, h)),
        out_shape=jax.ShapeDtypeStruct((bsz, n, DIFF_WIDTH), BF16),
        scratch_shapes=[pltpu.VMEM((tq, n_keys), F32)] * 2 + [pltpu.VMEM((tq, n_keys), BF16)] * 2,
        compiler_params=_cparams(3),
        name="diff_attention",
    )(q, *ks, *vs, *[p.reshape(1, -1) for p in lam_params], subln_w.reshape(1, -1))


def _s5_matrices(lam_re, lam_im, log_step, b_re, b_im, c_re, c_im):
    t = S5_T
    step = jnp.exp(log_step.astype(F32))[..., None]
    lr, li = lam_re.astype(F32), lam_im.astype(F32)
    mag = jnp.exp(lr * step)
    a = (mag * jnp.cos(li * step)) + 1j * (mag * jnp.sin(li * step))
    lam = lr + 1j * li
    bb = ((a - 1.0) / lam)[..., None] * (b_re.astype(F32) + 1j * b_im.astype(F32))
    cc = c_re.astype(F32) + 1j * c_im.astype(F32)
    pw = [jnp.ones_like(a)]
    for _ in range(t):
        pw.append(pw[-1] * a)
    pw = jnp.stack(pw, axis=0)
    kern = jnp.real(jnp.einsum('dgcn,tdgn,dgnk->dgtck', cc, pw[:t], bb))
    s_idx = np.arange(t)[:, None]
    t_idx = np.arange(t)[None, :]
    lag = t_idx - s_idx
    kf = jnp.where((lag >= 0)[None, :, :, None, None], kern[0][:, np.abs(lag)], 0.0)
    kb = jnp.where((lag <= 0)[None, :, :, None, None], kern[1][:, np.abs(lag)], 0.0)
    k_full = (kf + kb).transpose(0, 1, 4, 2, 3).reshape(S5_GROUPS, t * S5_GROUP, t * S5_GROUP)
    wf = pw[t - 1 - np.arange(t), 0][..., None] * bb[0][None]
    wb = pw[np.arange(t), 1][..., None] * bb[1][None]
    def rows_sc(w):
        return w.transpose(1, 0, 3, 2).reshape(S5_GROUPS, t * S5_GROUP, S5_STATE)
    w_state = jnp.concatenate([rows_sc(jnp.real(wf)), rows_sc(jnp.real(wb)),
                               rows_sc(jnp.imag(wf)), rows_sc(jnp.imag(wb))], axis=-1)
    cf = cc[0][None] * pw[1 + np.arange(t), 0][:, :, None, :]
    cbk = cc[1][None] * pw[t - np.arange(t), 1][:, :, None, :]
    def cols_tc(w):
        return w.transpose(1, 3, 0, 2).reshape(S5_GROUPS, S5_STATE, t * S5_GROUP)
    c_off = jnp.concatenate([cols_tc(jnp.real(cf)), cols_tc(jnp.real(cbk)),
                             cols_tc(-jnp.imag(cf)), cols_tc(-jnp.imag(cbk))], axis=1)
    at = pw[t]
    a_rows = jnp.stack([jnp.concatenate([jnp.real(at[0]), jnp.real(at[1])], axis=-1),
                        jnp.concatenate([jnp.imag(at[0]), jnp.imag(at[1])], axis=-1)], axis=1)
    a_chunk = jnp.pad(a_rows, ((0, 0), (0, SUBLANES - 2), (0, 0)))
    return k_full.astype(BF16), w_state.astype(BF16), c_off.astype(BF16), a_chunk.astype(F32)


def _s5_kernel(u_ref, kf_ref, ws_ref, co_ref, a_ref, y_ref, e_ref, h_ref, *, ncc):
    nct, bsz, w = u_ref.shape[1:]
    u = u_ref[0].reshape(nct * bsz, w)
    e_ref[...] = jnp.dot(u, ws_ref[0], preferred_element_type=F32).reshape(nct, bsz, w)
    ar = a_ref[0, 0:1, :]
    ai = a_ref[0, 1:2, :]
    half = S5_STATE
    lane = lax.broadcasted_iota(jnp.int32, (bsz, LANES), 1)
    is_f = lane < half

    def step(i, carry):
        h_re, h_im = carry
        jf = i
        jb = jnp.where(i < ncc, ncc - 1 - i, nct - 1 - (i - ncc))
        h_ref[jf, :, 0:half] = h_re[:, 0:half]
        h_ref[jb, :, half:LANES] = h_re[:, half:]
        h_ref[jf, :, LANES:LANES + half] = h_im[:, 0:half]
        h_ref[jb, :, LANES + half:] = h_im[:, half:]
        e_re = jnp.where(is_f, e_ref[jf, :, 0:LANES], e_ref[jb, :, 0:LANES])
        e_im = jnp.where(is_f, e_ref[jf, :, LANES:], e_ref[jb, :, LANES:])
        return (ar * h_re - ai * h_im + e_re, ar * h_im + ai * h_re + e_im)

    zero = jnp.zeros((bsz, LANES), F32)
    lax.fori_loop(0, nct, step, (zero, zero))
    hs = h_ref[...].reshape(nct * bsz, w).astype(BF16)
    y = (jnp.dot(u, kf_ref[0], preferred_element_type=F32)
         + jnp.dot(hs, co_ref[0], preferred_element_type=F32))
    y_ref[0] = y.reshape(nct, bsz, w)


def _s5_to_chunks(u):
    bsz, n, _ = u.shape
    u = u.reshape(bsz, n // S5_T, S5_T, S5_GROUPS, S5_GROUP).transpose(3, 1, 0, 2, 4)
    return u.reshape(S5_GROUPS, n // S5_T, bsz, S5_T * S5_GROUP)


def _s5_from_chunks(y, bsz):
    nchunk = y.shape[1]
    y = y.reshape(S5_GROUPS, nchunk, bsz, S5_T, S5_GROUP).transpose(2, 1, 3, 0, 4)
    return y.reshape(bsz, nchunk * S5_T, S5_WIDTH)


def _s5(u_l, u_c, mats):
    k_full, w_state, c_off, a_chunk = mats
    bsz = u_l.shape[0]
    ncc = u_c.shape[1] // S5_T
    u = jnp.concatenate([_s5_to_chunks(u_c), _s5_to_chunks(u_l)], axis=1).astype(BF16)
    nct, w = u.shape[1], u.shape[3]
    blk = lambda g: (g, 0, 0, 0)
    mat = lambda g: (g, 0, 0)
    y = pl.pallas_call(
        functools.partial(_s5_kernel, ncc=ncc),
        grid=(S5_GROUPS,),
        in_specs=[pl.BlockSpec((1, nct, bsz, w), blk),
                  pl.BlockSpec((1, w, w), mat), pl.BlockSpec((1, w, w), mat),
                  pl.BlockSpec((1, w, w), mat), pl.BlockSpec((1, SUBLANES, LANES), mat)],
        out_specs=pl.BlockSpec((1, nct, bsz, w), blk),
        out_shape=jax.ShapeDtypeStruct((S5_GROUPS, nct, bsz, w), F32),
        scratch_shapes=[pltpu.VMEM((nct, bsz, w), F32), pltpu.VMEM((nct, bsz, w), F32)],
        compiler_params=_cparams(1),
        name="s5_mixer",
    )(u, k_full, w_state, c_off, a_chunk)
    return _s5_from_chunks(y[:, ncc:], bsz), _s5_from_chunks(y[:, :ncc], bsz)


def _out_proj_kernel(a_ref, b_ref, ys_ref, u_ref, x_ref, gate_ref, d_ref, gw_ref, gb_ref,
                     w_ref, nw_ref, o_ref):
    y = jax.nn.gelu(ys_ref[0] + d_ref[...] * u_ref[0], approximate=True)
    gate = jnp.dot(y.astype(BF16), gw_ref[...], preferred_element_type=F32) + gb_ref[...]
    s = (y * jax.nn.sigmoid(gate)).astype(BF16)
    o1, o2 = SSD_WIDTH, SSD_WIDTH + DIFF_WIDTH
    mix = (jnp.dot(a_ref[0], w_ref[0:o1, :], preferred_element_type=F32)
           + jnp.dot(b_ref[0], w_ref[o1:o2, :], preferred_element_type=F32)
           + jnp.dot(s, w_ref[o2:, :], preferred_element_type=F32))
    o_ref[0] = x_ref[0] + gate_ref[0] * _rms(mix, nw_ref[...])


def _out_proj(a, b, ys, u, x, gate, s5_d, glu_w, glu_b, w_out, norm_w):
    bsz, n, d = x.shape
    tm = _row_tile(n)
    row = lambda bi, i: (bi, i, 0)
    per_b = lambda bi, i: (bi, 0, 0)
    const = lambda bi, i: (0, 0)
    return pl.pallas_call(
        _out_proj_kernel,
        grid=(bsz, n // tm),
        in_specs=[pl.BlockSpec((1, tm, SSD_WIDTH), row), pl.BlockSpec((1, tm, DIFF_WIDTH), row),
                  pl.BlockSpec((1, tm, S5_WIDTH), row), pl.BlockSpec((1, tm, S5_WIDTH), row),
                  pl.BlockSpec((1, tm, d), row), pl.BlockSpec((1, 1, d), per_b),
                  pl.BlockSpec((1, S5_WIDTH), const), pl.BlockSpec((S5_WIDTH, S5_WIDTH), const),
                  pl.BlockSpec((1, S5_WIDTH), const), pl.BlockSpec((d, d), const),
                  pl.BlockSpec((1, d), const)],
        out_specs=pl.BlockSpec((1, tm, d), row),
        out_shape=jax.ShapeDtypeStruct((bsz, n, d), F32),
        compiler_params=_cparams(2),
        name="out_proj",
    )(a, b, ys, u, x, gate, s5_d.reshape(1, -1), glu_w.astype(BF16), glu_b.reshape(1, -1),
      w_out.astype(BF16), norm_w.reshape(1, -1))


def _ffn_kernel(x_ref, xp_ref, xn_ref, sh_ref, sc_ref, gate_ref, nw1_ref, nw2_ref,
                wg_ref, wu_ref, cw_ref, cb_ref, wd_ref, o_ref, act_ref):
    i = pl.program_id(1)
    is_first = i == 0
    is_last = i == pl.num_programs(1) - 1
    tm = x_ref.shape[1]
    x = x_ref[0]
    x_ext = jnp.concatenate([x, xp_ref[0], xn_ref[0]], axis=0)
    h_ext = _rms(x_ext, nw1_ref[...]) * (1.0 + sc_ref[0]) + sh_ref[0]
    row_ext = lax.broadcasted_iota(jnp.int32, (tm + 2 * SUBLANES, 1), 0)
    outside = (((row_ext >= tm) & (row_ext < tm + SUBLANES) & is_first)
               | ((row_ext >= tm + SUBLANES) & is_last))
    h_ext = jnp.where(outside, 0.0, h_ext).astype(BF16)
    h = h_ext[0:tm]
    row = lax.broadcasted_iota(jnp.int32, (tm, 1), 0)
    for lo in range(0, wg_ref.shape[1], FFN_CHUNK):
        hi = lo + FFN_CHUNK
        g_ext = jnp.dot(h_ext, wg_ref[:, lo:hi], preferred_element_type=F32)
        g = g_ext[0:tm]
        g_prev = jnp.where(row == 0, g_ext[tm + SUBLANES - 1:tm + SUBLANES], pltpu.roll(g, 1, 0))
        g_next = jnp.where(row == tm - 1, g_ext[tm + SUBLANES:tm + SUBLANES + 1], pltpu.roll(g, tm - 1, 0))
        conv = (cw_ref[0:1, lo:hi] * g_prev + cw_ref[1:2, lo:hi] * g + cw_ref[2:3, lo:hi] * g_next
                + cb_ref[:, lo:hi])
        up = jnp.dot(h, wu_ref[:, lo:hi], preferred_element_type=F32)
        act_ref[:, lo:hi] = (_silu(conv) * up).astype(BF16)
    f = jnp.dot(act_ref[...], wd_ref[...], preferred_element_type=F32)
    o_ref[0] = x + gate_ref[0] * _rms(f, nw2_ref[...])


def _ffn(x, shift, scale, gate, norm_pre, norm_post, w_gate, w_up, conv_w, conv_b, w_down):
    bsz, n, d = x.shape
    tm = _row_tile(n)
    f = w_gate.shape[1]
    nb8 = n // SUBLANES
    tb8 = tm // SUBLANES
    row = lambda b, i: (b, i, 0)
    per_b = lambda b, i: (b, 0, 0)
    const = lambda b, i: (0, 0)
    return pl.pallas_call(
        _ffn_kernel,
        grid=(bsz, n // tm),
        in_specs=[pl.BlockSpec((1, tm, d), row),
                  pl.BlockSpec((1, SUBLANES, d), lambda b, i: (b, jnp.maximum(i * tb8 - 1, 0), 0)),
                  pl.BlockSpec((1, SUBLANES, d), lambda b, i: (b, jnp.minimum((i + 1) * tb8, nb8 - 1), 0)),
                  pl.BlockSpec((1, 1, d), per_b), pl.BlockSpec((1, 1, d), per_b),
                  pl.BlockSpec((1, 1, d), per_b),
                  pl.BlockSpec((1, d), const), pl.BlockSpec((1, d), const),
                  pl.BlockSpec((d, f), const), pl.BlockSpec((d, f), const),
                  pl.BlockSpec((FFN_CONV, f), const), pl.BlockSpec((1, f), const),
                  pl.BlockSpec((f, d), const)],
        out_specs=pl.BlockSpec((1, tm, d), row),
        out_shape=jax.ShapeDtypeStruct((bsz, n, d), F32),
        scratch_shapes=[pltpu.VMEM((tm, f), BF16)],
        compiler_params=_cparams(2),
        name="conv_ffn",
    )(x, x, x, shift, scale, gate, norm_pre.reshape(1, -1), norm_post.reshape(1, -1),
      w_gate.astype(BF16), w_up.astype(BF16), conv_w, conv_b.reshape(1, -1), w_down.astype(BF16))


def kernel(x, c, ctx, c_ctx, mod_w, mod_b, mix_norm_pre, mix_norm_post, ffn_norm_pre, ffn_norm_post, w_in, w_out, ssd_conv_w, ssd_conv_b, ssd_dt_bias, ssd_a_log, ssd_d, ssd_norm_w, diff_lam_q1, diff_lam_k1, diff_lam_q2, diff_lam_k2, diff_subln_w, s5_lam_re, s5_lam_im, s5_log_step, s5_b_re, s5_b_im, s5_c_re, s5_c_im, s5_d, s5_glu_w, s5_glu_b, ffn_w_gate, ffn_w_up, ffn_conv_w, ffn_conv_b, ffn_w_down):
    bsz, n_lat, d = x.shape
    n_layers = mod_w.shape[0]
    rope_tabs = _rope_tables(n_lat)
    cc = jnp.concatenate([c, jnp.broadcast_to(c_ctx[None, :], (SUBLANES, d))], axis=0)
    mod = _modulation(cc, mod_w, mod_b)

    for layer in range(n_layers):
        need_ctx = layer < n_layers - 1
        lam_init = 0.8 - 0.6 * math.exp(-0.3 * layer)
        mod_l = mod[layer, :bsz].reshape(bsz, N_MOD, 1, d)
        mod_c = jnp.broadcast_to(mod[layer, bsz].reshape(1, N_MOD, 1, d), (bsz, N_MOD, 1, d))
        sh_a, sc_a, g_a, sh_f, sc_f, g_f = (mod_l[:, i] for i in range(N_MOD))
        csh_a, csc_a, cg_a, csh_f, csc_f, cg_f = (mod_c[:, i] for i in range(N_MOD))

        w_merged = _merge_w_in(w_in[layer])
        zx_l, dt_l, q_l, k_l, v_l, u_l = _in_proj(x, sh_a, sc_a, mix_norm_pre[layer], w_merged, rope_tabs)
        zx_c, dt_c, q_c, k_c, v_c, u_c = _in_proj(ctx, csh_a, csc_a, mix_norm_pre[layer], w_merged, None)

        a_l, a_c = _ssd(zx_l, dt_l, zx_c, dt_c, ssd_conv_w[layer], ssd_conv_b[layer],
                        ssd_dt_bias[layer], ssd_a_log[layer], ssd_d[layer], ssd_norm_w[layer], need_ctx)

        lam_params = (diff_lam_q1[layer], diff_lam_k1[layer], diff_lam_q2[layer], diff_lam_k2[layer])
        b_l = _attention(q_l, (k_c, k_l), (v_c, v_l), lam_params, diff_subln_w[layer], lam_init)

        mats = _s5_matrices(s5_lam_re[layer], s5_lam_im[layer], s5_log_step[layer], s5_b_re[layer],
                            s5_b_im[layer], s5_c_re[layer], s5_c_im[layer])
        ys_l, ys_c = _s5(u_l, u_c, mats)

        mix_p = (s5_d[layer], s5_glu_w[layer], s5_glu_b[layer], w_out[layer], mix_norm_post[layer])
        ffn_p = (ffn_norm_pre[layer], ffn_norm_post[layer], ffn_w_gate[layer], ffn_w_up[layer],
                 ffn_conv_w[layer], ffn_conv_b[layer], ffn_w_down[layer])
        x = _out_proj(a_l, b_l, ys_l, u_l, x, g_a, *mix_p)
        x = _ffn(x, sh_f, sc_f, g_f, *ffn_p)
        if need_ctx:
            b_c = _attention(q_c, (k_c,), (v_c,), lam_params, diff_subln_w[layer], lam_init)
            ctx = _out_proj(a_c, b_c, ys_c, u_c, ctx, cg_a, *mix_p)
            ctx = _ffn(ctx, csh_f, csc_f, cg_f, *ffn_p)
    return x
```

```python
import functools
import math

import numpy as np
import jax
import jax.numpy as jnp
from jax import lax
from jax.experimental import pallas as pl
from jax.experimental.pallas import tpu as pltpu

F32 = jnp.float32
BF16 = jnp.bfloat16
HI = lax.Precision.HIGHEST

D_MODEL = 1024
N_LAYERS = 2
GRID_W = 64
EPS = 1e-6
N_MOD = 6
SSD_HEADS = 4
SSD_HEAD_DIM = 64
SSD_WIDTH = 256
SSD_GROUPS = 2
SSD_STATE = 64
SSD_CONV = 5
SSD_CHUNK = 128
SSD_ROWS = 64
SSD_XBC = 512
SSD_IN = 776
DIFF_HEADS = 4
DIFF_HEAD_DIM = 64
DIFF_WIDTH = 512
DIFF_IN = 1536
ROPE_THETA = 10000.0
ATTN_Q_TILE = 1024
ATTN_ROW_SPLIT = 256
S5_GROUP = 16
S5_WIDTH = 256
S5_GROUPS = 16
S5_STATE = 64
S5_T = 16
FFN_DIM = 2816
FFN_CHUNK = 256
FFN_CONV = 3

LANES = 128
SUBLANES = 8
W_IN_PAD = 2688
VMEM_LIMIT = 56 * 1024 * 1024


def _cparams(n_axes):
    return pltpu.CompilerParams(dimension_semantics=("parallel",) * n_axes,
                                vmem_limit_bytes=VMEM_LIMIT)


def _rms(x, w):
    return x * lax.rsqrt(jnp.mean(x * x, axis=-1, keepdims=True) + EPS) * w


def _silu(x):
    return x * jax.nn.sigmoid(x)


def _row_tile(n):
    return min(512, n)


def _mod_kernel(c_ref, w_ref, b_ref, o_ref):
    a = _silu(c_ref[...])
    o_ref[0] = jnp.dot(a, w_ref[0], preferred_element_type=F32, precision=HI) + b_ref[0]


def _modulation(cc, mod_w, mod_b):
    nl, d, n = mod_w.shape
    r = cc.shape[0]
    tn = 512
    return pl.pallas_call(
        _mod_kernel,
        grid=(nl, n // tn),
        in_specs=[pl.BlockSpec((r, d), lambda l, j: (0, 0)),
                  pl.BlockSpec((1, d, tn), lambda l, j: (l, 0, j)),
                  pl.BlockSpec((1, 1, tn), lambda l, j: (l, 0, j))],
        out_specs=pl.BlockSpec((1, r, tn), lambda l, j: (l, 0, j)),
        out_shape=jax.ShapeDtypeStruct((nl, r, n), F32),
        compiler_params=_cparams(2),
        name="modulation",
    )(cc, mod_w, mod_b.reshape(nl, 1, n))


def _in_proj_kernel(*refs, rope):
    if rope:
        (x_ref, sh_ref, sc_ref, nw_ref, w_ref, cos_ref, sin_ref,
         zx_ref, dt_ref, q_ref, k_ref, v_ref, u_ref, u16_ref) = refs
    else:
        (x_ref, sh_ref, sc_ref, nw_ref, w_ref,
         zx_ref, dt_ref, q_ref, k_ref, v_ref, u_ref, u16_ref) = refs
    h = _rms(x_ref[0], nw_ref[...]) * (1.0 + sc_ref[0]) + sh_ref[0]
    hb = h.astype(BF16)

    def mm(lo, hi):
        return jnp.dot(hb, w_ref[:, lo:hi], preferred_element_type=F32)

    def rot(p):
        lane = lax.broadcasted_iota(jnp.int32, p.shape, 1)
        partner = jnp.where(lane % 32 < 16, pltpu.roll(p, LANES - 16, 1), pltpu.roll(p, 16, 1))
        return p * cos_ref[...] + partner * sin_ref[...]

    o_zx = SSD_WIDTH + SSD_XBC
    zx_ref[0] = mm(0, o_zx)
    for j in range(DIFF_WIDTH // LANES):
        q = mm(o_zx + j * LANES, o_zx + (j + 1) * LANES) * (DIFF_HEAD_DIM ** -0.5)
        k = mm(o_zx + DIFF_WIDTH + j * LANES, o_zx + DIFF_WIDTH + (j + 1) * LANES)
        if rope:
            q, k = rot(q), rot(k)
        q_ref[0, :, j * LANES:(j + 1) * LANES] = q.astype(BF16)
        k_ref[0, j * LANES:(j + 1) * LANES, :] = k.T.astype(BF16)
    o_v = o_zx + 2 * DIFF_WIDTH
    v_ref[0] = mm(o_v, o_v + DIFF_WIDTH).astype(BF16)
    o_u = o_v + DIFF_WIDTH
    u = mm(o_u, o_u + S5_WIDTH)
    u_ref[0] = u
    u16_ref[0] = u.astype(BF16)
    dt_ref[0] = mm(o_u + S5_WIDTH, W_IN_PAD)


def _in_proj(x, shift, scale, norm_w, w_merged, rope_tabs):
    bsz, n, d = x.shape
    tm = _row_tile(n)
    rope = rope_tabs is not None
    row = lambda b, i: (b, i, 0)
    per_b = lambda b, i: (b, 0, 0)
    const = lambda b, i: (0, 0)
    in_specs = [pl.BlockSpec((1, tm, d), row),
                pl.BlockSpec((1, 1, d), per_b),
                pl.BlockSpec((1, 1, d), per_b),
                pl.BlockSpec((1, d), const),
                pl.BlockSpec((d, W_IN_PAD), const)]
    args = [x, shift, scale, norm_w.reshape(1, d), w_merged]
    if rope:
        in_specs += [pl.BlockSpec((tm, LANES), lambda b, i: (i, 0))] * 2
        args += list(rope_tabs)
    widths = (SSD_WIDTH + SSD_XBC, LANES, DIFF_WIDTH, DIFF_WIDTH, DIFF_WIDTH, S5_WIDTH, S5_WIDTH)
    dtypes = (F32, F32, BF16, BF16, BF16, F32, BF16)
    out_specs = [pl.BlockSpec((1, tm, w), row) for w in widths]
    out_shape = [jax.ShapeDtypeStruct((bsz, n, w), t) for w, t in zip(widths, dtypes)]
    out_specs[3] = pl.BlockSpec((1, DIFF_WIDTH, tm), lambda b, i: (b, 0, i))
    out_shape[3] = jax.ShapeDtypeStruct((bsz, DIFF_WIDTH, n), BF16)
    return pl.pallas_call(
        functools.partial(_in_proj_kernel, rope=rope),
        grid=(bsz, n // tm),
        in_specs=in_specs,
        out_specs=out_specs,
        out_shape=out_shape,
        compiler_params=_cparams(2),
        name="in_proj_rope" if rope else "in_proj",
    )(*args)


def _merge_w_in(w):
    o1, o2 = SSD_IN, SSD_IN + DIFF_IN
    parts = [w[:, :SSD_WIDTH + SSD_XBC], w[:, o1:o2], w[:, o2:], w[:, SSD_WIDTH + SSD_XBC:o1]]
    m = jnp.concatenate(parts, axis=1)
    return jnp.pad(m, ((0, 0), (0, W_IN_PAD - m.shape[1]))).astype(BF16)


def _rope_tables(n):
    t = jnp.arange(n, dtype=jnp.int32)
    pos = jnp.stack([(t // GRID_W).astype(F32), (t % GRID_W).astype(F32)], axis=1)
    quarter = DIFF_HEAD_DIM // 4
    inv_freq = ROPE_THETA ** (-jnp.arange(quarter, dtype=F32) / quarter)
    lane = np.arange(LANES)
    axis = (lane % DIFF_HEAD_DIM) // (DIFF_HEAD_DIM // 2)
    freq = lane % quarter
    ang = pos[:, axis] * inv_freq[freq][None, :]
    sign = np.where(lane % (2 * quarter) < quarter, -1.0, 1.0).astype(np.float32)
    return jnp.cos(ang), jnp.sin(ang) * sign[None, :]


def _cumsum_rows(x, reverse):
    n = x.shape[0]
    row = lax.broadcasted_iota(jnp.int32, x.shape, 0)
    k = 1
    while k < n:
        if reverse:
            x = x + jnp.where(row < n - k, pltpu.roll(x, n - k, 0), 0.0)
        else:
            x = x + jnp.where(row >= k, pltpu.roll(x, k, 0), 0.0)
        k *= 2
    return x


def _ssd_chunk(xact_ref, dtt_ref, dta_ref, exp_ref, state_ref, y_ref, c, direction):
    q = SSD_CHUNK
    nh = SSD_HEADS
    r0 = pl.multiple_of(c * q, q)
    xa = xact_ref[pl.ds(r0, q), :]
    dta = dta_ref[pl.ds(r0, q), :]
    dt_t = dtt_ref[c]
    rows = lax.broadcasted_iota(jnp.int32, (q, q), 0)
    cols = lax.broadcasted_iota(jnp.int32, (q, q), 1)
    lane = cols
    mask = (rows >= cols) if direction == 0 else (rows <= cols)
    acs = _cumsum_rows(dta, reverse=direction == 1)
    acs_t = acs.T
    hi = acs.astype(BF16).astype(F32)
    r1 = acs - hi
    mid = r1.astype(BF16).astype(F32)
    lo = (r1 - mid).astype(BF16).astype(F32)
    parts = hi + pltpu.roll(mid, 8, 1) + pltpu.roll(lo, 16, 1)
    neg = -pltpu.roll(parts, 24, 1)
    lhs = jnp.where(lane < 24, parts, jnp.where(lane < 48, 1.0, 0.0)).astype(BF16)
    rhs = []
    for h in range(nh):
        j = direction * nh + h
        is_one = (lane == j) | (lane == 8 + j) | (lane == 16 + j)
        is_neg = (lane == 24 + j) | (lane == 32 + j) | (lane == 40 + j)
        rhs.append(jnp.where(is_one, 1.0, jnp.where(is_neg, neg, 0.0)).astype(BF16))
    diff = lax.dot_general(lhs, jnp.concatenate(rhs, axis=0), (((1,), (1,)), ((), ())),
                           preferred_element_type=F32)
    mask4 = jnp.concatenate([mask] * nh, axis=1)
    decay = jnp.exp(jnp.where(mask4, diff, -jnp.inf))
    acs_x = jnp.dot(parts.astype(BF16), exp_ref[direction], preferred_element_type=F32)
    last = q - 1 if direction == 0 else 0
    total_x = acs_x[last:last + 1, :]

    xb = xa[:, 0:SSD_WIDTH].astype(BF16)
    bm = xa[:, SSD_WIDTH:SSD_WIDTH + LANES]
    cm = xa[:, SSD_WIDTH + LANES:]
    bm_b = bm.astype(BF16)
    bm_t = bm.T
    state = state_ref[direction]
    y_off = jnp.dot(cm.astype(BF16), state.astype(BF16), preferred_element_type=F32) * jnp.exp(acs_x)
    row_grp = lax.broadcasted_iota(jnp.int32, (LANES, SSD_WIDTH), 0) // SSD_STATE
    lane_head = lax.broadcasted_iota(jnp.int32, (LANES, SSD_WIDTH), 1) // SSD_HEAD_DIM
    upd = jnp.zeros((LANES, SSD_WIDTH), F32)
    y_diag = []
    for g in range(SSD_GROUPS):
        cg = jnp.where(lane // SSD_STATE == g, cm, 0.0).astype(BF16)
        cb = lax.dot_general(cg, bm_b, (((1,), (1,)), ((), ())), preferred_element_type=F32)
        xg = xb[:, g * LANES:(g + 1) * LANES]
        pair = []
        for h in range(g * 2, g * 2 + 2):
            j = direction * nh + h
            row_dt = dt_t[j:j + 1, :]
            scores = (cb * decay[:, h * q:(h + 1) * q] * row_dt).astype(BF16)
            pair.append(jnp.dot(scores, xg, preferred_element_type=F32))
            w_row = row_dt * jnp.exp(acs_t[j:j + 1, last:last + 1] - acs_t[j:j + 1, :])
            upd_h = jnp.dot((bm_t * w_row).astype(BF16), xb, preferred_element_type=F32)
            upd = upd + jnp.where((row_grp == g) & (lane_head == h), upd_h, 0.0)
        y_diag.append(jnp.where(lane < SSD_HEAD_DIM, pair[0], pair[1]))
    y_ref[pl.ds(r0, q), :] += jnp.concatenate(y_diag, axis=-1) + y_off
    state_ref[direction] = jnp.exp(total_x) * state + upd


def _ssd_kernel(*refs, n_ctx, n_lat, need_ctx):
    (zx_l, dt_l, zx_c, dt_c, cw_ref, cb_ref, dtb_ref, alog_ref, d_ref, nw_ref, exp_ref) = refs[:11]
    outs = refs[11:13] if need_ctx else (refs[11], None)
    pad_ref, xact_ref, dtt_ref, dta_ref, y_ref, state_ref = refs[-6:]
    out_l, out_c = outs
    halo = (SSD_CONV - 1) // 2

    for seg_ref, n, base in ((zx_c, n_ctx, 0), (zx_l, n_lat, n_ctx)):
        pad_ref[0:SUBLANES, :] = jnp.zeros((SUBLANES, SSD_XBC), F32)
        pad_ref[SUBLANES + n:2 * SUBLANES + n, :] = jnp.zeros((SUBLANES, SSD_XBC), F32)

        def copy_rows(i, carry, seg_ref=seg_ref):
            r = pl.multiple_of(i * SSD_CHUNK, SSD_CHUNK)
            pad_ref[pl.ds(SUBLANES + r, SSD_CHUNK), :] = seg_ref[0, pl.ds(r, SSD_CHUNK), SSD_WIDTH:]
            return carry

        lax.fori_loop(0, n // SSD_CHUNK, copy_rows, 0)

        for r in range(0, n, SSD_ROWS):
            acc = jnp.broadcast_to(cb_ref[...], (SSD_ROWS, SSD_XBC))
            for k in range(SSD_CONV):
                lo = r + SUBLANES + k - halo
                acc = acc + cw_ref[k:k + 1, :] * pad_ref[lo:lo + SSD_ROWS, :]
            act = _silu(acc)
            xact_ref[base + r:base + r + SSD_ROWS, :] = act
            y_ref[base + r:base + r + SSD_ROWS, :] = act[:, 0:SSD_WIDTH] * d_ref[...]
    a_neg = -jnp.exp(alog_ref[...])
    head_lane = lax.broadcasted_iota(jnp.int32, (SSD_CHUNK, LANES), 1) < 2 * SSD_HEADS
    for seg_ref, n, base in ((dt_c, n_ctx, 0), (dt_l, n_lat, n_ctx)):
        for t0 in range(0, n, SSD_CHUNK):
            dt = jax.nn.softplus(seg_ref[0, t0:t0 + SSD_CHUNK, :] + dtb_ref[...])
            dtt_ref[(base + t0) // SSD_CHUNK] = dt.T
            dta_ref[base + t0:base + t0 + SSD_CHUNK, :] = jnp.where(head_lane, dt * a_neg, 0.0)
    n_tot = n_ctx + n_lat
    state_ref[...] = jnp.zeros(state_ref.shape, F32)

    ncc, nct = n_ctx // SSD_CHUNK, n_tot // SSD_CHUNK

    def step(i, carry):
        _ssd_chunk(xact_ref, dtt_ref, dta_ref, exp_ref, state_ref, y_ref, i, 0)
        cb_idx = jnp.where(i < ncc, ncc - 1 - i, nct - 1 - (i - ncc))
        _ssd_chunk(xact_ref, dtt_ref, dta_ref, exp_ref, state_ref, y_ref, cb_idx, 1)
        return carry

    lax.fori_loop(0, nct, step, 0, unroll=2)

    for seg_ref, o_ref, n, base in ((zx_c, out_c, n_ctx, 0), (zx_l, out_l, n_lat, n_ctx)):
        if o_ref is None:
            continue
        def gate_rows(i, carry, seg_ref=seg_ref, o_ref=o_ref, base=base):
            r = pl.multiple_of(i * SSD_ROWS, SSD_ROWS)
            z = seg_ref[0, pl.ds(r, SSD_ROWS), 0:SSD_WIDTH]
            v = y_ref[pl.ds(base + r, SSD_ROWS), :] * _silu(z)
            o_ref[0, pl.ds(r, SSD_ROWS), :] = _rms(v, nw_ref[...]).astype(BF16)
            return carry

        lax.fori_loop(0, n // SSD_ROWS, gate_rows, 0, unroll=4)


def _ssd(zx_l, dt_l, zx_c, dt_c, conv_w, conv_b, dt_bias, a_log, d_skip, norm_w, need_ctx):
    bsz, n_lat, wzx = zx_l.shape
    n_ctx = zx_c.shape[1]
    n_tot = n_ctx + n_lat
    pad8 = lambda v: jnp.pad(v.reshape(1, -1), ((0, 0), (0, LANES - v.size)))
    per_b = lambda b: (b, 0, 0)
    const = lambda b: (0, 0)
    k_idx = np.arange(LANES)[None, :, None]
    d_idx = np.arange(2)[:, None, None]
    h_idx = (np.arange(SSD_WIDTH) // SSD_HEAD_DIM)[None, None, :]
    expand = jnp.asarray((k_idx < 24) & (k_idx % 8 == d_idx * SSD_HEADS + h_idx), BF16)
    params = [conv_w, conv_b.reshape(1, -1), pad8(dt_bias), pad8(a_log),
              jnp.repeat(d_skip, SSD_HEAD_DIM).reshape(1, -1), norm_w.reshape(1, -1)]
    in_specs = [pl.BlockSpec((1, n_lat, wzx), per_b), pl.BlockSpec((1, n_lat, LANES), per_b),
                pl.BlockSpec((1, n_ctx, wzx), per_b), pl.BlockSpec((1, n_ctx, LANES), per_b)]
    in_specs += [pl.BlockSpec(p.shape, const) for p in params]
    in_specs += [pl.BlockSpec(expand.shape, lambda b: (0, 0, 0))]
    out_specs = [pl.BlockSpec((1, n_lat, SSD_WIDTH), per_b)]
    out_shape = [jax.ShapeDtypeStruct((bsz, n_lat, SSD_WIDTH), BF16)]
    if need_ctx:
        out_specs.append(pl.BlockSpec((1, n_ctx, SSD_WIDTH), per_b))
        out_shape.append(jax.ShapeDtypeStruct((bsz, n_ctx, SSD_WIDTH), BF16))
    scratch = [pltpu.VMEM((max(n_lat, n_ctx) + 2 * SUBLANES, SSD_XBC), F32),
               pltpu.VMEM((n_tot, SSD_XBC), F32),
               pltpu.VMEM((n_tot // SSD_CHUNK, LANES, SSD_CHUNK), F32),
               pltpu.VMEM((n_tot, LANES), F32),
               pltpu.VMEM((n_tot, SSD_WIDTH), F32),
               pltpu.VMEM((2, SSD_GROUPS * SSD_STATE, SSD_WIDTH), F32)]
    outs = pl.pallas_call(
        functools.partial(_ssd_kernel, n_ctx=n_ctx, n_lat=n_lat, need_ctx=need_ctx),
        grid=(bsz,),
        in_specs=in_specs,
        out_specs=out_specs,
        out_shape=out_shape,
        scratch_shapes=scratch,
        compiler_params=_cparams(1),
        name="ssd_mixer",
    )(zx_l, dt_l, zx_c, dt_c, *params, expand)
    return (outs[0], outs[1]) if need_ctx else (outs[0], None)


def _attn_kernel(*refs, n_kv, lam_init):
    q_ref = refs[0]
    k_refs = refs[1:1 + n_kv]
    v_refs = refs[1 + n_kv:1 + 2 * n_kv]
    lq1, lk1, lq2, lk2, sw_ref, o_ref = refs[1 + 2 * n_kv:-4]
    s_refs, e_refs = refs[-4:-2], refs[-2:]
    q = q_ref[0]
    hw = q.shape[1]
    lane = lax.broadcasted_iota(jnp.int32, q.shape, 1)
    lam = (jnp.exp(jnp.sum(lq1[...] * lk1[...], axis=-1, keepdims=True))
           - jnp.exp(jnp.sum(lq2[...] * lk2[...], axis=-1, keepdims=True)) + lam_init)
    offs = np.cumsum([0] + [k_ref.shape[2] for k_ref in k_refs])
    v_ext = [jnp.concatenate([v_ref[0], jnp.ones(v_ref.shape[1:], BF16)], axis=-1) for v_ref in v_refs]
    tq = q.shape[0]
    halves = [(0, tq)] if tq < 2 * ATTN_ROW_SPLIT else [(0, tq // 2), (tq // 2, tq)]
    for comp, s_ref in enumerate(s_refs):
        own = (lane >= DIFF_HEAD_DIM) if comp else (lane < DIFF_HEAD_DIM)
        qc = jnp.where(own, q, jnp.zeros_like(q))
        for i, k_ref in enumerate(k_refs):
            s_ref[:, offs[i]:offs[i + 1]] = jnp.dot(qc, k_ref[0], preferred_element_type=F32)
    outs = []
    for s_ref, e_ref in zip(s_refs, e_refs):
        parts = []
        for lo, hi in halves:
            s = s_ref[lo:hi, :]
            e_ref[lo:hi, :] = jnp.exp(s - jnp.max(s, axis=-1, keepdims=True)).astype(BF16)
            ov = None
            for i in range(n_kv):
                part = jnp.dot(e_ref[lo:hi, offs[i]:offs[i + 1]], v_ext[i], preferred_element_type=F32)
                ov = part if ov is None else ov + part
            parts.append(ov[:, 0:hw] * (1.0 / ov[:, hw:hw + 1]))
        outs.append(parts)
    for (lo, hi), o0, o1 in zip(halves, *outs):
        o = o0 - lam * o1
        o_ref[0, lo:hi, :] = (_rms(o, sw_ref[...]) * (1.0 - lam_init)).astype(BF16)


def _attention(q, ks, vs, lam_params, subln_w, lam_init):
    bsz, n, _ = q.shape
    tq = min(ATTN_Q_TILE, n)
    n_kv = len(ks)
    n_keys = sum(a.shape[2] for a in ks)
    hw = 2 * DIFF_HEAD_DIM
    in_specs = [pl.BlockSpec((1, tq, hw), lambda b, h, i: (b, i, h))]
    in_specs += [pl.BlockSpec((1, hw, a.shape[2]), lambda b, h, i: (b, h, 0)) for a in ks]
    in_specs += [pl.BlockSpec((1, a.shape[1], hw), lambda b, h, i: (b, 0, h)) for a in vs]
    in_specs += [pl.BlockSpec((1, DIFF_HEAD_DIM), lambda b, h, i: (0, 0))] * 4
    in_specs += [pl.BlockSpec((1, hw), lambda b, h, i: (0, 0))]
    return pl.pallas_call(
        functools.partial(_attn_kernel, n_kv=n_kv, lam_init=lam_init),
        grid=(bsz, DIFF_HEADS, n // tq),
        in_specs=in_specs,
        out_specs=pl.BlockSpec((1, tq, hw), lambda b, h, i: (b, i, h)),
        out_shape=jax.ShapeDtypeStruct((bsz, n, DIFF_WIDTH), BF16),
        scratch_shapes=[pltpu.VMEM((tq, n_keys), F32)] * 2 + [pltpu.VMEM((tq, n_keys), BF16)] * 2,
        compiler_params=_cparams(3),
        name="diff_attention",
    )(q, *ks, *vs, *[p.reshape(1, -1) for p in lam_params], subln_w.reshape(1, -1))


def _s5_matrices(lam_re, lam_im, log_step, b_re, b_im, c_re, c_im):
    t = S5_T
    step = jnp.exp(log_step.astype(F32))[..., None]
    lr, li = lam_re.astype(F32), lam_im.astype(F32)
    mag = jnp.exp(lr * step)
    a = (mag * jnp.cos(li * step)) + 1j * (mag * jnp.sin(li * step))
    lam = lr + 1j * li
    bb = ((a - 1.0) / lam)[..., None] * (b_re.astype(F32) + 1j * b_im.astype(F32))
    cc = c_re.astype(F32) + 1j * c_im.astype(F32)
    pw = [jnp.ones_like(a)]
    for _ in range(t):
        pw.append(pw[-1] * a)
    pw = jnp.stack(pw, axis=0)
    kern = jnp.real(jnp.einsum('dgcn,tdgn,dgnk->dgtck', cc, pw[:t], bb))
    s_idx = np.arange(t)[:, None]
    t_idx = np.arange(t)[None, :]
    lag = t_idx - s_idx
    kf = jnp.where((lag >= 0)[None, :, :, None, None], kern[0][:, np.abs(lag)], 0.0)
    kb = jnp.where((lag <= 0)[None, :, :, None, None], kern[1][:, np.abs(lag)], 0.0)
    k_full = (kf + kb).transpose(0, 1, 4, 2, 3).reshape(S5_GROUPS, t * S5_GROUP, t * S5_GROUP)
    wf = pw[t - 1 - np.arange(t), 0][..., None] * bb[0][None]
    wb = pw[np.arange(t), 1][..., None] * bb[1][None]
    def rows_sc(w):
        return w.transpose(1, 0, 3, 2).reshape(S5_GROUPS, t * S5_GROUP, S5_STATE)
    w_state = jnp.concatenate([rows_sc(jnp.real(wf)), rows_sc(jnp.real(wb)),
                               rows_sc(jnp.imag(wf)), rows_sc(jnp.imag(wb))], axis=-1)
    cf = cc[0][None] * pw[1 + np.arange(t), 0][:, :, None, :]
    cbk = cc[1][None] * pw[t - np.arange(t), 1][:, :, None, :]
    def cols_tc(w):
        return w.transpose(1, 3, 0, 2).reshape(S5_GROUPS, S5_STATE, t * S5_GROUP)
    c_off = jnp.concatenate([cols_tc(jnp.real(cf)), cols_tc(jnp.real(cbk)),
                             cols_tc(-jnp.imag(cf)), cols_tc(-jnp.imag(cbk))], axis=1)
    at = pw[t]
    a_rows = jnp.stack([jnp.concatenate([jnp.real(at[0]), jnp.real(at[1])], axis=-1),
                        jnp.concatenate([jnp.imag(at[0]), jnp.imag(at[1])], axis=-1)], axis=1)
    a_chunk = jnp.pad(a_rows, ((0, 0), (0, SUBLANES - 2), (0, 0)))
    return k_full.astype(BF16), w_state.astype(BF16), c_off.astype(BF16), a_chunk.astype(F32)


def _s5_kernel(uc_ref, ul_ref, kf_ref, ws_ref, co_ref, a_ref, yc_ref, yl_ref, e_ref, h_ref):
    ncc, bsz, w = uc_ref.shape[1:]
    ncl = ul_ref.shape[1]
    nct = ncc + ncl
    uc = uc_ref[0].reshape(ncc * bsz, w)
    ul = ul_ref[0].reshape(ncl * bsz, w)
    e_ref[0:ncc] = jnp.dot(uc, ws_ref[0], preferred_element_type=F32).reshape(ncc, bsz, w)
    e_ref[ncc:nct] = jnp.dot(ul, ws_ref[0], preferred_element_type=F32).reshape(ncl, bsz, w)
    ar = a_ref[0, 0:1, :]
    ai = a_ref[0, 1:2, :]
    half = S5_STATE
    lane = lax.broadcasted_iota(jnp.int32, (bsz, LANES), 1)
    is_f = lane < half

    def step(i, carry):
        h_re, h_im = carry
        jf = i
        jb = jnp.where(i < ncc, ncc - 1 - i, nct - 1 - (i - ncc))
        h_ref[jf, :, 0:half] = h_re[:, 0:half]
        h_ref[jb, :, half:LANES] = h_re[:, half:]
        h_ref[jf, :, LANES:LANES + half] = h_im[:, 0:half]
        h_ref[jb, :, LANES + half:] = h_im[:, half:]
        e_re = jnp.where(is_f, e_ref[jf, :, 0:LANES], e_ref[jb, :, 0:LANES])
        e_im = jnp.where(is_f, e_ref[jf, :, LANES:], e_ref[jb, :, LANES:])
        return (ar * h_re - ai * h_im + e_re, ar * h_im + ai * h_re + e_im)

    zero = jnp.zeros((bsz, LANES), F32)
    lax.fori_loop(0, nct, step, (zero, zero))
    for u, y_ref, lo, nc in ((uc, yc_ref, 0, ncc), (ul, yl_ref, ncc, ncl)):
        hs = h_ref[lo:lo + nc].reshape(nc * bsz, w).astype(BF16)
        y = (jnp.dot(u, kf_ref[0], preferred_element_type=F32)
             + jnp.dot(hs, co_ref[0], preferred_element_type=F32))
        y_ref[0] = y.reshape(nc, bsz, w).astype(BF16)


def _s5_to_chunks(u):
    bsz, n, _ = u.shape
    u = u.reshape(bsz, n // S5_T, S5_T, S5_GROUPS, S5_GROUP).transpose(3, 1, 0, 2, 4)
    return u.reshape(S5_GROUPS, n // S5_T, bsz, S5_T * S5_GROUP)


def _s5_from_chunks(y, bsz):
    nchunk = y.shape[1]
    y = y.reshape(S5_GROUPS, nchunk, bsz, S5_T, S5_GROUP).transpose(2, 1, 3, 0, 4)
    return y.reshape(bsz, nchunk * S5_T, S5_WIDTH)


def _s5(u_l, u_c, mats):
    k_full, w_state, c_off, a_chunk = mats
    bsz = u_l.shape[0]
    uc, ul = _s5_to_chunks(u_c), _s5_to_chunks(u_l)
    ncc, ncl, w = uc.shape[1], ul.shape[1], ul.shape[3]
    nct = ncc + ncl
    blk = lambda g: (g, 0, 0, 0)
    mat = lambda g: (g, 0, 0)
    y_c, y_l = pl.pallas_call(
        _s5_kernel,
        grid=(S5_GROUPS,),
        in_specs=[pl.BlockSpec((1, ncc, bsz, w), blk), pl.BlockSpec((1, ncl, bsz, w), blk),
                  pl.BlockSpec((1, w, w), mat), pl.BlockSpec((1, w, w), mat),
                  pl.BlockSpec((1, w, w), mat), pl.BlockSpec((1, SUBLANES, LANES), mat)],
        out_specs=[pl.BlockSpec((1, ncc, bsz, w), blk), pl.BlockSpec((1, ncl, bsz, w), blk)],
        out_shape=[jax.ShapeDtypeStruct((S5_GROUPS, ncc, bsz, w), BF16),
                   jax.ShapeDtypeStruct((S5_GROUPS, ncl, bsz, w), BF16)],
        scratch_shapes=[pltpu.VMEM((nct, bsz, w), F32), pltpu.VMEM((nct, bsz, w), F32)],
        compiler_params=_cparams(1),
        name="s5_mixer",
    )(uc, ul, k_full, w_state, c_off, a_chunk)
    return _s5_from_chunks(y_l, bsz), _s5_from_chunks(y_c, bsz)


def _out_proj_kernel(a_ref, b_ref, ys_ref, u_ref, x_ref, gate_ref, d_ref, gw_ref, gb_ref,
                     w_ref, nw_ref, o_ref):
    y = jax.nn.gelu(ys_ref[0] + d_ref[...] * u_ref[0], approximate=True)
    gate = jnp.dot(y.astype(BF16), gw_ref[...], preferred_element_type=F32) + gb_ref[...]
    s = (y * jax.nn.sigmoid(gate)).astype(BF16)
    o1, o2 = SSD_WIDTH, SSD_WIDTH + DIFF_WIDTH
    mix = (jnp.dot(a_ref[0], w_ref[0:o1, :], preferred_element_type=F32)
           + jnp.dot(b_ref[0], w_ref[o1:o2, :], preferred_element_type=F32)
           + jnp.dot(s, w_ref[o2:, :], preferred_element_type=F32))
    o_ref[0] = x_ref[0] + gate_ref[0] * _rms(mix, nw_ref[...])


def _out_proj(a, b, ys, u, x, gate, s5_d, glu_w, glu_b, w_out, norm_w):
    bsz, n, d = x.shape
    tm = _row_tile(n)
    row = lambda bi, i: (bi, i, 0)
    per_b = lambda bi, i: (bi, 0, 0)
    const = lambda bi, i: (0, 0)
    return pl.pallas_call(
        _out_proj_kernel,
        grid=(bsz, n // tm),
        in_specs=[pl.BlockSpec((1, tm, SSD_WIDTH), row), pl.BlockSpec((1, tm, DIFF_WIDTH), row),
                  pl.BlockSpec((1, tm, S5_WIDTH), row), pl.BlockSpec((1, tm, S5_WIDTH), row),
                  pl.BlockSpec((1, tm, d), row), pl.BlockSpec((1, 1, d), per_b),
                  pl.BlockSpec((1, S5_WIDTH), const), pl.BlockSpec((S5_WIDTH, S5_WIDTH), const),
                  pl.BlockSpec((1, S5_WIDTH), const), pl.BlockSpec((d, d), const),
                  pl.BlockSpec((1, d), const)],
        out_specs=pl.BlockSpec((1, tm, d), row),
        out_shape=jax.ShapeDtypeStruct((bsz, n, d), F32),
        compiler_params=_cparams(2),
        name="out_proj",
    )(a, b, ys, u, x, gate, s5_d.reshape(1, -1), glu_w.astype(BF16), glu_b.reshape(1, -1),
      w_out.astype(BF16), norm_w.reshape(1, -1))


def _ffn_kernel(x_ref, xp_ref, xn_ref, sh_ref, sc_ref, gate_ref, nw1_ref, nw2_ref,
                wg_ref, wu_ref, cw_ref, cb_ref, wd_ref, o_ref, act_ref):
    i = pl.program_id(1)
    is_first = i == 0
    is_last = i == pl.num_programs(1) - 1
    tm = x_ref.shape[1]
    x = x_ref[0]
    x_ext = jnp.concatenate([x, xp_ref[0], xn_ref[0]], axis=0)
    h_ext = _rms(x_ext, nw1_ref[...]) * (1.0 + sc_ref[0]) + sh_ref[0]
    row_ext = lax.broadcasted_iota(jnp.int32, (tm + 2 * SUBLANES, 1), 0)
    outside = (((row_ext >= tm) & (row_ext < tm + SUBLANES) & is_first)
               | ((row_ext >= tm + SUBLANES) & is_last))
    h_ext = jnp.where(outside, 0.0, h_ext).astype(BF16)
    h = h_ext[0:tm]
    row = lax.broadcasted_iota(jnp.int32, (tm, 1), 0)
    for lo in range(0, wg_ref.shape[1], FFN_CHUNK):
        hi = lo + FFN_CHUNK
        g_ext = jnp.dot(h_ext, wg_ref[:, lo:hi], preferred_element_type=F32)
        g = g_ext[0:tm]
        g_prev = jnp.where(row == 0, g_ext[tm + SUBLANES - 1:tm + SUBLANES], pltpu.roll(g, 1, 0))
        g_next = jnp.where(row == tm - 1, g_ext[tm + SUBLANES:tm + SUBLANES + 1], pltpu.roll(g, tm - 1, 0))
        conv = (cw_ref[0:1, lo:hi] * g_prev + cw_ref[1:2, lo:hi] * g + cw_ref[2:3, lo:hi] * g_next
                + cb_ref[:, lo:hi])
        up = jnp.dot(h, wu_ref[:, lo:hi], preferred_element_type=F32)
        act_ref[:, lo:hi] = (_silu(conv) * up).astype(BF16)
    f = jnp.dot(act_ref[...], wd_ref[...], preferred_element_type=F32)
    o_ref[0] = x + gate_ref[0] * _rms(f, nw2_ref[...])


def _ffn(x, shift, scale, gate, norm_pre, norm_post, w_gate, w_up, conv_w, conv_b, w_down):
    bsz, n, d = x.shape
    tm = _row_tile(n)
    f = w_gate.shape[1]
    nb8 = n // SUBLANES
    tb8 = tm // SUBLANES
    row = lambda b, i: (b, i, 0)
    per_b = lambda b, i: (b, 0, 0)
    const = lambda b, i: (0, 0)
    return pl.pallas_call(
        _ffn_kernel,
        grid=(bsz, n // tm),
        in_specs=[pl.BlockSpec((1, tm, d), row),
                  pl.BlockSpec((1, SUBLANES, d), lambda b, i: (b, jnp.maximum(i * tb8 - 1, 0), 0)),
                  pl.BlockSpec((1, SUBLANES, d), lambda b, i: (b, jnp.minimum((i + 1) * tb8, nb8 - 1), 0)),
                  pl.BlockSpec((1, 1, d), per_b), pl.BlockSpec((1, 1, d), per_b),
                  pl.BlockSpec((1, 1, d), per_b),
                  pl.BlockSpec((1, d), const), pl.BlockSpec((1, d), const),
                  pl.BlockSpec((d, f), const), pl.BlockSpec((d, f), const),
                  pl.BlockSpec((FFN_CONV, f), const), pl.BlockSpec((1, f), const),
                  pl.BlockSpec((f, d), const)],
        out_specs=pl.BlockSpec((1, tm, d), row),
        out_shape=jax.ShapeDtypeStruct((bsz, n, d), F32),
        scratch_shapes=[pltpu.VMEM((tm, f), BF16)],
        compiler_params=_cparams(2),
        name="conv_ffn",
    )(x, x, x, shift, scale, gate, norm_pre.reshape(1, -1), norm_post.reshape(1, -1),
      w_gate.astype(BF16), w_up.astype(BF16), conv_w, conv_b.reshape(1, -1), w_down.astype(BF16))


def kernel(x, c, ctx, c_ctx, mod_w, mod_b, mix_norm_pre, mix_norm_post, ffn_norm_pre, ffn_norm_post, w_in, w_out, ssd_conv_w, ssd_conv_b, ssd_dt_bias, ssd_a_log, ssd_d, ssd_norm_w, diff_lam_q1, diff_lam_k1, diff_lam_q2, diff_lam_k2, diff_subln_w, s5_lam_re, s5_lam_im, s5_log_step, s5_b_re, s5_b_im, s5_c_re, s5_c_im, s5_d, s5_glu_w, s5_glu_b, ffn_w_gate, ffn_w_up, ffn_conv_w, ffn_conv_b, ffn_w_down):
    bsz, n_lat, d = x.shape
    n_layers = mod_w.shape[0]
    rope_tabs = _rope_tables(n_lat)
    cc = jnp.concatenate([c, jnp.broadcast_to(c_ctx[None, :], (SUBLANES, d))], axis=0)
    mod = _modulation(cc, mod_w, mod_b)

    for layer in range(n_layers):
        need_ctx = layer < n_layers - 1
        lam_init = 0.8 - 0.6 * math.exp(-0.3 * layer)
        mod_l = mod[layer, :bsz].reshape(bsz, N_MOD, 1, d)
        mod_c = jnp.broadcast_to(mod[layer, bsz].reshape(1, N_MOD, 1, d), (bsz, N_MOD, 1, d))
        sh_a, sc_a, g_a, sh_f, sc_f, g_f = (mod_l[:, i] for i in range(N_MOD))
        csh_a, csc_a, cg_a, csh_f, csc_f, cg_f = (mod_c[:, i] for i in range(N_MOD))

        w_merged = _merge_w_in(w_in[layer])
        zx_l, dt_l, q_l, k_l, v_l, u_l, u16_l = _in_proj(x, sh_a, sc_a, mix_norm_pre[layer], w_merged, rope_tabs)
        zx_c, dt_c, q_c, k_c, v_c, u_c, u16_c = _in_proj(ctx, csh_a, csc_a, mix_norm_pre[layer], w_merged, None)

        a_l, a_c = _ssd(zx_l, dt_l, zx_c, dt_c, ssd_conv_w[layer], ssd_conv_b[layer],
                        ssd_dt_bias[layer], ssd_a_log[layer], ssd_d[layer], ssd_norm_w[layer], need_ctx)

        lam_params = (diff_lam_q1[layer], diff_lam_k1[layer], diff_lam_q2[layer], diff_lam_k2[layer])
        b_l = _attention(q_l, (k_c, k_l), (v_c, v_l), lam_params, diff_subln_w[layer], lam_init)

        mats = _s5_matrices(s5_lam_re[layer], s5_lam_im[layer], s5_log_step[layer], s5_b_re[layer],
                            s5_b_im[layer], s5_c_re[layer], s5_c_im[layer])
        ys_l, ys_c = _s5(u16_l, u16_c, mats)

        mix_p = (s5_d[layer], s5_glu_w[layer], s5_glu_b[layer], w_out[layer], mix_norm_post[layer])
        ffn_p = (ffn_norm_pre[layer], ffn_norm_post[layer], ffn_w_gate[layer], ffn_w_up[layer],
                 ffn_conv_w[layer], ffn_conv_b[layer], ffn_w_down[layer])
        x = _out_proj(a_l, b_l, ys_l, u_l, x, g_a, *mix_p)
        x = _ffn(x, sh_f, sc_f, g_f, *ffn_p)
        if need_ctx:
            b_c = _attention(q_c, (k_c,), (v_c,), lam_params, diff_subln_w[layer], lam_init)
            ctx = _out_proj(a_c, b_c, ys_c, u_c, ctx, cg_a, *mix_p)
            ctx = _ffn(ctx, csh_f, csc_f, cg_f, *ffn_p)
    return x
```

```python
import functools
import math

import numpy as np
import jax
import jax.numpy as jnp
from jax import lax
from jax.experimental import pallas as pl
from jax.experimental.pallas import tpu as pltpu

F32 = jnp.float32
BF16 = jnp.bfloat16
HI = lax.Precision.HIGHEST

D_MODEL = 1024
N_LAYERS = 2
GRID_W = 64
EPS = 1e-6
N_MOD = 6
SSD_HEADS = 4
SSD_HEAD_DIM = 64
SSD_WIDTH = 256
SSD_GROUPS = 2
SSD_STATE = 64
SSD_CONV = 5
SSD_CHUNK = 128
SSD_ROWS = 64
CONV_ROWS = 64
SSD_XBC = 512
SSD_IN = 776
DIFF_HEADS = 4
DIFF_HEAD_DIM = 64
DIFF_WIDTH = 512
DIFF_IN = 1536
ROPE_THETA = 10000.0
ATTN_Q_TILE = 1024
ATTN_ROW_SPLIT = 128
S5_GROUP = 16
S5_WIDTH = 256
S5_GROUPS = 16
S5_STATE = 64
S5_T = 16
FFN_DIM = 2816
FFN_CHUNK = 256
FFN_CONV = 3

LANES = 128
SUBLANES = 8
W_IN_PAD = 2688
VMEM_LIMIT = 56 * 1024 * 1024


def _cparams(n_axes):
    return pltpu.CompilerParams(dimension_semantics=("parallel",) * n_axes,
                                vmem_limit_bytes=VMEM_LIMIT)


def _rms(x, w):
    return x * lax.rsqrt(jnp.mean(x * x, axis=-1, keepdims=True) + EPS) * w


def _silu(x):
    return x * jax.nn.sigmoid(x)


def _row_tile(n):
    return min(512, n)


def _mod_kernel(c_ref, w_ref, b_ref, o_ref):
    a = _silu(c_ref[...])
    o_ref[0] = jnp.dot(a, w_ref[0], preferred_element_type=F32, precision=HI) + b_ref[0]


def _modulation(cc, mod_w, mod_b):
    nl, d, n = mod_w.shape
    r = cc.shape[0]
    tn = 512
    return pl.pallas_call(
        _mod_kernel,
        grid=(nl, n // tn),
        in_specs=[pl.BlockSpec((r, d), lambda l, j: (0, 0)),
                  pl.BlockSpec((1, d, tn), lambda l, j: (l, 0, j)),
                  pl.BlockSpec((1, 1, tn), lambda l, j: (l, 0, j))],
        out_specs=pl.BlockSpec((1, r, tn), lambda l, j: (l, 0, j)),
        out_shape=jax.ShapeDtypeStruct((nl, r, n), F32),
        compiler_params=_cparams(2),
        name="modulation",
    )(cc, mod_w, mod_b.reshape(nl, 1, n))


def _in_proj_kernel(*refs, rope):
    (x_ref, xp_ref, xn_ref, sh_ref, sc_ref, nw_ref, w_ref, cw_ref, cb_ref) = refs[:9]
    cos_ref, sin_ref = refs[9:11] if rope else (None, None)
    zx_ref, dt_ref, q_ref, k_ref, v_ref, u_ref, u16_ref, pad_ref = refs[-8:]
    i = pl.program_id(1)
    tm = x_ref.shape[1]
    x_ext = jnp.concatenate([x_ref[0], xp_ref[0], xn_ref[0]], axis=0)
    h_ext = _rms(x_ext, nw_ref[...]) * (1.0 + sc_ref[0]) + sh_ref[0]
    row_ext = lax.broadcasted_iota(jnp.int32, (tm + 2 * SUBLANES, 1), 0)
    outside = (((row_ext >= tm) & (row_ext < tm + SUBLANES) & (i == 0))
               | ((row_ext >= tm + SUBLANES) & (i == pl.num_programs(1) - 1)))
    h_ext = jnp.where(outside, 0.0, h_ext).astype(BF16)
    hb = h_ext[0:tm]

    def mm(lo, hi):
        return jnp.dot(hb, w_ref[:, lo:hi], preferred_element_type=F32)

    xbc = jnp.dot(h_ext, w_ref[:, SSD_WIDTH:SSD_WIDTH + SSD_XBC], preferred_element_type=F32)
    pad_ref[0:SUBLANES, :] = xbc[tm:tm + SUBLANES]
    pad_ref[SUBLANES:SUBLANES + tm, :] = xbc[0:tm]
    pad_ref[SUBLANES + tm:2 * SUBLANES + tm, :] = xbc[tm + SUBLANES:]

    def rot(p):
        lane = lax.broadcasted_iota(jnp.int32, p.shape, 1)
        partner = jnp.where(lane % 32 < 16, pltpu.roll(p, LANES - 16, 1), pltpu.roll(p, 16, 1))
        return p * cos_ref[...] + partner * sin_ref[...]

    o_zx = SSD_WIDTH + SSD_XBC
    zx_ref[0, :, 0:SSD_WIDTH] = mm(0, SSD_WIDTH)
    for j in range(DIFF_WIDTH // LANES):
        q = mm(o_zx + j * LANES, o_zx + (j + 1) * LANES) * (DIFF_HEAD_DIM ** -0.5)
        k = mm(o_zx + DIFF_WIDTH + j * LANES, o_zx + DIFF_WIDTH + (j + 1) * LANES)
        if rope:
            q, k = rot(q), rot(k)
        q_ref[0, :, j * LANES:(j + 1) * LANES] = q.astype(BF16)
        k_ref[0, j * LANES:(j + 1) * LANES, :] = k.T.astype(BF16)
    o_v = o_zx + 2 * DIFF_WIDTH
    v_ref[0] = mm(o_v, o_v + DIFF_WIDTH).astype(BF16)
    o_u = o_v + DIFF_WIDTH
    u = mm(o_u, o_u + S5_WIDTH)
    u_ref[0] = u
    u16_ref[0] = u.astype(BF16)
    dt_ref[0] = mm(o_u + S5_WIDTH, W_IN_PAD)
    halo = (SSD_CONV - 1) // 2
    for r in range(0, tm, CONV_ROWS):
        acc = jnp.broadcast_to(cb_ref[...], (CONV_ROWS, SSD_XBC))
        for k in range(SSD_CONV):
            lo = r + SUBLANES + k - halo
            acc = acc + cw_ref[k:k + 1, :] * pad_ref[lo:lo + CONV_ROWS, :]
        zx_ref[0, r:r + CONV_ROWS, SSD_WIDTH:] = _silu(acc)


def _in_proj(x, shift, scale, norm_w, w_merged, conv_w, conv_b, rope_tabs):
    bsz, n, d = x.shape
    tm = _row_tile(n)
    nb8 = n // SUBLANES
    tb8 = tm // SUBLANES
    rope = rope_tabs is not None
    row = lambda b, i: (b, i, 0)
    per_b = lambda b, i: (b, 0, 0)
    const = lambda b, i: (0, 0)
    in_specs = [pl.BlockSpec((1, tm, d), row),
                pl.BlockSpec((1, SUBLANES, d), lambda b, i: (b, jnp.maximum(i * tb8 - 1, 0), 0)),
                pl.BlockSpec((1, SUBLANES, d), lambda b, i: (b, jnp.minimum((i + 1) * tb8, nb8 - 1), 0)),
                pl.BlockSpec((1, 1, d), per_b),
                pl.BlockSpec((1, 1, d), per_b),
                pl.BlockSpec((1, d), const),
                pl.BlockSpec((d, W_IN_PAD), const),
                pl.BlockSpec((SSD_CONV, SSD_XBC), const),
                pl.BlockSpec((1, SSD_XBC), const)]
    args = [x, x, x, shift, scale, norm_w.reshape(1, d), w_merged, conv_w, conv_b.reshape(1, -1)]
    if rope:
        in_specs += [pl.BlockSpec((tm, LANES), lambda b, i: (i, 0))] * 2
        args += list(rope_tabs)
    widths = (SSD_WIDTH + SSD_XBC, LANES, DIFF_WIDTH, DIFF_WIDTH, DIFF_WIDTH, S5_WIDTH, S5_WIDTH)
    dtypes = (F32, F32, BF16, BF16, BF16, F32, BF16)
    out_specs = [pl.BlockSpec((1, tm, w), row) for w in widths]
    out_shape = [jax.ShapeDtypeStruct((bsz, n, w), t) for w, t in zip(widths, dtypes)]
    out_specs[3] = pl.BlockSpec((1, DIFF_WIDTH, tm), lambda b, i: (b, 0, i))
    out_shape[3] = jax.ShapeDtypeStruct((bsz, DIFF_WIDTH, n), BF16)
    return pl.pallas_call(
        functools.partial(_in_proj_kernel, rope=rope),
        grid=(bsz, n // tm),
        in_specs=in_specs,
        out_specs=out_specs,
        out_shape=out_shape,
        scratch_shapes=[pltpu.VMEM((tm + 2 * SUBLANES, SSD_XBC), F32)],
        compiler_params=_cparams(2),
        name="in_proj_rope" if rope else "in_proj",
    )(*args)


def _merge_w_in(w):
    o1, o2 = SSD_IN, SSD_IN + DIFF_IN
    parts = [w[:, :SSD_WIDTH + SSD_XBC], w[:, o1:o2], w[:, o2:], w[:, SSD_WIDTH + SSD_XBC:o1]]
    m = jnp.concatenate(parts, axis=1)
    return jnp.pad(m, ((0, 0), (0, W_IN_PAD - m.shape[1]))).astype(BF16)


def _rope_tables(n):
    t = jnp.arange(n, dtype=jnp.int32)
    pos = jnp.stack([(t // GRID_W).astype(F32), (t % GRID_W).astype(F32)], axis=1)
    quarter = DIFF_HEAD_DIM // 4
    inv_freq = ROPE_THETA ** (-jnp.arange(quarter, dtype=F32) / quarter)
    lane = np.arange(LANES)
    axis = (lane % DIFF_HEAD_DIM) // (DIFF_HEAD_DIM // 2)
    freq = lane % quarter
    ang = pos[:, axis] * inv_freq[freq][None, :]
    sign = np.where(lane % (2 * quarter) < quarter, -1.0, 1.0).astype(np.float32)
    return jnp.cos(ang), jnp.sin(ang) * sign[None, :]


def _cumsum_rows(x, reverse):
    n = x.shape[0]
    row = lax.broadcasted_iota(jnp.int32, x.shape, 0)
    k = 1
    while k < n:
        if reverse:
            x = x + jnp.where(row < n - k, pltpu.roll(x, n - k, 0), 0.0)
        else:
            x = x + jnp.where(row >= k, pltpu.roll(x, k, 0), 0.0)
        k *= 2
    return x


def _ssd_chunk(xact_ref, dtt_ref, dta_ref, exp_ref, state_ref, y_ref, c, direction):
    q = SSD_CHUNK
    nh = SSD_HEADS
    r0 = pl.multiple_of(c * q, q)
    xa = xact_ref[pl.ds(r0, q), :]
    dta = dta_ref[pl.ds(r0, q), :]
    dt_t = dtt_ref[c]
    rows = lax.broadcasted_iota(jnp.int32, (q, q), 0)
    cols = lax.broadcasted_iota(jnp.int32, (q, q), 1)
    lane = cols
    mask = (rows >= cols) if direction == 0 else (rows <= cols)
    acs = _cumsum_rows(dta, reverse=direction == 1)
    acs_t = acs.T
    hi = acs.astype(BF16).astype(F32)
    r1 = acs - hi
    mid = r1.astype(BF16).astype(F32)
    lo = (r1 - mid).astype(BF16).astype(F32)
    parts = hi + pltpu.roll(mid, 8, 1) + pltpu.roll(lo, 16, 1)
    neg = -pltpu.roll(parts, 24, 1)
    lhs = jnp.where(lane < 24, parts, jnp.where(lane < 48, 1.0, 0.0)).astype(BF16)
    rhs = []
    for h in range(nh):
        j = direction * nh + h
        is_one = (lane == j) | (lane == 8 + j) | (lane == 16 + j)
        is_neg = (lane == 24 + j) | (lane == 32 + j) | (lane == 40 + j)
        rhs.append(jnp.where(is_one, 1.0, jnp.where(is_neg, neg, 0.0)).astype(BF16))
    diff = lax.dot_general(lhs, jnp.concatenate(rhs, axis=0), (((1,), (1,)), ((), ())),
                           preferred_element_type=F32)
    mask4 = jnp.concatenate([mask] * nh, axis=1)
    decay = jnp.exp(jnp.where(mask4, diff, -jnp.inf))
    acs_x = jnp.dot(parts.astype(BF16), exp_ref[direction], preferred_element_type=F32)
    last = q - 1 if direction == 0 else 0
    total_x = acs_x[last:last + 1, :]

    xb = xa[:, 0:SSD_WIDTH].astype(BF16)
    bm = xa[:, SSD_WIDTH:SSD_WIDTH + LANES]
    cm = xa[:, SSD_WIDTH + LANES:]
    bm_b = bm.astype(BF16)
    bm_t = bm.T
    state = state_ref[direction]
    y_off = jnp.dot(cm.astype(BF16), state.astype(BF16), preferred_element_type=F32) * jnp.exp(acs_x)
    row_grp = lax.broadcasted_iota(jnp.int32, (LANES, SSD_WIDTH), 0) // SSD_STATE
    lane_head = lax.broadcasted_iota(jnp.int32, (LANES, SSD_WIDTH), 1) // SSD_HEAD_DIM
    upd = jnp.zeros((LANES, SSD_WIDTH), F32)
    y_diag = []
    for g in range(SSD_GROUPS):
        cg = jnp.where(lane // SSD_STATE == g, cm, 0.0).astype(BF16)
        cb = lax.dot_general(cg, bm_b, (((1,), (1,)), ((), ())), preferred_element_type=F32)
        xg = xb[:, g * LANES:(g + 1) * LANES]
        pair = []
        for h in range(g * 2, g * 2 + 2):
            j = direction * nh + h
            row_dt = dt_t[j:j + 1, :]
            scores = (cb * decay[:, h * q:(h + 1) * q] * row_dt).astype(BF16)
            pair.append(jnp.dot(scores, xg, preferred_element_type=F32))
            w_row = row_dt * jnp.exp(acs_t[j:j + 1, last:last + 1] - acs_t[j:j + 1, :])
            upd_h = jnp.dot((bm_t * w_row).astype(BF16), xb, preferred_element_type=F32)
            upd = upd + jnp.where((row_grp == g) & (lane_head == h), upd_h, 0.0)
        y_diag.append(jnp.where(lane < SSD_HEAD_DIM, pair[0], pair[1]))
    y_ref[pl.ds(r0, q), :] += jnp.concatenate(y_diag, axis=-1) + y_off
    state_ref[direction] = jnp.exp(total_x) * state + upd


def _ssd_kernel(*refs, n_ctx, n_lat, need_ctx):
    (zx_l, dt_l, zx_c, dt_c, dtb_ref, alog_ref, d_ref, nw_ref, exp_ref) = refs[:9]
    outs = refs[9:11] if need_ctx else (refs[9], None)
    xact_ref, dtt_ref, dta_ref, y_ref, state_ref = refs[-5:]
    out_l, out_c = outs

    for seg_ref, n, base in ((zx_c, n_ctx, 0), (zx_l, n_lat, n_ctx)):
        def copy_rows(i, carry, seg_ref=seg_ref, base=base):
            r = pl.multiple_of(i * SSD_CHUNK, SSD_CHUNK)
            act = seg_ref[0, pl.ds(r, SSD_CHUNK), SSD_WIDTH:]
            xact_ref[pl.ds(base + r, SSD_CHUNK), :] = act
            y_ref[pl.ds(base + r, SSD_CHUNK), :] = act[:, 0:SSD_WIDTH] * d_ref[...]
            return carry

        lax.fori_loop(0, n // SSD_CHUNK, copy_rows, 0)
    a_neg = -jnp.exp(alog_ref[...])
    head_lane = lax.broadcasted_iota(jnp.int32, (SSD_CHUNK, LANES), 1) < 2 * SSD_HEADS
    for seg_ref, n, base in ((dt_c, n_ctx, 0), (dt_l, n_lat, n_ctx)):
        for t0 in range(0, n, SSD_CHUNK):
            dt = jax.nn.softplus(seg_ref[0, t0:t0 + SSD_CHUNK, :] + dtb_ref[...])
            dtt_ref[(base + t0) // SSD_CHUNK] = dt.T
            dta_ref[base + t0:base + t0 + SSD_CHUNK, :] = jnp.where(head_lane, dt * a_neg, 0.0)
    n_tot = n_ctx + n_lat
    state_ref[...] = jnp.zeros(state_ref.shape, F32)

    ncc, nct = n_ctx // SSD_CHUNK, n_tot // SSD_CHUNK

    def step(i, carry):
        _ssd_chunk(xact_ref, dtt_ref, dta_ref, exp_ref, state_ref, y_ref, i, 0)
        cb_idx = jnp.where(i < ncc, ncc - 1 - i, nct - 1 - (i - ncc))
        _ssd_chunk(xact_ref, dtt_ref, dta_ref, exp_ref, state_ref, y_ref, cb_idx, 1)
        return carry

    lax.fori_loop(0, nct, step, 0, unroll=3)

    for seg_ref, o_ref, n, base in ((zx_c, out_c, n_ctx, 0), (zx_l, out_l, n_lat, n_ctx)):
        if o_ref is None:
            continue
        def gate_rows(i, carry, seg_ref=seg_ref, o_ref=o_ref, base=base):
            r = pl.multiple_of(i * SSD_ROWS, SSD_ROWS)
            z = seg_ref[0, pl.ds(r, SSD_ROWS), 0:SSD_WIDTH]
            v = y_ref[pl.ds(base + r, SSD_ROWS), :] * _silu(z)
            o_ref[0, pl.ds(r, SSD_ROWS), :] = _rms(v, nw_ref[...]).astype(BF16)
            return carry

        lax.fori_loop(0, n // SSD_ROWS, gate_rows, 0, unroll=4)


def _ssd(zx_l, dt_l, zx_c, dt_c, dt_bias, a_log, d_skip, norm_w, need_ctx):
    bsz, n_lat, wzx = zx_l.shape
    n_ctx = zx_c.shape[1]
    n_tot = n_ctx + n_lat
    pad8 = lambda v: jnp.pad(v.reshape(1, -1), ((0, 0), (0, LANES - v.size)))
    per_b = lambda b: (b, 0, 0)
    const = lambda b: (0, 0)
    k_idx = np.arange(LANES)[None, :, None]
    d_idx = np.arange(2)[:, None, None]
    h_idx = (np.arange(SSD_WIDTH) // SSD_HEAD_DIM)[None, None, :]
    expand = jnp.asarray((k_idx < 24) & (k_idx % 8 == d_idx * SSD_HEADS + h_idx), BF16)
    params = [pad8(dt_bias), pad8(a_log),
              jnp.repeat(d_skip, SSD_HEAD_DIM).reshape(1, -1), norm_w.reshape(1, -1)]
    in_specs = [pl.BlockSpec((1, n_lat, wzx), per_b), pl.BlockSpec((1, n_lat, LANES), per_b),
                pl.BlockSpec((1, n_ctx, wzx), per_b), pl.BlockSpec((1, n_ctx, LANES), per_b)]
    in_specs += [pl.BlockSpec(p.shape, const) for p in params]
    in_specs += [pl.BlockSpec(expand.shape, lambda b: (0, 0, 0))]
    out_specs = [pl.BlockSpec((1, n_lat, SSD_WIDTH), per_b)]
    out_shape = [jax.ShapeDtypeStruct((bsz, n_lat, SSD_WIDTH), BF16)]
    if need_ctx:
        out_specs.append(pl.BlockSpec((1, n_ctx, SSD_WIDTH), per_b))
        out_shape.append(jax.ShapeDtypeStruct((bsz, n_ctx, SSD_WIDTH), BF16))
    scratch = [pltpu.VMEM((n_tot, SSD_XBC), F32),
               pltpu.VMEM((n_tot // SSD_CHUNK, LANES, SSD_CHUNK), F32),
               pltpu.VMEM((n_tot, LANES), F32),
               pltpu.VMEM((n_tot, SSD_WIDTH), F32),
               pltpu.VMEM((2, SSD_GROUPS * SSD_STATE, SSD_WIDTH), F32)]
    outs = pl.pallas_call(
        functools.partial(_ssd_kernel, n_ctx=n_ctx, n_lat=n_lat, need_ctx=need_ctx),
        grid=(bsz,),
        in_specs=in_specs,
        out_specs=out_specs,
        out_shape=out_shape,
        scratch_shapes=scratch,
        compiler_params=_cparams(1),
        name="ssd_mixer",
    )(zx_l, dt_l, zx_c, dt_c, *params, expand)
    return (outs[0], outs[1]) if need_ctx else (outs[0], None)


def _attn_kernel(*refs, n_kv, lam_init):
    q_ref = refs[0]
    k_refs = refs[1:1 + n_kv]
    v_refs = refs[1 + n_kv:1 + 2 * n_kv]
    lq1, lk1, lq2, lk2, sw_ref, o_ref = refs[1 + 2 * n_kv:-4]
    s_refs, e_refs = refs[-4:-2], refs[-2:]
    q = q_ref[0]
    hw = q.shape[1]
    lane = lax.broadcasted_iota(jnp.int32, q.shape, 1)
    lam = (jnp.exp(jnp.sum(lq1[...] * lk1[...], axis=-1, keepdims=True))
           - jnp.exp(jnp.sum(lq2[...] * lk2[...], axis=-1, keepdims=True)) + lam_init)
    offs = np.cumsum([0] + [k_ref.shape[2] for k_ref in k_refs])
    v_ext = [jnp.concatenate([v_ref[0], jnp.ones(v_ref.shape[1:], BF16)], axis=-1) for v_ref in v_refs]
    tq = q.shape[0]
    rows = max(ATTN_ROW_SPLIT, tq // 4)
    halves = [(lo, min(lo + rows, tq)) for lo in range(0, tq, rows)]
    for comp, s_ref in enumerate(s_refs):
        own = (lane >= DIFF_HEAD_DIM) if comp else (lane < DIFF_HEAD_DIM)
        qc = jnp.where(own, q, jnp.zeros_like(q))
        for i, k_ref in enumerate(k_refs):
            s_ref[:, offs[i]:offs[i + 1]] = jnp.dot(qc, k_ref[0], preferred_element_type=F32)
    outs = []
    for s_ref, e_ref in zip(s_refs, e_refs):
        parts = []
        for lo, hi in halves:
            s = s_ref[lo:hi, :]
            e_ref[lo:hi, :] = jnp.exp(s - jnp.max(s, axis=-1, keepdims=True)).astype(BF16)
            ov = None
            for i in range(n_kv):
                part = jnp.dot(e_ref[lo:hi, offs[i]:offs[i + 1]], v_ext[i], preferred_element_type=F32)
                ov = part if ov is None else ov + part
            parts.append(ov[:, 0:hw] * (1.0 / ov[:, hw:hw + 1]))
        outs.append(parts)
    for (lo, hi), o0, o1 in zip(halves, *outs):
        o = o0 - lam * o1
        o_ref[0, lo:hi, :] = (_rms(o, sw_ref[...]) * (1.0 - lam_init)).astype(BF16)


def _attention(q, ks, vs, lam_params, subln_w, lam_init):
    bsz, n, _ = q.shape
    tq = min(ATTN_Q_TILE, n)
    n_kv = len(ks)
    n_keys = sum(a.shape[2] for a in ks)
    hw = 2 * DIFF_HEAD_DIM
    in_specs = [pl.BlockSpec((1, tq, hw), lambda b, h, i: (b, i, h))]
    in_specs += [pl.BlockSpec((1, hw, a.shape[2]), lambda b, h, i: (b, h, 0)) for a in ks]
    in_specs += [pl.BlockSpec((1, a.shape[1], hw), lambda b, h, i: (b, 0, h)) for a in vs]
    in_specs += [pl.BlockSpec((1, DIFF_HEAD_DIM), lambda b, h, i: (0, 0))] * 4
    in_specs += [pl.BlockSpec((1, hw), lambda b, h, i: (0, 0))]
    return pl.pallas_call(
        functools.partial(_attn_kernel, n_kv=n_kv, lam_init=lam_init),
        grid=(bsz, DIFF_HEADS, n // tq),
        in_specs=in_specs,
        out_specs=pl.BlockSpec((1, tq, hw), lambda b, h, i: (b, i, h)),
        out_shape=jax.ShapeDtypeStruct((bsz, n, DIFF_WIDTH), BF16),
        scratch_shapes=[pltpu.VMEM((tq, n_keys), F32)] * 2 + [pltpu.VMEM((tq, n_keys), BF16)] * 2,
        compiler_params=_cparams(3),
        name="diff_attention",
    )(q, *ks, *vs, *[p.reshape(1, -1) for p in lam_params], subln_w.reshape(1, -1))


def _s5_matrices(lam_re, lam_im, log_step, b_re, b_im, c_re, c_im):
    t = S5_T
    step = jnp.exp(log_step.astype(F32))[..., None]
    lr, li = lam_re.astype(F32), lam_im.astype(F32)
    mag = jnp.exp(lr * step)
    a = (mag * jnp.cos(li * step)) + 1j * (mag * jnp.sin(li * step))
    lam = lr + 1j * li
    bb = ((a - 1.0) / lam)[..., None] * (b_re.astype(F32) + 1j * b_im.astype(F32))
    cc = c_re.astype(F32) + 1j * c_im.astype(F32)
    pw = [jnp.ones_like(a)]
    for _ in range(t):
        pw.append(pw[-1] * a)
    pw = jnp.stack(pw, axis=0)
    kern = jnp.real(jnp.einsum('dgcn,tdgn,dgnk->dgtck', cc, pw[:t], bb))
    s_idx = np.arange(t)[:, None]
    t_idx = np.arange(t)[None, :]
    lag = t_idx - s_idx
    kf = jnp.where((lag >= 0)[None, :, :, None, None], kern[0][:, np.abs(lag)], 0.0)
    kb = jnp.where((lag <= 0)[None, :, :, None, None], kern[1][:, np.abs(lag)], 0.0)
    k_full = (kf + kb).transpose(0, 1, 4, 2, 3).reshape(S5_GROUPS, t * S5_GROUP, t * S5_GROUP)
    wf = pw[t - 1 - np.arange(t), 0][..., None] * bb[0][None]
    wb = pw[np.arange(t), 1][..., None] * bb[1][None]
    def rows_sc(w):
        return w.transpose(1, 0, 3, 2).reshape(S5_GROUPS, t * S5_GROUP, S5_STATE)
    w_state = jnp.concatenate([rows_sc(jnp.real(wf)), rows_sc(jnp.real(wb)),
                               rows_sc(jnp.imag(wf)), rows_sc(jnp.imag(wb))], axis=-1)
    cf = cc[0][None] * pw[1 + np.arange(t), 0][:, :, None, :]
    cbk = cc[1][None] * pw[t - np.arange(t), 1][:, :, None, :]
    def cols_tc(w):
        return w.transpose(1, 3, 0, 2).reshape(S5_GROUPS, S5_STATE, t * S5_GROUP)
    c_off = jnp.concatenate([cols_tc(jnp.real(cf)), cols_tc(jnp.real(cbk)),
                             cols_tc(-jnp.imag(cf)), cols_tc(-jnp.imag(cbk))], axis=1)
    at = pw[t]
    a_rows = jnp.stack([jnp.concatenate([jnp.real(at[0]), jnp.real(at[1])], axis=-1),
                        jnp.concatenate([jnp.imag(at[0]), jnp.imag(at[1])], axis=-1)], axis=1)
    a_chunk = jnp.pad(a_rows, ((0, 0), (0, SUBLANES - 2), (0, 0)))
    return k_full.astype(BF16), w_state.astype(BF16), c_off.astype(BF16), a_chunk.astype(F32)


def _s5_kernel(uc_ref, ul_ref, kf_ref, ws_ref, co_ref, a_ref, yc_ref, yl_ref, e_ref, h_ref):
    ncc, bsz, w = uc_ref.shape[1:]
    ncl = ul_ref.shape[1]
    nct = ncc + ncl
    uc = uc_ref[0].reshape(ncc * bsz, w)
    ul = ul_ref[0].reshape(ncl * bsz, w)
    e_ref[0:ncc] = jnp.dot(uc, ws_ref[0], preferred_element_type=F32).reshape(ncc, bsz, w)
    e_ref[ncc:nct] = jnp.dot(ul, ws_ref[0], preferred_element_type=F32).reshape(ncl, bsz, w)
    ar = a_ref[0, 0:1, :]
    ai = a_ref[0, 1:2, :]
    half = S5_STATE
    lane = lax.broadcasted_iota(jnp.int32, (bsz, LANES), 1)
    is_f = lane < half

    def step(i, carry):
        h_re, h_im = carry
        jf = i
        jb = jnp.where(i < ncc, ncc - 1 - i, nct - 1 - (i - ncc))
        h_ref[jf, :, 0:half] = h_re[:, 0:half]
        h_ref[jb, :, half:LANES] = h_re[:, half:]
        h_ref[jf, :, LANES:LANES + half] = h_im[:, 0:half]
        h_ref[jb, :, LANES + half:] = h_im[:, half:]
        e_re = jnp.where(is_f, e_ref[jf, :, 0:LANES], e_ref[jb, :, 0:LANES])
        e_im = jnp.where(is_f, e_ref[jf, :, LANES:], e_ref[jb, :, LANES:])
        return (ar * h_re - ai * h_im + e_re, ar * h_im + ai * h_re + e_im)

    zero = jnp.zeros((bsz, LANES), F32)
    lax.fori_loop(0, nct, step, (zero, zero))
    for u, y_ref, lo, nc in ((uc, yc_ref, 0, ncc), (ul, yl_ref, ncc, ncl)):
        hs = h_ref[lo:lo + nc].reshape(nc * bsz, w).astype(BF16)
        y = (jnp.dot(u, kf_ref[0], preferred_element_type=F32)
             + jnp.dot(hs, co_ref[0], preferred_element_type=F32))
        y_ref[0] = y.reshape(nc, bsz, w).astype(BF16)


def _s5_to_chunks(u):
    bsz, n, _ = u.shape
    u = u.reshape(bsz, n // S5_T, S5_T, S5_GROUPS, S5_GROUP).transpose(3, 1, 0, 2, 4)
    return u.reshape(S5_GROUPS, n // S5_T, bsz, S5_T * S5_GROUP)


def _s5_from_chunks(y, bsz):
    nchunk = y.shape[1]
    y = y.reshape(S5_GROUPS, nchunk, bsz, S5_T, S5_GROUP).transpose(2, 1, 3, 0, 4)
    return y.reshape(bsz, nchunk * S5_T, S5_WIDTH)


def _s5(u_l, u_c, mats):
    k_full, w_state, c_off, a_chunk = mats
    bsz = u_l.shape[0]
    uc, ul = _s5_to_chunks(u_c), _s5_to_chunks(u_l)
    ncc, ncl, w = uc.shape[1], ul.shape[1], ul.shape[3]
    nct = ncc + ncl
    blk = lambda g: (g, 0, 0, 0)
    mat = lambda g: (g, 0, 0)
    y_c, y_l = pl.pallas_call(
        _s5_kernel,
        grid=(S5_GROUPS,),
        in_specs=[pl.BlockSpec((1, ncc, bsz, w), blk), pl.BlockSpec((1, ncl, bsz, w), blk),
                  pl.BlockSpec((1, w, w), mat), pl.BlockSpec((1, w, w), mat),
                  pl.BlockSpec((1, w, w), mat), pl.BlockSpec((1, SUBLANES, LANES), mat)],
        out_specs=[pl.BlockSpec((1, ncc, bsz, w), blk), pl.BlockSpec((1, ncl, bsz, w), blk)],
        out_shape=[jax.ShapeDtypeStruct((S5_GROUPS, ncc, bsz, w), BF16),
                   jax.ShapeDtypeStruct((S5_GROUPS, ncl, bsz, w), BF16)],
        scratch_shapes=[pltpu.VMEM((nct, bsz, w), F32), pltpu.VMEM((nct, bsz, w), F32)],
        compiler_params=_cparams(1),
        name="s5_mixer",
    )(uc, ul, k_full, w_state, c_off, a_chunk)
    return _s5_from_chunks(y_l, bsz), _s5_from_chunks(y_c, bsz)


def _out_proj_kernel(a_ref, b_ref, ys_ref, u_ref, x_ref, gate_ref, d_ref, gw_ref, gb_ref,
                     w_ref, nw_ref, o_ref):
    y = jax.nn.gelu(ys_ref[0] + d_ref[...] * u_ref[0], approximate=True)
    gate = jnp.dot(y.astype(BF16), gw_ref[...], preferred_element_type=F32) + gb_ref[...]
    s = (y * jax.nn.sigmoid(gate)).astype(BF16)
    o1, o2 = SSD_WIDTH, SSD_WIDTH + DIFF_WIDTH
    mix = (jnp.dot(a_ref[0], w_ref[0:o1, :], preferred_element_type=F32)
           + jnp.dot(b_ref[0], w_ref[o1:o2, :], preferred_element_type=F32)
           + jnp.dot(s, w_ref[o2:, :], preferred_element_type=F32))
    o_ref[0] = x_ref[0] + gate_ref[0] * _rms(mix, nw_ref[...])


def _out_proj(a, b, ys, u, x, gate, s5_d, glu_w, glu_b, w_out, norm_w):
    bsz, n, d = x.shape
    tm = _row_tile(n)
    row = lambda bi, i: (bi, i, 0)
    per_b = lambda bi, i: (bi, 0, 0)
    const = lambda bi, i: (0, 0)
    return pl.pallas_call(
        _out_proj_kernel,
        grid=(bsz, n // tm),
        in_specs=[pl.BlockSpec((1, tm, SSD_WIDTH), row), pl.BlockSpec((1, tm, DIFF_WIDTH), row),
                  pl.BlockSpec((1, tm, S5_WIDTH), row), pl.BlockSpec((1, tm, S5_WIDTH), row),
                  pl.BlockSpec((1, tm, d), row), pl.BlockSpec((1, 1, d), per_b),
                  pl.BlockSpec((1, S5_WIDTH), const), pl.BlockSpec((S5_WIDTH, S5_WIDTH), const),
                  pl.BlockSpec((1, S5_WIDTH), const), pl.BlockSpec((d, d), const),
                  pl.BlockSpec((1, d), const)],
        out_specs=pl.BlockSpec((1, tm, d), row),
        out_shape=jax.ShapeDtypeStruct((bsz, n, d), F32),
        compiler_params=_cparams(2),
        name="out_proj",
    )(a, b, ys, u, x, gate, s5_d.reshape(1, -1), glu_w.astype(BF16), glu_b.reshape(1, -1),
      w_out.astype(BF16), norm_w.reshape(1, -1))


def _ffn_kernel(x_ref, xp_ref, xn_ref, sh_ref, sc_ref, gate_ref, nw1_ref, nw2_ref,
                wg_ref, wu_ref, cw_ref, cb_ref, wd_ref, o_ref, act_ref):
    i = pl.program_id(1)
    is_first = i == 0
    is_last = i == pl.num_programs(1) - 1
    tm = x_ref.shape[1]
    x = x_ref[0]
    x_ext = jnp.concatenate([x, xp_ref[0], xn_ref[0]], axis=0)
    h_ext = _rms(x_ext, nw1_ref[...]) * (1.0 + sc_ref[0]) + sh_ref[0]
    row_ext = lax.broadcasted_iota(jnp.int32, (tm + 2 * SUBLANES, 1), 0)
    outside = (((row_ext >= tm) & (row_ext < tm + SUBLANES) & is_first)
               | ((row_ext >= tm + SUBLANES) & is_last))
    h_ext = jnp.where(outside, 0.0, h_ext).astype(BF16)
    h = h_ext[0:tm]
    row = lax.broadcasted_iota(jnp.int32, (tm, 1), 0)
    for lo in range(0, wg_ref.shape[1], FFN_CHUNK):
        hi = lo + FFN_CHUNK
        g_ext = jnp.dot(h_ext, wg_ref[:, lo:hi], preferred_element_type=F32)
        g = g_ext[0:tm]
        g_prev = jnp.where(row == 0, g_ext[tm + SUBLANES - 1:tm + SUBLANES], pltpu.roll(g, 1, 0))
        g_next = jnp.where(row == tm - 1, g_ext[tm + SUBLANES:tm + SUBLANES + 1], pltpu.roll(g, tm - 1, 0))
        conv = (cw_ref[0:1, lo:hi] * g_prev + cw_ref[1:2, lo:hi] * g + cw_ref[2:3, lo:hi] * g_next
                + cb_ref[:, lo:hi])
        up = jnp.dot(h, wu_ref[:, lo:hi], preferred_element_type=F32)
        act_ref[:, lo:hi] = (_silu(conv) * up).astype(BF16)
    f = jnp.dot(act_ref[...], wd_ref[...], preferred_element_type=F32)
    o_ref[0] = x + gate_ref[0] * _rms(f, nw2_ref[...])


def _ffn(x, shift, scale, gate, norm_pre, norm_post, w_gate, w_up, conv_w, conv_b, w_down):
    bsz, n, d = x.shape
    tm = _row_tile(n)
    f = w_gate.shape[1]
    nb8 = n // SUBLANES
    tb8 = tm // SUBLANES
    row = lambda b, i: (b, i, 0)
    per_b = lambda b, i: (b, 0, 0)
    const = lambda b, i: (0, 0)
    return pl.pallas_call(
        _ffn_kernel,
        grid=(bsz, n // tm),
        in_specs=[pl.BlockSpec((1, tm, d), row),
                  pl.BlockSpec((1, SUBLANES, d), lambda b, i: (b, jnp.maximum(i * tb8 - 1, 0), 0)),
                  pl.BlockSpec((1, SUBLANES, d), lambda b, i: (b, jnp.minimum((i + 1) * tb8, nb8 - 1), 0)),
                  pl.BlockSpec((1, 1, d), per_b), pl.BlockSpec((1, 1, d), per_b),
                  pl.BlockSpec((1, 1, d), per_b),
                  pl.BlockSpec((1, d), const), pl.BlockSpec((1, d), const),
                  pl.BlockSpec((d, f), const), pl.BlockSpec((d, f), const),
                  pl.BlockSpec((FFN_CONV, f), const), pl.BlockSpec((1, f), const),
                  pl.BlockSpec((f, d), const)],
        out_specs=pl.BlockSpec((1, tm, d), row),
        out_shape=jax.ShapeDtypeStruct((bsz, n, d), F32),
        scratch_shapes=[pltpu.VMEM((tm, f), BF16)],
        compiler_params=_cparams(2),
        name="conv_ffn",
    )(x, x, x, shift, scale, gate, norm_pre.reshape(1, -1), norm_post.reshape(1, -1),
      w_gate.astype(BF16), w_up.astype(BF16), conv_w, conv_b.reshape(1, -1), w_down.astype(BF16))


def kernel(x, c, ctx, c_ctx, mod_w, mod_b, mix_norm_pre, mix_norm_post, ffn_norm_pre, ffn_norm_post, w_in, w_out, ssd_conv_w, ssd_conv_b, ssd_dt_bias, ssd_a_log, ssd_d, ssd_norm_w, diff_lam_q1, diff_lam_k1, diff_lam_q2, diff_lam_k2, diff_subln_w, s5_lam_re, s5_lam_im, s5_log_step, s5_b_re, s5_b_im, s5_c_re, s5_c_im, s5_d, s5_glu_w, s5_glu_b, ffn_w_gate, ffn_w_up, ffn_conv_w, ffn_conv_b, ffn_w_down):
    bsz, n_lat, d = x.shape
    n_layers = mod_w.shape[0]
    rope_tabs = _rope_tables(n_lat)
    cc = jnp.concatenate([c, jnp.broadcast_to(c_ctx[None, :], (SUBLANES, d))], axis=0)
    mod = _modulation(cc, mod_w, mod_b)

    for layer in range(n_layers):
        need_ctx = layer < n_layers - 1
        lam_init = 0.8 - 0.6 * math.exp(-0.3 * layer)
        mod_l = mod[layer, :bsz].reshape(bsz, N_MOD, 1, d)
        mod_c = jnp.broadcast_to(mod[layer, bsz].reshape(1, N_MOD, 1, d), (bsz, N_MOD, 1, d))
        sh_a, sc_a, g_a, sh_f, sc_f, g_f = (mod_l[:, i] for i in range(N_MOD))
        csh_a, csc_a, cg_a, csh_f, csc_f, cg_f = (mod_c[:, i] for i in range(N_MOD))

        w_merged = _merge_w_in(w_in[layer])
        in_p = (mix_norm_pre[layer], w_merged, ssd_conv_w[layer], ssd_conv_b[layer])
        zx_l, dt_l, q_l, k_l, v_l, u_l, u16_l = _in_proj(x, sh_a, sc_a, *in_p, rope_tabs)
        zx_c, dt_c, q_c, k_c, v_c, u_c, u16_c = _in_proj(ctx, csh_a, csc_a, *in_p, None)

        a_l, a_c = _ssd(zx_l, dt_l, zx_c, dt_c, ssd_dt_bias[layer], ssd_a_log[layer], ssd_d[layer],
                        ssd_norm_w[layer], need_ctx)

        lam_params = (diff_lam_q1[layer], diff_lam_k1[layer], diff_lam_q2[layer], diff_lam_k2[layer])
        b_l = _attention(q_l, (k_c, k_l), (v_c, v_l), lam_params, diff_subln_w[layer], lam_init)

        mats = _s5_matrices(s5_lam_re[layer], s5_lam_im[layer], s5_log_step[layer], s5_b_re[layer],
                            s5_b_im[layer], s5_c_re[layer], s5_c_im[layer])
        ys_l, ys_c = _s5(u16_l, u16_c, mats)

        mix_p = (s5_d[layer], s5_glu_w[layer], s5_glu_b[layer], w_out[layer], mix_norm_post[layer])
        ffn_p = (ffn_norm_pre[layer], ffn_norm_post[layer], ffn_w_gate[layer], ffn_w_up[layer],
                 ffn_conv_w[layer], ffn_conv_b[layer], ffn_w_down[layer])
        x = _out_proj(a_l, b_l, ys_l, u_l, x, g_a, *mix_p)
        x = _ffn(x, sh_f, sc_f, g_f, *ffn_p)
        if need_ctx:
            b_c = _attention(q_c, (k_c,), (v_c,), lam_params, diff_subln_w[layer], lam_init)
            ctx = _out_proj(a_c, b_c, ys_c, u_c, ctx, cg_a, *mix_p)
            ctx = _ffn(ctx, csh_f, csc_f, cg_f, *ffn_p)
    return x
```

```python
import functools
import math

import numpy as np
import jax
import jax.numpy as jnp
from jax import lax
from jax.experimental import pallas as pl
from jax.experimental.pallas import tpu as pltpu

F32 = jnp.float32
BF16 = jnp.bfloat16
HI = lax.Precision.HIGHEST

D_MODEL = 1024
N_LAYERS = 2
GRID_W = 64
EPS = 1e-6
N_MOD = 6
SSD_HEADS = 4
SSD_HEAD_DIM = 64
SSD_WIDTH = 256
SSD_GROUPS = 2
SSD_STATE = 64
SSD_CONV = 5
SSD_CHUNK = 128
SSD_ROWS = 64
CONV_ROWS = 64
SSD_XBC = 512
SSD_IN = 776
DIFF_HEADS = 4
DIFF_HEAD_DIM = 64
DIFF_WIDTH = 512
DIFF_IN = 1536
ROPE_THETA = 10000.0
ATTN_Q_TILE = 1024
ATTN_ROW_SPLIT = 128
S5_GROUP = 16
S5_WIDTH = 256
S5_GROUPS = 16
S5_STATE = 64
S5_T = 16
FFN_DIM = 2816
FFN_CHUNK = 256
FFN_ROW_TILE = 1024
HALO_ROWS = 16
FFN_CONV = 3

LANES = 128
SUBLANES = 8
W_IN_PAD = 2688
VMEM_LIMIT = 56 * 1024 * 1024


def _cparams(n_axes):
    return pltpu.CompilerParams(dimension_semantics=("parallel",) * n_axes,
                                vmem_limit_bytes=VMEM_LIMIT)


def _rms(x, w):
    return x * lax.rsqrt(jnp.mean(x * x, axis=-1, keepdims=True) + EPS) * w


def _silu(x):
    return x * jax.nn.sigmoid(x)


def _row_tile(n):
    return min(512, n)


def _mod_kernel(c_ref, w_ref, b_ref, o_ref):
    a = _silu(c_ref[...])
    o_ref[0] = jnp.dot(a, w_ref[0], preferred_element_type=F32, precision=HI) + b_ref[0]


def _modulation(cc, mod_w, mod_b):
    nl, d, n = mod_w.shape
    r = cc.shape[0]
    tn = 512
    return pl.pallas_call(
        _mod_kernel,
        grid=(nl, n // tn),
        in_specs=[pl.BlockSpec((r, d), lambda l, j: (0, 0)),
                  pl.BlockSpec((1, d, tn), lambda l, j: (l, 0, j)),
                  pl.BlockSpec((1, 1, tn), lambda l, j: (l, 0, j))],
        out_specs=pl.BlockSpec((1, r, tn), lambda l, j: (l, 0, j)),
        out_shape=jax.ShapeDtypeStruct((nl, r, n), F32),
        compiler_params=_cparams(2),
        name="modulation",
    )(cc, mod_w, mod_b.reshape(nl, 1, n))


def _in_proj_kernel(*refs, rope):
    (x_ref, xp_ref, xn_ref, sh_ref, sc_ref, nw_ref, w_ref, cw_ref, cb_ref) = refs[:9]
    cos_ref, sin_ref = refs[9:11] if rope else (None, None)
    zx_ref, dt_ref, q_ref, k_ref, v_ref, u_ref, u16_ref, pad_ref = refs[-8:]
    i = pl.program_id(1)
    tm = x_ref.shape[1]
    x_ext = jnp.concatenate([x_ref[0], xp_ref[0], xn_ref[0]], axis=0)
    h_ext = _rms(x_ext, nw_ref[...]) * (1.0 + sc_ref[0]) + sh_ref[0]
    row_ext = lax.broadcasted_iota(jnp.int32, (tm + 2 * SUBLANES, 1), 0)
    outside = (((row_ext >= tm) & (row_ext < tm + SUBLANES) & (i == 0))
               | ((row_ext >= tm + SUBLANES) & (i == pl.num_programs(1) - 1)))
    h_ext = jnp.where(outside, 0.0, h_ext).astype(BF16)
    hb = h_ext[0:tm]

    def mm(lo, hi):
        return jnp.dot(hb, w_ref[:, lo:hi], preferred_element_type=F32)

    xbc = jnp.dot(h_ext, w_ref[:, SSD_WIDTH:SSD_WIDTH + SSD_XBC], preferred_element_type=F32)
    pad_ref[0:SUBLANES, :] = xbc[tm:tm + SUBLANES]
    pad_ref[SUBLANES:SUBLANES + tm, :] = xbc[0:tm]
    pad_ref[SUBLANES + tm:2 * SUBLANES + tm, :] = xbc[tm + SUBLANES:]

    def rot(p):
        lane = lax.broadcasted_iota(jnp.int32, p.shape, 1)
        partner = jnp.where(lane % 32 < 16, pltpu.roll(p, LANES - 16, 1), pltpu.roll(p, 16, 1))
        return p * cos_ref[...] + partner * sin_ref[...]

    o_zx = SSD_WIDTH + SSD_XBC
    zx_ref[0, :, 0:SSD_WIDTH] = mm(0, SSD_WIDTH)
    for j in range(DIFF_WIDTH // LANES):
        q = mm(o_zx + j * LANES, o_zx + (j + 1) * LANES) * (DIFF_HEAD_DIM ** -0.5)
        k = mm(o_zx + DIFF_WIDTH + j * LANES, o_zx + DIFF_WIDTH + (j + 1) * LANES)
        if rope:
            q, k = rot(q), rot(k)
        q_ref[0, :, j * LANES:(j + 1) * LANES] = q.astype(BF16)
        k_ref[0, j * LANES:(j + 1) * LANES, :] = k.T.astype(BF16)
    o_v = o_zx + 2 * DIFF_WIDTH
    v_ref[0] = mm(o_v, o_v + DIFF_WIDTH).astype(BF16)
    o_u = o_v + DIFF_WIDTH
    u = mm(o_u, o_u + S5_WIDTH)
    u_ref[0] = u
    u16_ref[0] = u.astype(BF16)
    dt_ref[0] = mm(o_u + S5_WIDTH, W_IN_PAD)
    halo = (SSD_CONV - 1) // 2
    for r in range(0, tm, CONV_ROWS):
        acc = jnp.broadcast_to(cb_ref[...], (CONV_ROWS, SSD_XBC))
        for k in range(SSD_CONV):
            lo = r + SUBLANES + k - halo
            acc = acc + cw_ref[k:k + 1, :] * pad_ref[lo:lo + CONV_ROWS, :]
        zx_ref[0, r:r + CONV_ROWS, SSD_WIDTH:] = _silu(acc)


def _in_proj(x, shift, scale, norm_w, w_merged, conv_w, conv_b, rope_tabs):
    bsz, n, d = x.shape
    tm = _row_tile(n)
    nb8 = n // SUBLANES
    tb8 = tm // SUBLANES
    rope = rope_tabs is not None
    row = lambda b, i: (b, i, 0)
    per_b = lambda b, i: (b, 0, 0)
    const = lambda b, i: (0, 0)
    in_specs = [pl.BlockSpec((1, tm, d), row),
                pl.BlockSpec((1, SUBLANES, d), lambda b, i: (b, jnp.maximum(i * tb8 - 1, 0), 0)),
                pl.BlockSpec((1, SUBLANES, d), lambda b, i: (b, jnp.minimum((i + 1) * tb8, nb8 - 1), 0)),
                pl.BlockSpec((1, 1, d), per_b),
                pl.BlockSpec((1, 1, d), per_b),
                pl.BlockSpec((1, d), const),
                pl.BlockSpec((d, W_IN_PAD), const),
                pl.BlockSpec((SSD_CONV, SSD_XBC), const),
                pl.BlockSpec((1, SSD_XBC), const)]
    args = [x, x, x, shift, scale, norm_w.reshape(1, d), w_merged, conv_w, conv_b.reshape(1, -1)]
    if rope:
        in_specs += [pl.BlockSpec((tm, LANES), lambda b, i: (i, 0))] * 2
        args += list(rope_tabs)
    widths = (SSD_WIDTH + SSD_XBC, LANES, DIFF_WIDTH, DIFF_WIDTH, DIFF_WIDTH, S5_WIDTH, S5_WIDTH)
    dtypes = (F32, F32, BF16, BF16, BF16, F32, BF16)
    out_specs = [pl.BlockSpec((1, tm, w), row) for w in widths]
    out_shape = [jax.ShapeDtypeStruct((bsz, n, w), t) for w, t in zip(widths, dtypes)]
    out_specs[3] = pl.BlockSpec((1, DIFF_WIDTH, tm), lambda b, i: (b, 0, i))
    out_shape[3] = jax.ShapeDtypeStruct((bsz, DIFF_WIDTH, n), BF16)
    return pl.pallas_call(
        functools.partial(_in_proj_kernel, rope=rope),
        grid=(bsz, n // tm),
        in_specs=in_specs,
        out_specs=out_specs,
        out_shape=out_shape,
        scratch_shapes=[pltpu.VMEM((tm + 2 * SUBLANES, SSD_XBC), F32)],
        compiler_params=_cparams(2),
        name="in_proj_rope" if rope else "in_proj",
    )(*args)


def _merge_w_in(w):
    o1, o2 = SSD_IN, SSD_IN + DIFF_IN
    parts = [w[:, :SSD_WIDTH + SSD_XBC], w[:, o1:o2], w[:, o2:], w[:, SSD_WIDTH + SSD_XBC:o1]]
    m = jnp.concatenate(parts, axis=1)
    return jnp.pad(m, ((0, 0), (0, W_IN_PAD - m.shape[1]))).astype(BF16)


def _rope_tables(n):
    t = np.arange(n)
    pos = np.stack([t // GRID_W, t % GRID_W], axis=1).astype(np.float32)
    quarter = DIFF_HEAD_DIM // 4
    inv_freq = (np.float32(ROPE_THETA) ** (-np.arange(quarter, dtype=np.float32) / np.float32(quarter)))
    lane = np.arange(LANES)
    axis = (lane % DIFF_HEAD_DIM) // (DIFF_HEAD_DIM // 2)
    freq = lane % quarter
    ang = (pos[:, axis] * inv_freq.astype(np.float32)[freq][None, :]).astype(np.float32)
    sign = np.where(lane % (2 * quarter) < quarter, -1.0, 1.0).astype(np.float32)
    return jnp.asarray(np.cos(ang), F32), jnp.asarray(np.sin(ang) * sign[None, :], F32)


def _cumsum_rows(x, reverse):
    n = x.shape[0]
    row = lax.broadcasted_iota(jnp.int32, x.shape, 0)
    k = 1
    while k < n:
        if reverse:
            x = x + jnp.where(row < n - k, pltpu.roll(x, n - k, 0), 0.0)
        else:
            x = x + jnp.where(row >= k, pltpu.roll(x, k, 0), 0.0)
        k *= 2
    return x


def _ssd_chunk(xact_ref, dtt_ref, dta_ref, exp_ref, state_ref, y_ref, c, direction):
    q = SSD_CHUNK
    nh = SSD_HEADS
    r0 = pl.multiple_of(c * q, q)
    xa = xact_ref[pl.ds(r0, q), :]
    dta = dta_ref[pl.ds(r0, q), :]
    dt_t = dtt_ref[c]
    rows = lax.broadcasted_iota(jnp.int32, (q, q), 0)
    cols = lax.broadcasted_iota(jnp.int32, (q, q), 1)
    lane = cols
    mask = (rows >= cols) if direction == 0 else (rows <= cols)
    acs = _cumsum_rows(dta, reverse=direction == 1)
    acs_t = acs.T
    hi = acs.astype(BF16).astype(F32)
    r1 = acs - hi
    mid = r1.astype(BF16).astype(F32)
    lo = (r1 - mid).astype(BF16).astype(F32)
    parts = hi + pltpu.roll(mid, 8, 1) + pltpu.roll(lo, 16, 1)
    neg = -pltpu.roll(parts, 24, 1)
    lhs = jnp.where(lane < 24, parts, jnp.where(lane < 48, 1.0, 0.0)).astype(BF16)
    rhs = []
    for h in range(nh):
        j = direction * nh + h
        is_one = (lane == j) | (lane == 8 + j) | (lane == 16 + j)
        is_neg = (lane == 24 + j) | (lane == 32 + j) | (lane == 40 + j)
        rhs.append(jnp.where(is_one, 1.0, jnp.where(is_neg, neg, 0.0)).astype(BF16))
    diff = lax.dot_general(lhs, jnp.concatenate(rhs, axis=0), (((1,), (1,)), ((), ())),
                           preferred_element_type=F32)
    mask4 = jnp.concatenate([mask] * nh, axis=1)
    decay = jnp.exp(jnp.where(mask4, diff, -jnp.inf))
    acs_x = jnp.dot(parts.astype(BF16), exp_ref[direction], preferred_element_type=F32)
    last = q - 1 if direction == 0 else 0
    total_x = acs_x[last:last + 1, :]

    xb = xa[:, 0:SSD_WIDTH].astype(BF16)
    bm = xa[:, SSD_WIDTH:SSD_WIDTH + LANES]
    cm = xa[:, SSD_WIDTH + LANES:]
    bm_b = bm.astype(BF16)
    bm_t = bm.T
    state = state_ref[direction]
    y_off = jnp.dot(cm.astype(BF16), state.astype(BF16), preferred_element_type=F32) * jnp.exp(acs_x)
    row_grp = lax.broadcasted_iota(jnp.int32, (LANES, SSD_WIDTH), 0) // SSD_STATE
    lane_head = lax.broadcasted_iota(jnp.int32, (LANES, SSD_WIDTH), 1) // SSD_HEAD_DIM
    upd = jnp.zeros((LANES, SSD_WIDTH), F32)
    y_diag = []
    for g in range(SSD_GROUPS):
        cg = jnp.where(lane // SSD_STATE == g, cm, 0.0).astype(BF16)
        cb = lax.dot_general(cg, bm_b, (((1,), (1,)), ((), ())), preferred_element_type=F32)
        xg = xb[:, g * LANES:(g + 1) * LANES]
        pair = []
        for h in range(g * 2, g * 2 + 2):
            j = direction * nh + h
            row_dt = dt_t[j:j + 1, :]
            scores = (cb * decay[:, h * q:(h + 1) * q] * row_dt).astype(BF16)
            pair.append(jnp.dot(scores, xg, preferred_element_type=F32))
            w_row = row_dt * jnp.exp(acs_t[j:j + 1, last:last + 1] - acs_t[j:j + 1, :])
            upd_h = jnp.dot((bm_t * w_row).astype(BF16), xb, preferred_element_type=F32)
            upd = upd + jnp.where((row_grp == g) & (lane_head == h), upd_h, 0.0)
        y_diag.append(jnp.where(lane < SSD_HEAD_DIM, pair[0], pair[1]))
    y_ref[pl.ds(r0, q), :] += jnp.concatenate(y_diag, axis=-1) + y_off
    state_ref[direction] = jnp.exp(total_x) * state + upd


def _ssd_kernel(*refs, n_ctx, n_lat, need_ctx):
    (zx_l, dt_l, zx_c, dt_c, dtb_ref, alog_ref, d_ref, nw_ref, exp_ref) = refs[:9]
    outs = refs[9:11] if need_ctx else (refs[9], None)
    xact_ref, dtt_ref, dta_ref, y_ref, state_ref = refs[-5:]
    out_l, out_c = outs

    for seg_ref, n, base in ((zx_c, n_ctx, 0), (zx_l, n_lat, n_ctx)):
        def copy_rows(i, carry, seg_ref=seg_ref, base=base):
            r = pl.multiple_of(i * SSD_CHUNK, SSD_CHUNK)
            act = seg_ref[0, pl.ds(r, SSD_CHUNK), SSD_WIDTH:]
            xact_ref[pl.ds(base + r, SSD_CHUNK), :] = act
            y_ref[pl.ds(base + r, SSD_CHUNK), :] = act[:, 0:SSD_WIDTH] * d_ref[...]
            return carry

        lax.fori_loop(0, n // SSD_CHUNK, copy_rows, 0)
    a_neg = -jnp.exp(alog_ref[...])
    head_lane = lax.broadcasted_iota(jnp.int32, (SSD_CHUNK, LANES), 1) < 2 * SSD_HEADS
    for seg_ref, n, base in ((dt_c, n_ctx, 0), (dt_l, n_lat, n_ctx)):
        for t0 in range(0, n, SSD_CHUNK):
            dt = jax.nn.softplus(seg_ref[0, t0:t0 + SSD_CHUNK, :] + dtb_ref[...])
            dtt_ref[(base + t0) // SSD_CHUNK] = dt.T
            dta_ref[base + t0:base + t0 + SSD_CHUNK, :] = jnp.where(head_lane, dt * a_neg, 0.0)
    n_tot = n_ctx + n_lat
    state_ref[...] = jnp.zeros(state_ref.shape, F32)

    ncc, nct = n_ctx // SSD_CHUNK, n_tot // SSD_CHUNK

    def step(i, carry):
        _ssd_chunk(xact_ref, dtt_ref, dta_ref, exp_ref, state_ref, y_ref, i, 0)
        cb_idx = jnp.where(i < ncc, ncc - 1 - i, nct - 1 - (i - ncc))
        _ssd_chunk(xact_ref, dtt_ref, dta_ref, exp_ref, state_ref, y_ref, cb_idx, 1)
        return carry

    lax.fori_loop(0, nct, step, 0, unroll=3)

    for seg_ref, o_ref, n, base in ((zx_c, out_c, n_ctx, 0), (zx_l, out_l, n_lat, n_ctx)):
        if o_ref is None:
            continue
        def gate_rows(i, carry, seg_ref=seg_ref, o_ref=o_ref, base=base):
            r = pl.multiple_of(i * SSD_ROWS, SSD_ROWS)
            z = seg_ref[0, pl.ds(r, SSD_ROWS), 0:SSD_WIDTH]
            v = y_ref[pl.ds(base + r, SSD_ROWS), :] * _silu(z)
            o_ref[0, pl.ds(r, SSD_ROWS), :] = _rms(v, nw_ref[...]).astype(BF16)
            return carry

        lax.fori_loop(0, n // SSD_ROWS, gate_rows, 0, unroll=4)


def _ssd(zx_l, dt_l, zx_c, dt_c, dt_bias, a_log, d_skip, norm_w, need_ctx):
    bsz, n_lat, wzx = zx_l.shape
    n_ctx = zx_c.shape[1]
    n_tot = n_ctx + n_lat
    pad8 = lambda v: jnp.pad(v.reshape(1, -1), ((0, 0), (0, LANES - v.size)))
    per_b = lambda b: (b, 0, 0)
    const = lambda b: (0, 0)
    k_idx = np.arange(LANES)[None, :, None]
    d_idx = np.arange(2)[:, None, None]
    h_idx = (np.arange(SSD_WIDTH) // SSD_HEAD_DIM)[None, None, :]
    expand = jnp.asarray((k_idx < 24) & (k_idx % 8 == d_idx * SSD_HEADS + h_idx), BF16)
    params = [pad8(dt_bias), pad8(a_log),
              jnp.repeat(d_skip, SSD_HEAD_DIM).reshape(1, -1), norm_w.reshape(1, -1)]
    in_specs = [pl.BlockSpec((1, n_lat, wzx), per_b), pl.BlockSpec((1, n_lat, LANES), per_b),
                pl.BlockSpec((1, n_ctx, wzx), per_b), pl.BlockSpec((1, n_ctx, LANES), per_b)]
    in_specs += [pl.BlockSpec(p.shape, const) for p in params]
    in_specs += [pl.BlockSpec(expand.shape, lambda b: (0, 0, 0))]
    out_specs = [pl.BlockSpec((1, n_lat, SSD_WIDTH), per_b)]
    out_shape = [jax.ShapeDtypeStruct((bsz, n_lat, SSD_WIDTH), BF16)]
    if need_ctx:
        out_specs.append(pl.BlockSpec((1, n_ctx, SSD_WIDTH), per_b))
        out_shape.append(jax.ShapeDtypeStruct((bsz, n_ctx, SSD_WIDTH), BF16))
    scratch = [pltpu.VMEM((n_tot, SSD_XBC), F32),
               pltpu.VMEM((n_tot // SSD_CHUNK, LANES, SSD_CHUNK), F32),
               pltpu.VMEM((n_tot, LANES), F32),
               pltpu.VMEM((n_tot, SSD_WIDTH), F32),
               pltpu.VMEM((2, SSD_GROUPS * SSD_STATE, SSD_WIDTH), F32)]
    outs = pl.pallas_call(
        functools.partial(_ssd_kernel, n_ctx=n_ctx, n_lat=n_lat, need_ctx=need_ctx),
        grid=(bsz,),
        in_specs=in_specs,
        out_specs=out_specs,
        out_shape=out_shape,
        scratch_shapes=scratch,
        compiler_params=_cparams(1),
        name="ssd_mixer",
    )(zx_l, dt_l, zx_c, dt_c, *params, expand)
    return (outs[0], outs[1]) if need_ctx else (outs[0], None)


def _attn_kernel(*refs, n_kv, lam_init):
    q_ref = refs[0]
    k_refs = refs[1:1 + n_kv]
    v_refs = refs[1 + n_kv:1 + 2 * n_kv]
    lq1, lk1, lq2, lk2, sw_ref, o_ref = refs[1 + 2 * n_kv:-4]
    s_refs, e_refs = refs[-4:-2], refs[-2:]
    q = q_ref[0]
    hw = q.shape[1]
    lane = lax.broadcasted_iota(jnp.int32, q.shape, 1)
    lam = (jnp.exp(jnp.sum(lq1[...] * lk1[...], axis=-1, keepdims=True))
           - jnp.exp(jnp.sum(lq2[...] * lk2[...], axis=-1, keepdims=True)) + lam_init)
    offs = np.cumsum([0] + [k_ref.shape[2] for k_ref in k_refs])
    v_ext = [jnp.concatenate([v_ref[0], jnp.ones(v_ref.shape[1:], BF16)], axis=-1) for v_ref in v_refs]
    tq = q.shape[0]
    rows = max(ATTN_ROW_SPLIT, tq // 4)
    halves = [(lo, min(lo + rows, tq)) for lo in range(0, tq, rows)]
    for comp, s_ref in enumerate(s_refs):
        own = (lane >= DIFF_HEAD_DIM) if comp else (lane < DIFF_HEAD_DIM)
        qc = jnp.where(own, q, jnp.zeros_like(q))
        for i, k_ref in enumerate(k_refs):
            s_ref[:, offs[i]:offs[i + 1]] = jnp.dot(qc, k_ref[0], preferred_element_type=F32)
    outs = []
    for s_ref, e_ref in zip(s_refs, e_refs):
        parts = []
        for lo, hi in halves:
            s = s_ref[lo:hi, :]
            e_ref[lo:hi, :] = jnp.exp(s - jnp.max(s, axis=-1, keepdims=True)).astype(BF16)
            ov = None
            for i in range(n_kv):
                part = jnp.dot(e_ref[lo:hi, offs[i]:offs[i + 1]], v_ext[i], preferred_element_type=F32)
                ov = part if ov is None else ov + part
            parts.append(ov[:, 0:hw] * (1.0 / ov[:, hw:hw + 1]))
        outs.append(parts)
    for (lo, hi), o0, o1 in zip(halves, *outs):
        o = o0 - lam * o1
        o_ref[0, lo:hi, :] = (_rms(o, sw_ref[...]) * (1.0 - lam_init)).astype(BF16)


def _attention(q, ks, vs, lam_params, subln_w, lam_init):
    bsz, n, _ = q.shape
    tq = min(ATTN_Q_TILE, n)
    n_kv = len(ks)
    n_keys = sum(a.shape[2] for a in ks)
    hw = 2 * DIFF_HEAD_DIM
    in_specs = [pl.BlockSpec((1, tq, hw), lambda b, h, i: (b, i, h))]
    in_specs += [pl.BlockSpec((1, hw, a.shape[2]), lambda b, h, i: (b, h, 0)) for a in ks]
    in_specs += [pl.BlockSpec((1, a.shape[1], hw), lambda b, h, i: (b, 0, h)) for a in vs]
    in_specs += [pl.BlockSpec((1, DIFF_HEAD_DIM), lambda b, h, i: (0, 0))] * 4
    in_specs += [pl.BlockSpec((1, hw), lambda b, h, i: (0, 0))]
    return pl.pallas_call(
        functools.partial(_attn_kernel, n_kv=n_kv, lam_init=lam_init),
        grid=(bsz, DIFF_HEADS, n // tq),
        in_specs=in_specs,
        out_specs=pl.BlockSpec((1, tq, hw), lambda b, h, i: (b, i, h)),
        out_shape=jax.ShapeDtypeStruct((bsz, n, DIFF_WIDTH), BF16),
        scratch_shapes=[pltpu.VMEM((tq, n_keys), F32)] * 2 + [pltpu.VMEM((tq, n_keys), BF16)] * 2,
        compiler_params=_cparams(3),
        name="diff_attention",
    )(q, *ks, *vs, *[p.reshape(1, -1) for p in lam_params], subln_w.reshape(1, -1))


def _s5_matrices(lam_re, lam_im, log_step, b_re, b_im, c_re, c_im):
    t = S5_T
    step = jnp.exp(log_step.astype(F32))[..., None]
    lr, li = lam_re.astype(F32), lam_im.astype(F32)
    mag = jnp.exp(lr * step)
    a = (mag * jnp.cos(li * step)) + 1j * (mag * jnp.sin(li * step))
    lam = lr + 1j * li
    bb = ((a - 1.0) / lam)[..., None] * (b_re.astype(F32) + 1j * b_im.astype(F32))
    cc = c_re.astype(F32) + 1j * c_im.astype(F32)
    pw = [jnp.ones_like(a)]
    for _ in range(t):
        pw.append(pw[-1] * a)
    pw = jnp.stack(pw, axis=0)
    kern = jnp.real(jnp.einsum('dgcn,tdgn,dgnk->dgktc', cc, pw[:t], bb))
    gc = S5_GROUP
    fwd = kern[0].reshape(S5_GROUPS, gc, t * gc)
    bwd = kern[1][:, :, ::-1].reshape(S5_GROUPS, gc, t * gc)
    band = jnp.concatenate([bwd[..., :(t - 1) * gc], bwd[..., (t - 1) * gc:] + fwd[..., :gc],
                            fwd[..., gc:]], axis=-1)
    k_full = jnp.concatenate([band[..., (t - 1 - s) * gc:(2 * t - 1 - s) * gc] for s in range(t)],
                             axis=1)
    wf = pw[t - 1 - np.arange(t), 0][..., None] * bb[0][None]
    wb = pw[np.arange(t), 1][..., None] * bb[1][None]
    def rows_sc(w):
        return w.transpose(1, 0, 3, 2).reshape(S5_GROUPS, t * S5_GROUP, S5_STATE)
    w_state = jnp.concatenate([rows_sc(jnp.real(wf)), rows_sc(jnp.real(wb)),
                               rows_sc(jnp.imag(wf)), rows_sc(jnp.imag(wb))], axis=-1)
    cf = cc[0][None] * pw[1 + np.arange(t), 0][:, :, None, :]
    cbk = cc[1][None] * pw[t - np.arange(t), 1][:, :, None, :]
    def cols_tc(w):
        return w.transpose(1, 3, 0, 2).reshape(S5_GROUPS, S5_STATE, t * S5_GROUP)
    c_off = jnp.concatenate([cols_tc(jnp.real(cf)), cols_tc(jnp.real(cbk)),
                             cols_tc(-jnp.imag(cf)), cols_tc(-jnp.imag(cbk))], axis=1)
    at = pw[t]
    a_rows = jnp.stack([jnp.concatenate([jnp.real(at[0]), jnp.real(at[1])], axis=-1),
                        jnp.concatenate([jnp.imag(at[0]), jnp.imag(at[1])], axis=-1)], axis=1)
    a_chunk = jnp.pad(a_rows, ((0, 0), (0, SUBLANES - 2), (0, 0)))
    return k_full.astype(BF16), w_state.astype(BF16), c_off.astype(BF16), a_chunk.astype(F32)


def _s5_kernel(uc_ref, ul_ref, kf_ref, ws_ref, co_ref, a_ref, yc_ref, yl_ref, e_ref, h_ref):
    ncc, bsz, w = uc_ref.shape[1:]
    ncl = ul_ref.shape[1]
    nct = ncc + ncl
    uc = uc_ref[0].reshape(ncc * bsz, w)
    ul = ul_ref[0].reshape(ncl * bsz, w)
    e_ref[0:ncc] = jnp.dot(uc, ws_ref[0], preferred_element_type=F32).reshape(ncc, bsz, w)
    e_ref[ncc:nct] = jnp.dot(ul, ws_ref[0], preferred_element_type=F32).reshape(ncl, bsz, w)
    ar = a_ref[0, 0:1, :]
    ai = a_ref[0, 1:2, :]
    half = S5_STATE
    lane = lax.broadcasted_iota(jnp.int32, (bsz, LANES), 1)
    is_f = lane < half

    def step(i, carry):
        h_re, h_im = carry
        jf = i
        jb = jnp.where(i < ncc, ncc - 1 - i, nct - 1 - (i - ncc))
        h_ref[jf, :, 0:half] = h_re[:, 0:half]
        h_ref[jb, :, half:LANES] = h_re[:, half:]
        h_ref[jf, :, LANES:LANES + half] = h_im[:, 0:half]
        h_ref[jb, :, LANES + half:] = h_im[:, half:]
        e_re = jnp.where(is_f, e_ref[jf, :, 0:LANES], e_ref[jb, :, 0:LANES])
        e_im = jnp.where(is_f, e_ref[jf, :, LANES:], e_ref[jb, :, LANES:])
        return (ar * h_re - ai * h_im + e_re, ar * h_im + ai * h_re + e_im)

    zero = jnp.zeros((bsz, LANES), F32)
    lax.fori_loop(0, nct, step, (zero, zero))
    for u, y_ref, lo, nc in ((uc, yc_ref, 0, ncc), (ul, yl_ref, ncc, ncl)):
        hs = h_ref[lo:lo + nc].reshape(nc * bsz, w).astype(BF16)
        y = (jnp.dot(u, kf_ref[0], preferred_element_type=F32)
             + jnp.dot(hs, co_ref[0], preferred_element_type=F32))
        y_ref[0] = y.reshape(nc, bsz, w).astype(BF16)


def _s5_to_chunks(u):
    bsz, n, _ = u.shape
    u = u.reshape(bsz, n // S5_T, S5_T, S5_GROUPS, S5_GROUP).transpose(3, 1, 0, 2, 4)
    return u.reshape(S5_GROUPS, n // S5_T, bsz, S5_T * S5_GROUP)


def _s5_from_chunks(y, bsz):
    nchunk = y.shape[1]
    y = y.reshape(S5_GROUPS, nchunk, bsz, S5_T, S5_GROUP).transpose(2, 1, 3, 0, 4)
    return y.reshape(bsz, nchunk * S5_T, S5_WIDTH)


def _s5(u_l, u_c, mats):
    k_full, w_state, c_off, a_chunk = mats
    bsz = u_l.shape[0]
    uc, ul = _s5_to_chunks(u_c), _s5_to_chunks(u_l)
    ncc, ncl, w = uc.shape[1], ul.shape[1], ul.shape[3]
    nct = ncc + ncl
    blk = lambda g: (g, 0, 0, 0)
    mat = lambda g: (g, 0, 0)
    y_c, y_l = pl.pallas_call(
        _s5_kernel,
        grid=(S5_GROUPS,),
        in_specs=[pl.BlockSpec((1, ncc, bsz, w), blk), pl.BlockSpec((1, ncl, bsz, w), blk),
                  pl.BlockSpec((1, w, w), mat), pl.BlockSpec((1, w, w), mat),
                  pl.BlockSpec((1, w, w), mat), pl.BlockSpec((1, SUBLANES, LANES), mat)],
        out_specs=[pl.BlockSpec((1, ncc, bsz, w), blk), pl.BlockSpec((1, ncl, bsz, w), blk)],
        out_shape=[jax.ShapeDtypeStruct((S5_GROUPS, ncc, bsz, w), BF16),
                   jax.ShapeDtypeStruct((S5_GROUPS, ncl, bsz, w), BF16)],
        scratch_shapes=[pltpu.VMEM((nct, bsz, w), F32), pltpu.VMEM((nct, bsz, w), F32)],
        compiler_params=_cparams(1),
        name="s5_mixer",
    )(uc, ul, k_full, w_state, c_off, a_chunk)
    return _s5_from_chunks(y_l, bsz), _s5_from_chunks(y_c, bsz)


def _mix_ffn_kernel(*refs):
    rows = [refs[3 * j:3 * j + 3] for j in range(5)]
    (ga_ref, sh_ref, sc_ref, gf_ref, d_ref, gw_ref, gb_ref, wo_ref, nwm_ref, nw1_ref, nw2_ref,
     wg_ref, wu_ref, cw_ref, cb_ref, wd_ref, o_ref, act_ref) = refs[15:]
    i = pl.program_id(1)
    is_first = i == 0
    is_last = i == pl.num_programs(1) - 1
    tm = o_ref.shape[1]
    a, b, ys, u, x_ext = (jnp.concatenate([r[0] for r in trio], axis=0) for trio in rows)
    y = jax.nn.gelu(ys + d_ref[...] * u, approximate=True)
    s_gate = jnp.dot(y.astype(BF16), gw_ref[...], preferred_element_type=F32) + gb_ref[...]
    s = (y * jax.nn.sigmoid(s_gate)).astype(BF16)
    o1, o2 = SSD_WIDTH, SSD_WIDTH + DIFF_WIDTH
    mix = (jnp.dot(a, wo_ref[0:o1, :], preferred_element_type=F32)
           + jnp.dot(b, wo_ref[o1:o2, :], preferred_element_type=F32)
           + jnp.dot(s, wo_ref[o2:, :], preferred_element_type=F32))
    x1_ext = x_ext + ga_ref[0] * _rms(mix, nwm_ref[...])
    x1 = x1_ext[0:tm]
    h_ext = _rms(x1_ext, nw1_ref[...]) * (1.0 + sc_ref[0]) + sh_ref[0]
    row_ext = lax.broadcasted_iota(jnp.int32, (tm + 2 * HALO_ROWS, 1), 0)
    outside = (((row_ext >= tm) & (row_ext < tm + HALO_ROWS) & is_first)
               | ((row_ext >= tm + HALO_ROWS) & is_last))
    h_ext = jnp.where(outside, 0.0, h_ext).astype(BF16)
    h = h_ext[0:tm]
    row = lax.broadcasted_iota(jnp.int32, (tm, 1), 0)
    prev_row, next_row = tm + HALO_ROWS - 1, tm + HALO_ROWS
    for lo in range(0, wg_ref.shape[1], FFN_CHUNK):
        hi = lo + FFN_CHUNK
        g_ext = jnp.dot(h_ext, wg_ref[:, lo:hi], preferred_element_type=F32)
        g = g_ext[0:tm]
        g_prev = jnp.where(row == 0, g_ext[prev_row:prev_row + 1], pltpu.roll(g, 1, 0))
        g_next = jnp.where(row == tm - 1, g_ext[next_row:next_row + 1], pltpu.roll(g, tm - 1, 0))
        conv = (cw_ref[0:1, lo:hi] * g_prev + cw_ref[1:2, lo:hi] * g + cw_ref[2:3, lo:hi] * g_next
                + cb_ref[:, lo:hi])
        up = jnp.dot(h, wu_ref[:, lo:hi], preferred_element_type=F32)
        act_ref[:, lo:hi] = (_silu(conv) * up).astype(BF16)
    f = jnp.dot(act_ref[...], wd_ref[...], preferred_element_type=F32)
    o_ref[0] = x1 + gf_ref[0] * _rms(f, nw2_ref[...])


def _mix_ffn(a, b, ys, u, x, g_a, shift, scale, g_f, s5_d, glu_w, glu_b, w_out, norm_mix,
             norm_pre, norm_post, w_gate, w_up, conv_w, conv_b, w_down):
    bsz, n, d = x.shape
    tm = min(FFN_ROW_TILE, n)
    f = w_gate.shape[1]
    nbh = n // HALO_ROWS
    tbh = tm // HALO_ROWS
    per_b = lambda bi, i: (bi, 0, 0)
    const = lambda bi, i: (0, 0)
    once = dict(pipeline_mode=pl.Buffered(1))

    def row_specs(w):
        return [pl.BlockSpec((1, tm, w), lambda bi, i: (bi, i, 0)),
                pl.BlockSpec((1, HALO_ROWS, w), lambda bi, i: (bi, jnp.maximum(i * tbh - 1, 0), 0)),
                pl.BlockSpec((1, HALO_ROWS, w), lambda bi, i: (bi, jnp.minimum((i + 1) * tbh, nbh - 1), 0))]

    row_args, in_specs = [], []
    for arr in (a, b, ys, u, x):
        row_args += [arr] * 3
        in_specs += row_specs(arr.shape[2])
    in_specs += [pl.BlockSpec((1, 1, d), per_b)] * 4
    in_specs += [pl.BlockSpec((1, S5_WIDTH), const), pl.BlockSpec((S5_WIDTH, S5_WIDTH), const),
                 pl.BlockSpec((1, S5_WIDTH), const), pl.BlockSpec((d, d), const, **once),
                 pl.BlockSpec((1, d), const), pl.BlockSpec((1, d), const), pl.BlockSpec((1, d), const),
                 pl.BlockSpec((d, f), const, **once), pl.BlockSpec((d, f), const, **once),
                 pl.BlockSpec((FFN_CONV, f), const), pl.BlockSpec((1, f), const),
                 pl.BlockSpec((f, d), const, **once)]
    return pl.pallas_call(
        _mix_ffn_kernel,
        grid=(bsz, n // tm),
        in_specs=in_specs,
        out_specs=pl.BlockSpec((1, tm, d), lambda bi, i: (bi, i, 0)),
        out_shape=jax.ShapeDtypeStruct((bsz, n, d), F32),
        scratch_shapes=[pltpu.VMEM((tm, f), BF16)],
        compiler_params=_cparams(2),
        name="mix_ffn",
    )(*row_args, g_a, shift, scale, g_f, s5_d.reshape(1, -1), glu_w.astype(BF16), glu_b.reshape(1, -1),
      w_out.astype(BF16), norm_mix.reshape(1, -1), norm_pre.reshape(1, -1), norm_post.reshape(1, -1),
      w_gate.astype(BF16), w_up.astype(BF16), conv_w, conv_b.reshape(1, -1), w_down.astype(BF16))


def kernel(x, c, ctx, c_ctx, mod_w, mod_b, mix_norm_pre, mix_norm_post, ffn_norm_pre, ffn_norm_post, w_in, w_out, ssd_conv_w, ssd_conv_b, ssd_dt_bias, ssd_a_log, ssd_d, ssd_norm_w, diff_lam_q1, diff_lam_k1, diff_lam_q2, diff_lam_k2, diff_subln_w, s5_lam_re, s5_lam_im, s5_log_step, s5_b_re, s5_b_im, s5_c_re, s5_c_im, s5_d, s5_glu_w, s5_glu_b, ffn_w_gate, ffn_w_up, ffn_conv_w, ffn_conv_b, ffn_w_down):
    bsz, n_lat, d = x.shape
    n_layers = mod_w.shape[0]
    rope_tabs = _rope_tables(n_lat)
    cc = jnp.concatenate([c, jnp.broadcast_to(c_ctx[None, :], (SUBLANES, d))], axis=0)
    mod = _modulation(cc, mod_w, mod_b)

    for layer in range(n_layers):
        need_ctx = layer < n_layers - 1
        lam_init = 0.8 - 0.6 * math.exp(-0.3 * layer)
        mod_l = mod[layer, :bsz].reshape(bsz, N_MOD, 1, d)
        mod_c = jnp.broadcast_to(mod[layer, bsz].reshape(1, N_MOD, 1, d), (bsz, N_MOD, 1, d))
        sh_a, sc_a, g_a, sh_f, sc_f, g_f = (mod_l[:, i] for i in range(N_MOD))
        csh_a, csc_a, cg_a, csh_f, csc_f, cg_f = (mod_c[:, i] for i in range(N_MOD))

        w_merged = _merge_w_in(w_in[layer])
        in_p = (mix_norm_pre[layer], w_merged, ssd_conv_w[layer], ssd_conv_b[layer])
        zx_l, dt_l, q_l, k_l, v_l, u_l, u16_l = _in_proj(x, sh_a, sc_a, *in_p, rope_tabs)
        zx_c, dt_c, q_c, k_c, v_c, u_c, u16_c = _in_proj(ctx, csh_a, csc_a, *in_p, None)

        a_l, a_c = _ssd(zx_l, dt_l, zx_c, dt_c, ssd_dt_bias[layer], ssd_a_log[layer], ssd_d[layer],
                        ssd_norm_w[layer], need_ctx)

        lam_params = (diff_lam_q1[layer], diff_lam_k1[layer], diff_lam_q2[layer], diff_lam_k2[layer])
        b_l = _attention(q_l, (k_c, k_l), (v_c, v_l), lam_params, diff_subln_w[layer], lam_init)

        mats = _s5_matrices(s5_lam_re[layer], s5_lam_im[layer], s5_log_step[layer], s5_b_re[layer],
                            s5_b_im[layer], s5_c_re[layer], s5_c_im[layer])
        ys_l, ys_c = _s5(u16_l, u16_c, mats)

        mix_p = (s5_d[layer], s5_glu_w[layer], s5_glu_b[layer], w_out[layer], mix_norm_post[layer])
        ffn_p = (ffn_norm_pre[layer], ffn_norm_post[layer], ffn_w_gate[layer], ffn_w_up[layer],
                 ffn_conv_w[layer], ffn_conv_b[layer], ffn_w_down[layer])
        x = _mix_ffn(a_l, b_l, ys_l, u_l, x, g_a, sh_f, sc_f, g_f, *mix_p, *ffn_p)
        if need_ctx:
            b_c = _attention(q_c, (k_c,), (v_c,), lam_params, diff_subln_w[layer], lam_init)
            ctx = _mix_ffn(a_c, b_c, ys_c, u_c, ctx, cg_a, csh_f, csc_f, cg_f, *mix_p, *ffn_p)
    return x
```

```python
import functools
import math

import numpy as np
import jax
import jax.numpy as jnp
from jax import lax
from jax.experimental import pallas as pl
from jax.experimental.pallas import tpu as pltpu

F32 = jnp.float32
BF16 = jnp.bfloat16
HI = lax.Precision.HIGHEST

D_MODEL = 1024
N_LAYERS = 2
GRID_W = 64
EPS = 1e-6
N_MOD = 6
SSD_HEADS = 4
SSD_HEAD_DIM = 64
SSD_WIDTH = 256
SSD_GROUPS = 2
SSD_STATE = 64
SSD_CONV = 5
SSD_CHUNK = 128
SSD_ROWS = 64
CONV_ROWS = 64
SSD_XBC = 512
SSD_IN = 776
DIFF_HEADS = 4
DIFF_HEAD_DIM = 64
DIFF_WIDTH = 512
DIFF_IN = 1536
ROPE_THETA = 10000.0
ATTN_Q_TILE = 1024
ATTN_ROW_SPLIT = 128
S5_GROUP = 16
S5_WIDTH = 256
S5_GROUPS = 16
S5_STATE = 64
S5_T = 16
FFN_DIM = 2816
FFN_CHUNK = 256
FFN_ROW_TILE = 1024
HALO_ROWS = 16
FFN_CONV = 3

LANES = 128
SUBLANES = 8
W_IN_PAD = 2688
VMEM_LIMIT = 56 * 1024 * 1024


def _cparams(n_axes):
    return pltpu.CompilerParams(dimension_semantics=("parallel",) * n_axes,
                                vmem_limit_bytes=VMEM_LIMIT)


def _rms(x, w):
    return x * lax.rsqrt(jnp.mean(x * x, axis=-1, keepdims=True) + EPS) * w


def _silu(x):
    return x * jax.nn.sigmoid(x)


def _row_tile(n):
    return min(512, n)


def _mod_kernel(c_ref, w_ref, b_ref, o_ref):
    a = _silu(c_ref[...])
    o_ref[0] = jnp.dot(a, w_ref[0], preferred_element_type=F32, precision=HI) + b_ref[0]


def _modulation(cc, mod_w, mod_b):
    nl, d, n = mod_w.shape
    r = cc.shape[0]
    tn = 512
    return pl.pallas_call(
        _mod_kernel,
        grid=(nl, n // tn),
        in_specs=[pl.BlockSpec((r, d), lambda l, j: (0, 0)),
                  pl.BlockSpec((1, d, tn), lambda l, j: (l, 0, j)),
                  pl.BlockSpec((1, 1, tn), lambda l, j: (l, 0, j))],
        out_specs=pl.BlockSpec((1, r, tn), lambda l, j: (l, 0, j)),
        out_shape=jax.ShapeDtypeStruct((nl, r, n), F32),
        compiler_params=_cparams(2),
        name="modulation",
    )(cc, mod_w, mod_b.reshape(nl, 1, n))


def _in_proj_kernel(*refs, rope):
    (x_ref, xp_ref, xn_ref, sh_ref, sc_ref, nw_ref, w_ref, cw_ref, cb_ref) = refs[:9]
    cos_ref, sin_ref = refs[9:11] if rope else (None, None)
    zx_ref, dt_ref, q_ref, k_ref, v_ref, u_ref, u16_ref = refs[-7:]
    i = pl.program_id(1)
    tm = x_ref.shape[1]
    x_ext = jnp.concatenate([xp_ref[0], x_ref[0], xn_ref[0]], axis=0)
    h_ext = _rms(x_ext, nw_ref[...]) * (1.0 + sc_ref[0]) + sh_ref[0]
    row_ext = lax.broadcasted_iota(jnp.int32, (tm + 2 * HALO_ROWS, 1), 0)
    outside = (((row_ext < HALO_ROWS) & (i == 0))
               | ((row_ext >= tm + HALO_ROWS) & (i == pl.num_programs(1) - 1)))
    h_ext = jnp.where(outside, 0.0, h_ext).astype(BF16)
    hb = h_ext[HALO_ROWS:HALO_ROWS + tm]

    def mm(lo, hi):
        return jnp.dot(hb, w_ref[:, lo:hi], preferred_element_type=F32)

    xbc = jnp.dot(h_ext, w_ref[:, SSD_WIDTH:SSD_WIDTH + SSD_XBC], preferred_element_type=F32)
    halo = (SSD_CONV - 1) // 2
    win = CONV_ROWS + 2 * SUBLANES
    for r in range(0, tm, CONV_ROWS):
        lo = HALO_ROWS - SUBLANES + r
        window = xbc[lo:lo + win]
        acc = jnp.broadcast_to(cb_ref[...], (CONV_ROWS, SSD_XBC))
        for k in range(SSD_CONV):
            shift = (halo - k) % win
            tap = window if shift == 0 else pltpu.roll(window, shift, 0)
            acc = acc + cw_ref[k:k + 1, :] * tap[SUBLANES:SUBLANES + CONV_ROWS]
        zx_ref[0, r:r + CONV_ROWS, SSD_WIDTH:] = _silu(acc)

    def rot(p):
        lane = lax.broadcasted_iota(jnp.int32, p.shape, 1)
        partner = jnp.where(lane % 32 < 16, pltpu.roll(p, LANES - 16, 1), pltpu.roll(p, 16, 1))
        return p * cos_ref[...] + partner * sin_ref[...]

    o_zx = SSD_WIDTH + SSD_XBC
    zx_ref[0, :, 0:SSD_WIDTH] = mm(0, SSD_WIDTH)
    q_all = mm(o_zx, o_zx + DIFF_WIDTH)
    k_all = mm(o_zx + DIFF_WIDTH, o_zx + 2 * DIFF_WIDTH)
    for j in range(DIFF_WIDTH // LANES):
        q = q_all[:, j * LANES:(j + 1) * LANES] * (DIFF_HEAD_DIM ** -0.5)
        k = k_all[:, j * LANES:(j + 1) * LANES]
        if rope:
            q, k = rot(q), rot(k)
        q_ref[0, :, j * LANES:(j + 1) * LANES] = q.astype(BF16)
        k_ref[0, j * LANES:(j + 1) * LANES, :] = k.T.astype(BF16)
    o_v = o_zx + 2 * DIFF_WIDTH
    v_ref[0] = mm(o_v, o_v + DIFF_WIDTH).astype(BF16)
    o_u = o_v + DIFF_WIDTH
    u_dt = mm(o_u, W_IN_PAD)
    u = u_dt[:, 0:S5_WIDTH]
    u_ref[0] = u
    u16_ref[0] = u.astype(BF16)
    dt_ref[0] = u_dt[:, S5_WIDTH:]


def _in_proj(x, shift, scale, norm_w, w_merged, conv_w, conv_b, rope_tabs):
    bsz, n, d = x.shape
    tm = _row_tile(n)
    nbh = n // HALO_ROWS
    tbh = tm // HALO_ROWS
    rope = rope_tabs is not None
    row = lambda b, i: (b, i, 0)
    per_b = lambda b, i: (b, 0, 0)
    const = lambda b, i: (0, 0)
    in_specs = [pl.BlockSpec((1, tm, d), row),
                pl.BlockSpec((1, HALO_ROWS, d), lambda b, i: (b, jnp.maximum(i * tbh - 1, 0), 0)),
                pl.BlockSpec((1, HALO_ROWS, d), lambda b, i: (b, jnp.minimum((i + 1) * tbh, nbh - 1), 0)),
                pl.BlockSpec((1, 1, d), per_b),
                pl.BlockSpec((1, 1, d), per_b),
                pl.BlockSpec((1, d), const),
                pl.BlockSpec((d, W_IN_PAD), const),
                pl.BlockSpec((SSD_CONV, SSD_XBC), const),
                pl.BlockSpec((1, SSD_XBC), const)]
    args = [x, x, x, shift, scale, norm_w.reshape(1, d), w_merged, conv_w, conv_b.reshape(1, -1)]
    if rope:
        in_specs += [pl.BlockSpec((tm, LANES), lambda b, i: (i, 0))] * 2
        args += list(rope_tabs)
    widths = (SSD_WIDTH + SSD_XBC, LANES, DIFF_WIDTH, DIFF_WIDTH, DIFF_WIDTH, S5_WIDTH, S5_WIDTH)
    dtypes = (F32, F32, BF16, BF16, BF16, F32, BF16)
    out_specs = [pl.BlockSpec((1, tm, w), row) for w in widths]
    out_shape = [jax.ShapeDtypeStruct((bsz, n, w), t) for w, t in zip(widths, dtypes)]
    out_specs[3] = pl.BlockSpec((1, DIFF_WIDTH, tm), lambda b, i: (b, 0, i))
    out_shape[3] = jax.ShapeDtypeStruct((bsz, DIFF_WIDTH, n), BF16)
    return pl.pallas_call(
        functools.partial(_in_proj_kernel, rope=rope),
        grid=(bsz, n // tm),
        in_specs=in_specs,
        out_specs=out_specs,
        out_shape=out_shape,
        compiler_params=_cparams(2),
        name="in_proj_rope" if rope else "in_proj",
    )(*args)


def _merge_w_in(w):
    o1, o2 = SSD_IN, SSD_IN + DIFF_IN
    parts = [w[:, :SSD_WIDTH + SSD_XBC], w[:, o1:o2], w[:, o2:], w[:, SSD_WIDTH + SSD_XBC:o1]]
    m = jnp.concatenate(parts, axis=1)
    return jnp.pad(m, ((0, 0), (0, W_IN_PAD - m.shape[1]))).astype(BF16)


def _rope_tables(n):
    t = np.arange(n)
    pos = np.stack([t // GRID_W, t % GRID_W], axis=1).astype(np.float32)
    quarter = DIFF_HEAD_DIM // 4
    inv_freq = (np.float32(ROPE_THETA) ** (-np.arange(quarter, dtype=np.float32) / np.float32(quarter)))
    lane = np.arange(LANES)
    axis = (lane % DIFF_HEAD_DIM) // (DIFF_HEAD_DIM // 2)
    freq = lane % quarter
    ang = (pos[:, axis] * inv_freq.astype(np.float32)[freq][None, :]).astype(np.float32)
    sign = np.where(lane % (2 * quarter) < quarter, -1.0, 1.0).astype(np.float32)
    return jnp.asarray(np.cos(ang), F32), jnp.asarray(np.sin(ang) * sign[None, :], F32)


def _cumsum_rows(x, reverse):
    n = x.shape[0]
    row = lax.broadcasted_iota(jnp.int32, x.shape, 0)
    k = 1
    while k < n:
        if reverse:
            x = x + jnp.where(row < n - k, pltpu.roll(x, n - k, 0), 0.0)
        else:
            x = x + jnp.where(row >= k, pltpu.roll(x, k, 0), 0.0)
        k *= 2
    return x


def _ssd_chunk(xact_ref, dtt_ref, dta_ref, exp_ref, state_ref, y_ref, c, direction):
    q = SSD_CHUNK
    nh = SSD_HEADS
    r0 = pl.multiple_of(c * q, q)
    xa = xact_ref[pl.ds(r0, q), :]
    dta = dta_ref[pl.ds(r0, q), :]
    dt_t = dtt_ref[c]
    rows = lax.broadcasted_iota(jnp.int32, (q, q), 0)
    cols = lax.broadcasted_iota(jnp.int32, (q, q), 1)
    lane = cols
    mask = (rows >= cols) if direction == 0 else (rows <= cols)
    acs = _cumsum_rows(dta, reverse=direction == 1)
    acs_t = acs.T
    hi = acs.astype(BF16).astype(F32)
    r1 = acs - hi
    mid = r1.astype(BF16).astype(F32)
    lo = (r1 - mid).astype(BF16).astype(F32)
    parts = hi + pltpu.roll(mid, 8, 1) + pltpu.roll(lo, 16, 1)
    neg = -pltpu.roll(parts, 24, 1)
    lhs = jnp.where(lane < 24, parts, jnp.where(lane < 48, 1.0, 0.0)).astype(BF16)
    rhs = []
    for h in range(nh):
        j = direction * nh + h
        is_one = (lane == j) | (lane == 8 + j) | (lane == 16 + j)
        is_neg = (lane == 24 + j) | (lane == 32 + j) | (lane == 40 + j)
        rhs.append(jnp.where(is_one, 1.0, jnp.where(is_neg, neg, 0.0)).astype(BF16))
    diff = lax.dot_general(lhs, jnp.concatenate(rhs, axis=0), (((1,), (1,)), ((), ())),
                           preferred_element_type=F32)
    mask4 = jnp.concatenate([mask] * nh, axis=1)
    decay = jnp.exp(jnp.where(mask4, diff, -jnp.inf))
    acs_x = jnp.dot(parts.astype(BF16), exp_ref[direction], preferred_element_type=F32)
    last = q - 1 if direction == 0 else 0
    total_x = acs_x[last:last + 1, :]

    xb = xa[:, 0:SSD_WIDTH].astype(BF16)
    bm = xa[:, SSD_WIDTH:SSD_WIDTH + LANES]
    cm = xa[:, SSD_WIDTH + LANES:]
    bm_b = bm.astype(BF16)
    bm_t = bm.T
    state = state_ref[direction]
    y_off = jnp.dot(cm.astype(BF16), state.astype(BF16), preferred_element_type=F32) * jnp.exp(acs_x)
    row_grp = lax.broadcasted_iota(jnp.int32, (LANES, SSD_WIDTH), 0) // SSD_STATE
    lane_head = lax.broadcasted_iota(jnp.int32, (LANES, SSD_WIDTH), 1) // SSD_HEAD_DIM
    upd = jnp.zeros((LANES, SSD_WIDTH), F32)
    y_diag = []
    for g in range(SSD_GROUPS):
        cg = jnp.where(lane // SSD_STATE == g, cm, 0.0).astype(BF16)
        cb = lax.dot_general(cg, bm_b, (((1,), (1,)), ((), ())), preferred_element_type=F32)
        xg = xb[:, g * LANES:(g + 1) * LANES]
        pair = []
        for h in range(g * 2, g * 2 + 2):
            j = direction * nh + h
            row_dt = dt_t[j:j + 1, :]
            scores = (cb * decay[:, h * q:(h + 1) * q] * row_dt).astype(BF16)
            pair.append(jnp.dot(scores, xg, preferred_element_type=F32))
            w_row = row_dt * jnp.exp(acs_t[j:j + 1, last:last + 1] - acs_t[j:j + 1, :])
            upd_h = jnp.dot((bm_t * w_row).astype(BF16), xb, preferred_element_type=F32)
            upd = upd + jnp.where((row_grp == g) & (lane_head == h), upd_h, 0.0)
        y_diag.append(jnp.where(lane < SSD_HEAD_DIM, pair[0], pair[1]))
    y_ref[pl.ds(r0, q), :] += jnp.concatenate(y_diag, axis=-1) + y_off
    state_ref[direction] = jnp.exp(total_x) * state + upd


def _ssd_kernel(*refs, n_ctx, n_lat, need_ctx):
    (zx_l, dt_l, zx_c, dt_c, dtb_ref, alog_ref, d_ref, nw_ref, exp_ref) = refs[:9]
    outs = refs[9:11] if need_ctx else (refs[9], None)
    xact_ref, dtt_ref, dta_ref, y_ref, state_ref = refs[-5:]
    out_l, out_c = outs

    for seg_ref, n, base in ((zx_c, n_ctx, 0), (zx_l, n_lat, n_ctx)):
        def copy_rows(i, carry, seg_ref=seg_ref, base=base):
            r = pl.multiple_of(i * SSD_CHUNK, SSD_CHUNK)
            act = seg_ref[0, pl.ds(r, SSD_CHUNK), SSD_WIDTH:]
            xact_ref[pl.ds(base + r, SSD_CHUNK), :] = act
            y_ref[pl.ds(base + r, SSD_CHUNK), :] = act[:, 0:SSD_WIDTH] * d_ref[...]
            return carry

        lax.fori_loop(0, n // SSD_CHUNK, copy_rows, 0)
    a_neg = -jnp.exp(alog_ref[...])
    head_lane = lax.broadcasted_iota(jnp.int32, (SSD_CHUNK, LANES), 1) < 2 * SSD_HEADS
    for seg_ref, n, base in ((dt_c, n_ctx, 0), (dt_l, n_lat, n_ctx)):
        for t0 in range(0, n, SSD_CHUNK):
            dt = jax.nn.softplus(seg_ref[0, t0:t0 + SSD_CHUNK, :] + dtb_ref[...])
            dtt_ref[(base + t0) // SSD_CHUNK] = dt.T
            dta_ref[base + t0:base + t0 + SSD_CHUNK, :] = jnp.where(head_lane, dt * a_neg, 0.0)
    n_tot = n_ctx + n_lat
    state_ref[...] = jnp.zeros(state_ref.shape, F32)

    ncc, nct = n_ctx // SSD_CHUNK, n_tot // SSD_CHUNK

    def step(i, carry):
        _ssd_chunk(xact_ref, dtt_ref, dta_ref, exp_ref, state_ref, y_ref, i, 0)
        cb_idx = jnp.where(i < ncc, ncc - 1 - i, nct - 1 - (i - ncc))
        _ssd_chunk(xact_ref, dtt_ref, dta_ref, exp_ref, state_ref, y_ref, cb_idx, 1)
        return carry

    lax.fori_loop(0, nct, step, 0, unroll=3)

    for seg_ref, o_ref, n, base in ((zx_c, out_c, n_ctx, 0), (zx_l, out_l, n_lat, n_ctx)):
        if o_ref is None:
            continue
        def gate_rows(i, carry, seg_ref=seg_ref, o_ref=o_ref, base=base):
            r = pl.multiple_of(i * SSD_ROWS, SSD_ROWS)
            z = seg_ref[0, pl.ds(r, SSD_ROWS), 0:SSD_WIDTH]
            v = y_ref[pl.ds(base + r, SSD_ROWS), :] * _silu(z)
            o_ref[0, pl.ds(r, SSD_ROWS), :] = _rms(v, nw_ref[...]).astype(BF16)
            return carry

        lax.fori_loop(0, n // SSD_ROWS, gate_rows, 0, unroll=4)


def _ssd(zx_l, dt_l, zx_c, dt_c, dt_bias, a_log, d_skip, norm_w, need_ctx):
    bsz, n_lat, wzx = zx_l.shape
    n_ctx = zx_c.shape[1]
    n_tot = n_ctx + n_lat
    pad8 = lambda v: jnp.pad(v.reshape(1, -1), ((0, 0), (0, LANES - v.size)))
    per_b = lambda b: (b, 0, 0)
    const = lambda b: (0, 0)
    k_idx = np.arange(LANES)[None, :, None]
    d_idx = np.arange(2)[:, None, None]
    h_idx = (np.arange(SSD_WIDTH) // SSD_HEAD_DIM)[None, None, :]
    expand = jnp.asarray((k_idx < 24) & (k_idx % 8 == d_idx * SSD_HEADS + h_idx), BF16)
    params = [pad8(dt_bias), pad8(a_log),
              jnp.repeat(d_skip, SSD_HEAD_DIM).reshape(1, -1), norm_w.reshape(1, -1)]
    in_specs = [pl.BlockSpec((1, n_lat, wzx), per_b), pl.BlockSpec((1, n_lat, LANES), per_b),
                pl.BlockSpec((1, n_ctx, wzx), per_b), pl.BlockSpec((1, n_ctx, LANES), per_b)]
    in_specs += [pl.BlockSpec(p.shape, const) for p in params]
    in_specs += [pl.BlockSpec(expand.shape, lambda b: (0, 0, 0))]
    out_specs = [pl.BlockSpec((1, n_lat, SSD_WIDTH), per_b)]
    out_shape = [jax.ShapeDtypeStruct((bsz, n_lat, SSD_WIDTH), BF16)]
    if need_ctx:
        out_specs.append(pl.BlockSpec((1, n_ctx, SSD_WIDTH), per_b))
        out_shape.append(jax.ShapeDtypeStruct((bsz, n_ctx, SSD_WIDTH), BF16))
    scratch = [pltpu.VMEM((n_tot, SSD_XBC), F32),
               pltpu.VMEM((n_tot // SSD_CHUNK, LANES, SSD_CHUNK), F32),
               pltpu.VMEM((n_tot, LANES), F32),
               pltpu.VMEM((n_tot, SSD_WIDTH), F32),
               pltpu.VMEM((2, SSD_GROUPS * SSD_STATE, SSD_WIDTH), F32)]
    outs = pl.pallas_call(
        functools.partial(_ssd_kernel, n_ctx=n_ctx, n_lat=n_lat, need_ctx=need_ctx),
        grid=(bsz,),
        in_specs=in_specs,
        out_specs=out_specs,
        out_shape=out_shape,
        scratch_shapes=scratch,
        compiler_params=_cparams(1),
        name="ssd_mixer",
    )(zx_l, dt_l, zx_c, dt_c, *params, expand)
    return (outs[0], outs[1]) if need_ctx else (outs[0], None)


def _attn_kernel(*refs, n_kv, lam_init):
    q_ref = refs[0]
    k_refs = refs[1:1 + n_kv]
    v_refs = refs[1 + n_kv:1 + 2 * n_kv]
    lq1, lk1, lq2, lk2, sw_ref, o_ref = refs[1 + 2 * n_kv:-4]
    s_refs, e_refs = refs[-4:-2], refs[-2:]
    q = q_ref[0]
    hw = q.shape[1]
    lane = lax.broadcasted_iota(jnp.int32, q.shape, 1)
    lam = (jnp.exp(jnp.sum(lq1[...] * lk1[...], axis=-1, keepdims=True))
           - jnp.exp(jnp.sum(lq2[...] * lk2[...], axis=-1, keepdims=True)) + lam_init)
    offs = np.cumsum([0] + [k_ref.shape[2] for k_ref in k_refs])
    v_ext = [jnp.concatenate([v_ref[0], jnp.ones(v_ref.shape[1:], BF16)], axis=-1) for v_ref in v_refs]
    tq = q.shape[0]
    rows = max(ATTN_ROW_SPLIT, tq // 4)
    halves = [(lo, min(lo + rows, tq)) for lo in range(0, tq, rows)]
    for comp, s_ref in enumerate(s_refs):
        own = (lane >= DIFF_HEAD_DIM) if comp else (lane < DIFF_HEAD_DIM)
        qc = jnp.where(own, q, jnp.zeros_like(q))
        for i, k_ref in enumerate(k_refs):
            s_ref[:, offs[i]:offs[i + 1]] = jnp.dot(qc, k_ref[0], preferred_element_type=F32)
    outs = []
    for s_ref, e_ref in zip(s_refs, e_refs):
        parts = []
        for lo, hi in halves:
            s = s_ref[lo:hi, :]
            e_ref[lo:hi, :] = jnp.exp(s - jnp.max(s, axis=-1, keepdims=True)).astype(BF16)
            ov = None
            for i in range(n_kv):
                part = jnp.dot(e_ref[lo:hi, offs[i]:offs[i + 1]], v_ext[i], preferred_element_type=F32)
                ov = part if ov is None else ov + part
            parts.append(ov[:, 0:hw] * (1.0 / ov[:, hw:hw + 1]))
        outs.append(parts)
    for (lo, hi), o0, o1 in zip(halves, *outs):
        o = o0 - lam * o1
        o_ref[0, lo:hi, :] = (_rms(o, sw_ref[...]) * (1.0 - lam_init)).astype(BF16)


def _attention(q, ks, vs, lam_params, subln_w, lam_init):
    bsz, n, _ = q.shape
    tq = min(ATTN_Q_TILE, n)
    n_kv = len(ks)
    n_keys = sum(a.shape[2] for a in ks)
    hw = 2 * DIFF_HEAD_DIM
    in_specs = [pl.BlockSpec((1, tq, hw), lambda b, h, i: (b, i, h))]
    in_specs += [pl.BlockSpec((1, hw, a.shape[2]), lambda b, h, i: (b, h, 0)) for a in ks]
    in_specs += [pl.BlockSpec((1, a.shape[1], hw), lambda b, h, i: (b, 0, h)) for a in vs]
    in_specs += [pl.BlockSpec((1, DIFF_HEAD_DIM), lambda b, h, i: (0, 0))] * 4
    in_specs += [pl.BlockSpec((1, hw), lambda b, h, i: (0, 0))]
    return pl.pallas_call(
        functools.partial(_attn_kernel, n_kv=n_kv, lam_init=lam_init),
        grid=(bsz, DIFF_HEADS, n // tq),
        in_specs=in_specs,
        out_specs=pl.BlockSpec((1, tq, hw), lambda b, h, i: (b, i, h)),
        out_shape=jax.ShapeDtypeStruct((bsz, n, DIFF_WIDTH), BF16),
        scratch_shapes=[pltpu.VMEM((tq, n_keys), F32)] * 2 + [pltpu.VMEM((tq, n_keys), BF16)] * 2,
        compiler_params=_cparams(3),
        name="diff_attention",
    )(q, *ks, *vs, *[p.reshape(1, -1) for p in lam_params], subln_w.reshape(1, -1))


def _s5_matrices(lam_re, lam_im, log_step, b_re, b_im, c_re, c_im):
    t = S5_T
    step = jnp.exp(log_step.astype(F32))[..., None]
    lr, li = lam_re.astype(F32), lam_im.astype(F32)
    mag = jnp.exp(lr * step)
    a = (mag * jnp.cos(li * step)) + 1j * (mag * jnp.sin(li * step))
    lam = lr + 1j * li
    bb = ((a - 1.0) / lam)[..., None] * (b_re.astype(F32) + 1j * b_im.astype(F32))
    cc = c_re.astype(F32) + 1j * c_im.astype(F32)
    pw = [jnp.ones_like(a)]
    for _ in range(t):
        pw.append(pw[-1] * a)
    pw = jnp.stack(pw, axis=0)
    kern = jnp.real(jnp.einsum('dgcn,tdgn,dgnk->dgktc', cc, pw[:t], bb))
    gc = S5_GROUP
    fwd = kern[0].reshape(S5_GROUPS, gc, t * gc)
    bwd = kern[1][:, :, ::-1].reshape(S5_GROUPS, gc, t * gc)
    band = jnp.concatenate([bwd[..., :(t - 1) * gc], bwd[..., (t - 1) * gc:] + fwd[..., :gc],
                            fwd[..., gc:]], axis=-1)
    k_full = jnp.concatenate([band[..., (t - 1 - s) * gc:(2 * t - 1 - s) * gc] for s in range(t)],
                             axis=1)
    wf = pw[t - 1 - np.arange(t), 0][..., None] * bb[0][None]
    wb = pw[np.arange(t), 1][..., None] * bb[1][None]
    def rows_sc(w):
        return w.transpose(1, 0, 3, 2).reshape(S5_GROUPS, t * S5_GROUP, S5_STATE)
    w_state = jnp.concatenate([rows_sc(jnp.real(wf)), rows_sc(jnp.real(wb)),
                               rows_sc(jnp.imag(wf)), rows_sc(jnp.imag(wb))], axis=-1)
    cf = cc[0][None] * pw[1 + np.arange(t), 0][:, :, None, :]
    cbk = cc[1][None] * pw[t - np.arange(t), 1][:, :, None, :]
    def cols_tc(w):
        return w.transpose(1, 3, 0, 2).reshape(S5_GROUPS, S5_STATE, t * S5_GROUP)
    c_off = jnp.concatenate([cols_tc(jnp.real(cf)), cols_tc(jnp.real(cbk)),
                             cols_tc(-jnp.imag(cf)), cols_tc(-jnp.imag(cbk))], axis=1)
    at = pw[t]
    a_rows = jnp.stack([jnp.concatenate([jnp.real(at[0]), jnp.real(at[1])], axis=-1),
                        jnp.concatenate([jnp.imag(at[0]), jnp.imag(at[1])], axis=-1)], axis=1)
    a_chunk = jnp.pad(a_rows, ((0, 0), (0, SUBLANES - 2), (0, 0)))
    return k_full.astype(BF16), w_state.astype(BF16), c_off.astype(BF16), a_chunk.astype(F32)


def _s5_kernel(uc_ref, ul_ref, kf_ref, ws_ref, co_ref, a_ref, yc_ref, yl_ref, e_ref, h_ref):
    ncc, bsz, w = uc_ref.shape[1:]
    ncl = ul_ref.shape[1]
    nct = ncc + ncl
    uc = uc_ref[0].reshape(ncc * bsz, w)
    ul = ul_ref[0].reshape(ncl * bsz, w)
    e_ref[0:ncc] = jnp.dot(uc, ws_ref[0], preferred_element_type=F32).reshape(ncc, bsz, w)
    e_ref[ncc:nct] = jnp.dot(ul, ws_ref[0], preferred_element_type=F32).reshape(ncl, bsz, w)
    ar = a_ref[0, 0:1, :]
    ai = a_ref[0, 1:2, :]
    half = S5_STATE
    lane = lax.broadcasted_iota(jnp.int32, (bsz, LANES), 1)
    is_f = lane < half

    def step(i, carry):
        h_re, h_im = carry
        jf = i
        jb = jnp.where(i < ncc, ncc - 1 - i, nct - 1 - (i - ncc))
        h_ref[jf, :, 0:half] = h_re[:, 0:half]
        h_ref[jb, :, half:LANES] = h_re[:, half:]
        h_ref[jf, :, LANES:LANES + half] = h_im[:, 0:half]
        h_ref[jb, :, LANES + half:] = h_im[:, half:]
        e_re = jnp.where(is_f, e_ref[jf, :, 0:LANES], e_ref[jb, :, 0:LANES])
        e_im = jnp.where(is_f, e_ref[jf, :, LANES:], e_ref[jb, :, LANES:])
        return (ar * h_re - ai * h_im + e_re, ar * h_im + ai * h_re + e_im)

    zero = jnp.zeros((bsz, LANES), F32)
    lax.fori_loop(0, nct, step, (zero, zero))
    for u, y_ref, lo, nc in ((uc, yc_ref, 0, ncc), (ul, yl_ref, ncc, ncl)):
        hs = h_ref[lo:lo + nc].reshape(nc * bsz, w).astype(BF16)
        y = (jnp.dot(u, kf_ref[0], preferred_element_type=F32)
             + jnp.dot(hs, co_ref[0], preferred_element_type=F32))
        y_ref[0] = y.reshape(nc, bsz, w).astype(BF16)


def _s5_to_chunks(u):
    bsz, n, _ = u.shape
    u = u.reshape(bsz, n // S5_T, S5_T, S5_GROUPS, S5_GROUP).transpose(3, 1, 0, 2, 4)
    return u.reshape(S5_GROUPS, n // S5_T, bsz, S5_T * S5_GROUP)


def _s5_from_chunks(y, bsz):
    nchunk = y.shape[1]
    y = y.reshape(S5_GROUPS, nchunk, bsz, S5_T, S5_GROUP).transpose(2, 1, 3, 0, 4)
    return y.reshape(bsz, nchunk * S5_T, S5_WIDTH)


def _s5(u_l, u_c, mats):
    k_full, w_state, c_off, a_chunk = mats
    bsz = u_l.shape[0]
    uc, ul = _s5_to_chunks(u_c), _s5_to_chunks(u_l)
    ncc, ncl, w = uc.shape[1], ul.shape[1], ul.shape[3]
    nct = ncc + ncl
    blk = lambda g: (g, 0, 0, 0)
    mat = lambda g: (g, 0, 0)
    y_c, y_l = pl.pallas_call(
        _s5_kernel,
        grid=(S5_GROUPS,),
        in_specs=[pl.BlockSpec((1, ncc, bsz, w), blk), pl.BlockSpec((1, ncl, bsz, w), blk),
                  pl.BlockSpec((1, w, w), mat), pl.BlockSpec((1, w, w), mat),
                  pl.BlockSpec((1, w, w), mat), pl.BlockSpec((1, SUBLANES, LANES), mat)],
        out_specs=[pl.BlockSpec((1, ncc, bsz, w), blk), pl.BlockSpec((1, ncl, bsz, w), blk)],
        out_shape=[jax.ShapeDtypeStruct((S5_GROUPS, ncc, bsz, w), BF16),
                   jax.ShapeDtypeStruct((S5_GROUPS, ncl, bsz, w), BF16)],
        scratch_shapes=[pltpu.VMEM((nct, bsz, w), F32), pltpu.VMEM((nct, bsz, w), F32)],
        compiler_params=_cparams(1),
        name="s5_mixer",
    )(uc, ul, k_full, w_state, c_off, a_chunk)
    return _s5_from_chunks(y_l, bsz), _s5_from_chunks(y_c, bsz)


def _mix_ffn_kernel(*refs):
    rows = [refs[3 * j:3 * j + 3] for j in range(5)]
    (ga_ref, sh_ref, sc_ref, gf_ref, d_ref, gw_ref, gb_ref, wo_ref, nwm_ref, nw1_ref, nw2_ref,
     wg_ref, wu_ref, cw_ref, cb_ref, wd_ref, o_ref, act_ref) = refs[15:]
    i = pl.program_id(1)
    is_first = i == 0
    is_last = i == pl.num_programs(1) - 1
    tm = o_ref.shape[1]
    a, b, ys, u, x_ext = (jnp.concatenate([r[0] for r in trio], axis=0) for trio in rows)
    y = jax.nn.gelu(ys + d_ref[...] * u, approximate=True)
    s_gate = jnp.dot(y.astype(BF16), gw_ref[...], preferred_element_type=F32) + gb_ref[...]
    s = (y * jax.nn.sigmoid(s_gate)).astype(BF16)
    o1, o2 = SSD_WIDTH, SSD_WIDTH + DIFF_WIDTH
    mix = (jnp.dot(a, wo_ref[0:o1, :], preferred_element_type=F32)
           + jnp.dot(b, wo_ref[o1:o2, :], preferred_element_type=F32)
           + jnp.dot(s, wo_ref[o2:, :], preferred_element_type=F32))
    x1_ext = x_ext + ga_ref[0] * _rms(mix, nwm_ref[...])
    x1 = x1_ext[0:tm]
    h_ext = _rms(x1_ext, nw1_ref[...]) * (1.0 + sc_ref[0]) + sh_ref[0]
    row_ext = lax.broadcasted_iota(jnp.int32, (tm + 2 * HALO_ROWS, 1), 0)
    outside = (((row_ext >= tm) & (row_ext < tm + HALO_ROWS) & is_first)
               | ((row_ext >= tm + HALO_ROWS) & is_last))
    h_ext = jnp.where(outside, 0.0, h_ext).astype(BF16)
    h = h_ext[0:tm]
    row = lax.broadcasted_iota(jnp.int32, (tm, 1), 0)
    prev_row, next_row = tm + HALO_ROWS - 1, tm + HALO_ROWS
    for lo in range(0, wg_ref.shape[1], FFN_CHUNK):
        hi = lo + FFN_CHUNK
        g_ext = jnp.dot(h_ext, wg_ref[:, lo:hi], preferred_element_type=F32)
        g = g_ext[0:tm]
        g_prev = jnp.where(row == 0, g_ext[prev_row:prev_row + 1], pltpu.roll(g, 1, 0))
        g_next = jnp.where(row == tm - 1, g_ext[next_row:next_row + 1], pltpu.roll(g, tm - 1, 0))
        conv = (cw_ref[0:1, lo:hi] * g_prev + cw_ref[1:2, lo:hi] * g + cw_ref[2:3, lo:hi] * g_next
                + cb_ref[:, lo:hi])
        up = jnp.dot(h, wu_ref[:, lo:hi], preferred_element_type=F32)
        act_ref[:, lo:hi] = (_silu(conv) * up).astype(BF16)
    f = jnp.dot(act_ref[...], wd_ref[...], preferred_element_type=F32)
    o_ref[0] = x1 + gf_ref[0] * _rms(f, nw2_ref[...])


def _mix_ffn(a, b, ys, u, x, g_a, shift, scale, g_f, s5_d, glu_w, glu_b, w_out, norm_mix,
             norm_pre, norm_post, w_gate, w_up, conv_w, conv_b, w_down):
    bsz, n, d = x.shape
    tm = min(FFN_ROW_TILE, n)
    f = w_gate.shape[1]
    nbh = n // HALO_ROWS
    tbh = tm // HALO_ROWS
    per_b = lambda bi, i: (bi, 0, 0)
    const = lambda bi, i: (0, 0)
    once = dict(pipeline_mode=pl.Buffered(1))

    def row_specs(w):
        return [pl.BlockSpec((1, tm, w), lambda bi, i: (bi, i, 0)),
                pl.BlockSpec((1, HALO_ROWS, w), lambda bi, i: (bi, jnp.maximum(i * tbh - 1, 0), 0)),
                pl.BlockSpec((1, HALO_ROWS, w), lambda bi, i: (bi, jnp.minimum((i + 1) * tbh, nbh - 1), 0))]

    row_args, in_specs = [], []
    for arr in (a, b, ys, u, x):
        row_args += [arr] * 3
        in_specs += row_specs(arr.shape[2])
    in_specs += [pl.BlockSpec((1, 1, d), per_b)] * 4
    in_specs += [pl.BlockSpec((1, S5_WIDTH), const), pl.BlockSpec((S5_WIDTH, S5_WIDTH), const),
                 pl.BlockSpec((1, S5_WIDTH), const), pl.BlockSpec((d, d), const, **once),
                 pl.BlockSpec((1, d), const), pl.BlockSpec((1, d), const), pl.BlockSpec((1, d), const),
                 pl.BlockSpec((d, f), const, **once), pl.BlockSpec((d, f), const, **once),
                 pl.BlockSpec((FFN_CONV, f), const), pl.BlockSpec((1, f), const),
                 pl.BlockSpec((f, d), const, **once)]
    return pl.pallas_call(
        _mix_ffn_kernel,
        grid=(bsz, n // tm),
        in_specs=in_specs,
        out_specs=pl.BlockSpec((1, tm, d), lambda bi, i: (bi, i, 0)),
        out_shape=jax.ShapeDtypeStruct((bsz, n, d), F32),
        scratch_shapes=[pltpu.VMEM((tm, f), BF16)],
        compiler_params=_cparams(2),
        name="mix_ffn",
    )(*row_args, g_a, shift, scale, g_f, s5_d.reshape(1, -1), glu_w.astype(BF16), glu_b.reshape(1, -1),
      w_out.astype(BF16), norm_mix.reshape(1, -1), norm_pre.reshape(1, -1), norm_post.reshape(1, -1),
      w_gate.astype(BF16), w_up.astype(BF16), conv_w, conv_b.reshape(1, -1), w_down.astype(BF16))


def kernel(x, c, ctx, c_ctx, mod_w, mod_b, mix_norm_pre, mix_norm_post, ffn_norm_pre, ffn_norm_post, w_in, w_out, ssd_conv_w, ssd_conv_b, ssd_dt_bias, ssd_a_log, ssd_d, ssd_norm_w, diff_lam_q1, diff_lam_k1, diff_lam_q2, diff_lam_k2, diff_subln_w, s5_lam_re, s5_lam_im, s5_log_step, s5_b_re, s5_b_im, s5_c_re, s5_c_im, s5_d, s5_glu_w, s5_glu_b, ffn_w_gate, ffn_w_up, ffn_conv_w, ffn_conv_b, ffn_w_down):
    bsz, n_lat, d = x.shape
    n_layers = mod_w.shape[0]
    rope_tabs = _rope_tables(n_lat)
    cc = jnp.concatenate([c, jnp.broadcast_to(c_ctx[None, :], (SUBLANES, d))], axis=0)
    mod = _modulation(cc, mod_w, mod_b)

    for layer in range(n_layers):
        need_ctx = layer < n_layers - 1
        lam_init = 0.8 - 0.6 * math.exp(-0.3 * layer)
        mod_l = mod[layer, :bsz].reshape(bsz, N_MOD, 1, d)
        mod_c = jnp.broadcast_to(mod[layer, bsz].reshape(1, N_MOD, 1, d), (bsz, N_MOD, 1, d))
        sh_a, sc_a, g_a, sh_f, sc_f, g_f = (mod_l[:, i] for i in range(N_MOD))
        csh_a, csc_a, cg_a, csh_f, csc_f, cg_f = (mod_c[:, i] for i in range(N_MOD))

        w_merged = _merge_w_in(w_in[layer])
        in_p = (mix_norm_pre[layer], w_merged, ssd_conv_w[layer], ssd_conv_b[layer])
        zx_l, dt_l, q_l, k_l, v_l, u_l, u16_l = _in_proj(x, sh_a, sc_a, *in_p, rope_tabs)
        zx_c, dt_c, q_c, k_c, v_c, u_c, u16_c = _in_proj(ctx, csh_a, csc_a, *in_p, None)

        a_l, a_c = _ssd(zx_l, dt_l, zx_c, dt_c, ssd_dt_bias[layer], ssd_a_log[layer], ssd_d[layer],
                        ssd_norm_w[layer], need_ctx)

        lam_params = (diff_lam_q1[layer], diff_lam_k1[layer], diff_lam_q2[layer], diff_lam_k2[layer])
        b_l = _attention(q_l, (k_c, k_l), (v_c, v_l), lam_params, diff_subln_w[layer], lam_init)

        mats = _s5_matrices(s5_lam_re[layer], s5_lam_im[layer], s5_log_step[layer], s5_b_re[layer],
                            s5_b_im[layer], s5_c_re[layer], s5_c_im[layer])
        ys_l, ys_c = _s5(u16_l, u16_c, mats)

        mix_p = (s5_d[layer], s5_glu_w[layer], s5_glu_b[layer], w_out[layer], mix_norm_post[layer])
        ffn_p = (ffn_norm_pre[layer], ffn_norm_post[layer], ffn_w_gate[layer], ffn_w_up[layer],
                 ffn_conv_w[layer], ffn_conv_b[layer], ffn_w_down[layer])
        x = _mix_ffn(a_l, b_l, ys_l, u_l, x, g_a, sh_f, sc_f, g_f, *mix_p, *ffn_p)
        if need_ctx:
            b_c = _attention(q_c, (k_c,), (v_c,), lam_params, diff_subln_w[layer], lam_init)
            ctx = _mix_ffn(a_c, b_c, ys_c, u_c, ctx, cg_a, csh_f, csc_f, cg_f, *mix_p, *ffn_p)
    return x
```

```python
import functools
import math

import numpy as np
import jax
import jax.numpy as jnp
from jax import lax
from jax.experimental import pallas as pl
from jax.experimental.pallas import tpu as pltpu

F32 = jnp.float32
BF16 = jnp.bfloat16
HI = lax.Precision.HIGHEST

D_MODEL = 1024
N_LAYERS = 2
GRID_W = 64
EPS = 1e-6
N_MOD = 6
SSD_HEADS = 4
SSD_HEAD_DIM = 64
SSD_WIDTH = 256
SSD_GROUPS = 2
SSD_STATE = 64
SSD_CONV = 5
SSD_CHUNK = 128
SSD_ROWS = 64
CONV_ROWS = 64
SSD_XBC = 512
SSD_IN = 776
DIFF_HEADS = 4
DIFF_HEAD_DIM = 64
DIFF_WIDTH = 512
DIFF_IN = 1536
ROPE_THETA = 10000.0
ATTN_Q_TILE = 2048
ATTN_UNIT = 1024
ATTN_ROW_SPLIT = 128
S5_GROUP = 16
S5_WIDTH = 256
S5_GROUPS = 16
S5_STATE = 64
S5_T = 16
FFN_DIM = 2816
FFN_CHUNK = 256
FFN_ROW_TILE = 1024
HALO_ROWS = 16
FFN_CONV = 3

LANES = 128
SUBLANES = 8
W_IN_PAD = 2688
VMEM_LIMIT = 56 * 1024 * 1024


def _cparams(n_axes):
    return pltpu.CompilerParams(dimension_semantics=("parallel",) * n_axes,
                                vmem_limit_bytes=VMEM_LIMIT)


def _rms(x, w):
    return x * lax.rsqrt(jnp.mean(x * x, axis=-1, keepdims=True) + EPS) * w


def _silu(x):
    return x * jax.nn.sigmoid(x)


def _row_tile(n):
    return min(1024, n)


def _mod_kernel(c_ref, w_ref, b_ref, o_ref):
    a = _silu(c_ref[...])
    o_ref[0] = jnp.dot(a, w_ref[0], preferred_element_type=F32, precision=HI) + b_ref[0]


def _modulation(cc, mod_w, mod_b):
    nl, d, n = mod_w.shape
    r = cc.shape[0]
    tn = 512
    return pl.pallas_call(
        _mod_kernel,
        grid=(nl, n // tn),
        in_specs=[pl.BlockSpec((r, d), lambda l, j: (0, 0)),
                  pl.BlockSpec((1, d, tn), lambda l, j: (l, 0, j)),
                  pl.BlockSpec((1, 1, tn), lambda l, j: (l, 0, j))],
        out_specs=pl.BlockSpec((1, r, tn), lambda l, j: (l, 0, j)),
        out_shape=jax.ShapeDtypeStruct((nl, r, n), F32),
        compiler_params=_cparams(2),
        name="modulation",
    )(cc, mod_w, mod_b.reshape(nl, 1, n))


def _in_proj_kernel(*refs, rope):
    (x_ref, xp_ref, xn_ref, sh_ref, sc_ref, nw_ref, w_ref, cw_ref, cb_ref) = refs[:9]
    cos_ref, sin_ref = refs[9:11] if rope else (None, None)
    zx_ref, dt_ref, q_ref, k_ref, v_ref, u_ref, u16_ref = refs[-7:]
    i = pl.program_id(1)
    tm = x_ref.shape[1]
    x_ext = jnp.concatenate([xp_ref[0], x_ref[0], xn_ref[0]], axis=0)
    h_ext = _rms(x_ext, nw_ref[...]) * (1.0 + sc_ref[0]) + sh_ref[0]
    row_ext = lax.broadcasted_iota(jnp.int32, (tm + 2 * HALO_ROWS, 1), 0)
    outside = (((row_ext < HALO_ROWS) & (i == 0))
               | ((row_ext >= tm + HALO_ROWS) & (i == pl.num_programs(1) - 1)))
    h_ext = jnp.where(outside, 0.0, h_ext).astype(BF16)
    hb = h_ext[HALO_ROWS:HALO_ROWS + tm]

    def mm(lo, hi):
        return jnp.dot(hb, w_ref[:, lo:hi], preferred_element_type=F32)

    xbc = jnp.dot(h_ext, w_ref[:, SSD_WIDTH:SSD_WIDTH + SSD_XBC], preferred_element_type=F32)
    halo = (SSD_CONV - 1) // 2
    win = CONV_ROWS + 2 * SUBLANES
    for r in range(0, tm, CONV_ROWS):
        lo = HALO_ROWS - SUBLANES + r
        window = xbc[lo:lo + win]
        acc = jnp.broadcast_to(cb_ref[...], (CONV_ROWS, SSD_XBC))
        for k in range(SSD_CONV):
            shift = (halo - k) % win
            tap = window if shift == 0 else pltpu.roll(window, shift, 0)
            acc = acc + cw_ref[k:k + 1, :] * tap[SUBLANES:SUBLANES + CONV_ROWS]
        zx_ref[0, r:r + CONV_ROWS, SSD_WIDTH:] = _silu(acc)

    def rot(p):
        lane = lax.broadcasted_iota(jnp.int32, p.shape, 1)
        partner = jnp.where(lane % 32 < 16, pltpu.roll(p, LANES - 16, 1), pltpu.roll(p, 16, 1))
        return p * cos_ref[...] + partner * sin_ref[...]

    o_zx = SSD_WIDTH + SSD_XBC
    zx_ref[0, :, 0:SSD_WIDTH] = mm(0, SSD_WIDTH)
    q_all = mm(o_zx, o_zx + DIFF_WIDTH)
    k_all = mm(o_zx + DIFF_WIDTH, o_zx + 2 * DIFF_WIDTH)
    for j in range(DIFF_WIDTH // LANES):
        q = q_all[:, j * LANES:(j + 1) * LANES] * (DIFF_HEAD_DIM ** -0.5)
        k = k_all[:, j * LANES:(j + 1) * LANES]
        if rope:
            q, k = rot(q), rot(k)
        q_ref[0, :, j * LANES:(j + 1) * LANES] = q.astype(BF16)
        k_ref[0, j * LANES:(j + 1) * LANES, :] = k.T.astype(BF16)
    o_v = o_zx + 2 * DIFF_WIDTH
    v_ref[0] = mm(o_v, o_v + DIFF_WIDTH).astype(BF16)
    o_u = o_v + DIFF_WIDTH
    u_dt = mm(o_u, W_IN_PAD)
    u = u_dt[:, 0:S5_WIDTH]
    u_ref[0] = u
    u16_ref[0] = u.astype(BF16)
    dt_ref[0] = u_dt[:, S5_WIDTH:]


def _in_proj(x, shift, scale, norm_w, w_merged, conv_w, conv_b, rope_tabs):
    bsz, n, d = x.shape
    tm = _row_tile(n)
    nbh = n // HALO_ROWS
    tbh = tm // HALO_ROWS
    rope = rope_tabs is not None
    row = lambda b, i: (b, i, 0)
    per_b = lambda b, i: (b, 0, 0)
    const = lambda b, i: (0, 0)
    in_specs = [pl.BlockSpec((1, tm, d), row),
                pl.BlockSpec((1, HALO_ROWS, d), lambda b, i: (b, jnp.maximum(i * tbh - 1, 0), 0)),
                pl.BlockSpec((1, HALO_ROWS, d), lambda b, i: (b, jnp.minimum((i + 1) * tbh, nbh - 1), 0)),
                pl.BlockSpec((1, 1, d), per_b),
                pl.BlockSpec((1, 1, d), per_b),
                pl.BlockSpec((1, d), const),
                pl.BlockSpec((d, W_IN_PAD), const, pipeline_mode=pl.Buffered(1)),
                pl.BlockSpec((SSD_CONV, SSD_XBC), const),
                pl.BlockSpec((1, SSD_XBC), const)]
    args = [x, x, x, shift, scale, norm_w.reshape(1, d), w_merged, conv_w, conv_b.reshape(1, -1)]
    if rope:
        in_specs += [pl.BlockSpec((tm, LANES), lambda b, i: (i, 0))] * 2
        args += list(rope_tabs)
    widths = (SSD_WIDTH + SSD_XBC, LANES, DIFF_WIDTH, DIFF_WIDTH, DIFF_WIDTH, S5_WIDTH, S5_WIDTH)
    dtypes = (F32, F32, BF16, BF16, BF16, F32, BF16)
    out_specs = [pl.BlockSpec((1, tm, w), row) for w in widths]
    out_shape = [jax.ShapeDtypeStruct((bsz, n, w), t) for w, t in zip(widths, dtypes)]
    out_specs[3] = pl.BlockSpec((1, DIFF_WIDTH, tm), lambda b, i: (b, 0, i))
    out_shape[3] = jax.ShapeDtypeStruct((bsz, DIFF_WIDTH, n), BF16)
    return pl.pallas_call(
        functools.partial(_in_proj_kernel, rope=rope),
        grid=(bsz, n // tm),
        in_specs=in_specs,
        out_specs=out_specs,
        out_shape=out_shape,
        compiler_params=_cparams(2),
        name="in_proj_rope" if rope else "in_proj",
    )(*args)


def _merge_w_in(w):
    o1, o2 = SSD_IN, SSD_IN + DIFF_IN
    parts = [w[:, :SSD_WIDTH + SSD_XBC], w[:, o1:o2], w[:, o2:], w[:, SSD_WIDTH + SSD_XBC:o1]]
    m = jnp.concatenate(parts, axis=1)
    return jnp.pad(m, ((0, 0), (0, W_IN_PAD - m.shape[1]))).astype(BF16)


def _rope_tables(n):
    t = np.arange(n)
    pos = np.stack([t // GRID_W, t % GRID_W], axis=1).astype(np.float32)
    quarter = DIFF_HEAD_DIM // 4
    inv_freq = (np.float32(ROPE_THETA) ** (-np.arange(quarter, dtype=np.float32) / np.float32(quarter)))
    lane = np.arange(LANES)
    axis = (lane % DIFF_HEAD_DIM) // (DIFF_HEAD_DIM // 2)
    freq = lane % quarter
    ang = (pos[:, axis] * inv_freq.astype(np.float32)[freq][None, :]).astype(np.float32)
    sign = np.where(lane % (2 * quarter) < quarter, -1.0, 1.0).astype(np.float32)
    return jnp.asarray(np.cos(ang), F32), jnp.asarray(np.sin(ang) * sign[None, :], F32)


def _cumsum_rows(x, reverse):
    n = x.shape[0]
    row = lax.broadcasted_iota(jnp.int32, x.shape, 0)
    k = 1
    while k < n:
        if reverse:
            x = x + jnp.where(row < n - k, pltpu.roll(x, n - k, 0), 0.0)
        else:
            x = x + jnp.where(row >= k, pltpu.roll(x, k, 0), 0.0)
        k *= 2
    return x


def _ssd_chunk(xact_ref, dtt_ref, dta_ref, exp_ref, state_ref, y_ref, c, direction):
    q = SSD_CHUNK
    nh = SSD_HEADS
    r0 = pl.multiple_of(c * q, q)
    xa = xact_ref[pl.ds(r0, q), :]
    dta = dta_ref[pl.ds(r0, q), :]
    dt_t = dtt_ref[c]
    rows = lax.broadcasted_iota(jnp.int32, (q, q), 0)
    cols = lax.broadcasted_iota(jnp.int32, (q, q), 1)
    lane = cols
    mask = (rows >= cols) if direction == 0 else (rows <= cols)
    acs = _cumsum_rows(dta, reverse=direction == 1)
    acs_t = acs.T
    hi = acs.astype(BF16).astype(F32)
    r1 = acs - hi
    mid = r1.astype(BF16).astype(F32)
    lo = (r1 - mid).astype(BF16).astype(F32)
    parts = hi + pltpu.roll(mid, 8, 1) + pltpu.roll(lo, 16, 1)
    neg = -pltpu.roll(parts, 24, 1)
    lhs = jnp.where(lane < 24, parts, jnp.where(lane < 48, 1.0, 0.0)).astype(BF16)
    rhs = []
    for h in range(nh):
        j = direction * nh + h
        is_one = (lane == j) | (lane == 8 + j) | (lane == 16 + j)
        is_neg = (lane == 24 + j) | (lane == 32 + j) | (lane == 40 + j)
        rhs.append(jnp.where(is_one, 1.0, jnp.where(is_neg, neg, 0.0)).astype(BF16))
    diff = lax.dot_general(lhs, jnp.concatenate(rhs, axis=0), (((1,), (1,)), ((), ())),
                           preferred_element_type=F32)
    mask4 = jnp.concatenate([mask] * nh, axis=1)
    decay = jnp.exp(jnp.where(mask4, diff, -jnp.inf))
    acs_x = jnp.dot(parts.astype(BF16), exp_ref[direction], preferred_element_type=F32)
    last = q - 1 if direction == 0 else 0
    total_x = acs_x[last:last + 1, :]

    xb = xa[:, 0:SSD_WIDTH].astype(BF16)
    bm = xa[:, SSD_WIDTH:SSD_WIDTH + LANES]
    cm = xa[:, SSD_WIDTH + LANES:]
    bm_b = bm.astype(BF16)
    bm_t = bm.T
    state = state_ref[direction]
    y_off = jnp.dot(cm.astype(BF16), state.astype(BF16), preferred_element_type=F32) * jnp.exp(acs_x)
    row_grp = lax.broadcasted_iota(jnp.int32, (LANES, SSD_WIDTH), 0) // SSD_STATE
    lane_head = lax.broadcasted_iota(jnp.int32, (LANES, SSD_WIDTH), 1) // SSD_HEAD_DIM
    upd = jnp.zeros((LANES, SSD_WIDTH), F32)
    y_diag = []
    for g in range(SSD_GROUPS):
        cg = jnp.where(lane // SSD_STATE == g, cm, 0.0).astype(BF16)
        cb = lax.dot_general(cg, bm_b, (((1,), (1,)), ((), ())), preferred_element_type=F32)
        xg = xb[:, g * LANES:(g + 1) * LANES]
        pair = []
        for h in range(g * 2, g * 2 + 2):
            j = direction * nh + h
            row_dt = dt_t[j:j + 1, :]
            scores = (cb * decay[:, h * q:(h + 1) * q] * row_dt).astype(BF16)
            pair.append(jnp.dot(scores, xg, preferred_element_type=F32))
            w_row = row_dt * jnp.exp(acs_t[j:j + 1, last:last + 1] - acs_t[j:j + 1, :])
            upd_h = jnp.dot((bm_t * w_row).astype(BF16), xb, preferred_element_type=F32)
            upd = upd + jnp.where((row_grp == g) & (lane_head == h), upd_h, 0.0)
        y_diag.append(jnp.where(lane < SSD_HEAD_DIM, pair[0], pair[1]))
    y_ref[pl.ds(r0, q), :] += jnp.concatenate(y_diag, axis=-1) + y_off
    state_ref[direction] = jnp.exp(total_x) * state + upd


def _ssd_kernel(*refs, n_ctx, n_lat, need_ctx):
    (zx_l, dt_l, zx_c, dt_c, dtb_ref, alog_ref, d_ref, nw_ref, exp_ref) = refs[:9]
    outs = refs[9:11] if need_ctx else (refs[9], None)
    xact_ref, dtt_ref, dta_ref, y_ref, state_ref = refs[-5:]
    out_l, out_c = outs

    for seg_ref, n, base in ((zx_c, n_ctx, 0), (zx_l, n_lat, n_ctx)):
        def copy_rows(i, carry, seg_ref=seg_ref, base=base):
            r = pl.multiple_of(i * SSD_CHUNK, SSD_CHUNK)
            act = seg_ref[0, pl.ds(r, SSD_CHUNK), SSD_WIDTH:]
            xact_ref[pl.ds(base + r, SSD_CHUNK), :] = act
            y_ref[pl.ds(base + r, SSD_CHUNK), :] = act[:, 0:SSD_WIDTH] * d_ref[...]
            return carry

        lax.fori_loop(0, n // SSD_CHUNK, copy_rows, 0)
    a_neg = -jnp.exp(alog_ref[...])
    head_lane = lax.broadcasted_iota(jnp.int32, (SSD_CHUNK, LANES), 1) < 2 * SSD_HEADS
    for seg_ref, n, base in ((dt_c, n_ctx, 0), (dt_l, n_lat, n_ctx)):
        for t0 in range(0, n, SSD_CHUNK):
            dt = jax.nn.softplus(seg_ref[0, t0:t0 + SSD_CHUNK, :] + dtb_ref[...])
            dtt_ref[(base + t0) // SSD_CHUNK] = dt.T
            dta_ref[base + t0:base + t0 + SSD_CHUNK, :] = jnp.where(head_lane, dt * a_neg, 0.0)
    n_tot = n_ctx + n_lat
    state_ref[...] = jnp.zeros(state_ref.shape, F32)

    ncc, nct = n_ctx // SSD_CHUNK, n_tot // SSD_CHUNK

    def step(i, carry):
        _ssd_chunk(xact_ref, dtt_ref, dta_ref, exp_ref, state_ref, y_ref, i, 0)
        cb_idx = jnp.where(i < ncc, ncc - 1 - i, nct - 1 - (i - ncc))
        _ssd_chunk(xact_ref, dtt_ref, dta_ref, exp_ref, state_ref, y_ref, cb_idx, 1)
        return carry

    lax.fori_loop(0, nct, step, 0, unroll=3)

    for seg_ref, o_ref, n, base in ((zx_c, out_c, n_ctx, 0), (zx_l, out_l, n_lat, n_ctx)):
        if o_ref is None:
            continue
        def gate_rows(i, carry, seg_ref=seg_ref, o_ref=o_ref, base=base):
            r = pl.multiple_of(i * SSD_ROWS, SSD_ROWS)
            z = seg_ref[0, pl.ds(r, SSD_ROWS), 0:SSD_WIDTH]
            v = y_ref[pl.ds(base + r, SSD_ROWS), :] * _silu(z)
            o_ref[0, pl.ds(r, SSD_ROWS), :] = _rms(v, nw_ref[...]).astype(BF16)
            return carry

        lax.fori_loop(0, n // SSD_ROWS, gate_rows, 0, unroll=4)


def _ssd(zx_l, dt_l, zx_c, dt_c, dt_bias, a_log, d_skip, norm_w, need_ctx):
    bsz, n_lat, wzx = zx_l.shape
    n_ctx = zx_c.shape[1]
    n_tot = n_ctx + n_lat
    pad8 = lambda v: jnp.pad(v.reshape(1, -1), ((0, 0), (0, LANES - v.size)))
    per_b = lambda b: (b, 0, 0)
    const = lambda b: (0, 0)
    k_idx = np.arange(LANES)[None, :, None]
    d_idx = np.arange(2)[:, None, None]
    h_idx = (np.arange(SSD_WIDTH) // SSD_HEAD_DIM)[None, None, :]
    expand = jnp.asarray((k_idx < 24) & (k_idx % 8 == d_idx * SSD_HEADS + h_idx), BF16)
    params = [pad8(dt_bias), pad8(a_log),
              jnp.repeat(d_skip, SSD_HEAD_DIM).reshape(1, -1), norm_w.reshape(1, -1)]
    in_specs = [pl.BlockSpec((1, n_lat, wzx), per_b), pl.BlockSpec((1, n_lat, LANES), per_b),
                pl.BlockSpec((1, n_ctx, wzx), per_b), pl.BlockSpec((1, n_ctx, LANES), per_b)]
    in_specs += [pl.BlockSpec(p.shape, const) for p in params]
    in_specs += [pl.BlockSpec(expand.shape, lambda b: (0, 0, 0))]
    out_specs = [pl.BlockSpec((1, n_lat, SSD_WIDTH), per_b)]
    out_shape = [jax.ShapeDtypeStruct((bsz, n_lat, SSD_WIDTH), BF16)]
    if need_ctx:
        out_specs.append(pl.BlockSpec((1, n_ctx, SSD_WIDTH), per_b))
        out_shape.append(jax.ShapeDtypeStruct((bsz, n_ctx, SSD_WIDTH), BF16))
    scratch = [pltpu.VMEM((n_tot, SSD_XBC), F32),
               pltpu.VMEM((n_tot // SSD_CHUNK, LANES, SSD_CHUNK), F32),
               pltpu.VMEM((n_tot, LANES), F32),
               pltpu.VMEM((n_tot, SSD_WIDTH), F32),
               pltpu.VMEM((2, SSD_GROUPS * SSD_STATE, SSD_WIDTH), F32)]
    outs = pl.pallas_call(
        functools.partial(_ssd_kernel, n_ctx=n_ctx, n_lat=n_lat, need_ctx=need_ctx),
        grid=(bsz,),
        in_specs=in_specs,
        out_specs=out_specs,
        out_shape=out_shape,
        scratch_shapes=scratch,
        compiler_params=_cparams(1),
        name="ssd_mixer",
    )(zx_l, dt_l, zx_c, dt_c, *params, expand)
    return (outs[0], outs[1]) if need_ctx else (outs[0], None)


def _attn_kernel(*refs, n_kv, lam_init):
    q_ref = refs[0]
    k_refs = refs[1:1 + n_kv]
    v_refs = refs[1 + n_kv:1 + 2 * n_kv]
    lq1, lk1, lq2, lk2, sw_ref, o_ref = refs[1 + 2 * n_kv:-4]
    s_refs, e_refs = refs[-4:-2], refs[-2:]
    tq, hw = q_ref.shape[1:]
    unit = s_refs[0].shape[0]
    lane = lax.broadcasted_iota(jnp.int32, (unit, hw), 1)
    lam = (jnp.exp(jnp.sum(lq1[...] * lk1[...], axis=-1, keepdims=True))
           - jnp.exp(jnp.sum(lq2[...] * lk2[...], axis=-1, keepdims=True)) + lam_init)
    offs = np.cumsum([0] + [k_ref.shape[2] for k_ref in k_refs])
    v_ext = [jnp.concatenate([v_ref[0], jnp.ones(v_ref.shape[1:], BF16)], axis=-1) for v_ref in v_refs]
    rows = max(ATTN_ROW_SPLIT, unit // 4)
    pieces = [(lo, min(lo + rows, unit)) for lo in range(0, unit, rows)]
    for base in range(0, tq, unit):
        q = q_ref[0, base:base + unit, :]
        for comp, s_ref in enumerate(s_refs):
            own = (lane >= DIFF_HEAD_DIM) if comp else (lane < DIFF_HEAD_DIM)
            qc = jnp.where(own, q, jnp.zeros_like(q))
            for i, k_ref in enumerate(k_refs):
                s_ref[:, offs[i]:offs[i + 1]] = jnp.dot(qc, k_ref[0], preferred_element_type=F32)
        outs = []
        for s_ref, e_ref in zip(s_refs, e_refs):
            parts = []
            for lo, hi in pieces:
                s = s_ref[lo:hi, :]
                e_ref[lo:hi, :] = jnp.exp(s - jnp.max(s, axis=-1, keepdims=True)).astype(BF16)
                ov = None
                for i in range(n_kv):
                    part = jnp.dot(e_ref[lo:hi, offs[i]:offs[i + 1]], v_ext[i], preferred_element_type=F32)
                    ov = part if ov is None else ov + part
                parts.append(ov[:, 0:hw] * (1.0 / ov[:, hw:hw + 1]))
            outs.append(parts)
        for (lo, hi), o0, o1 in zip(pieces, *outs):
            o = o0 - lam * o1
            o_ref[0, base + lo:base + hi, :] = (_rms(o, sw_ref[...]) * (1.0 - lam_init)).astype(BF16)


def _attention(q, ks, vs, lam_params, subln_w, lam_init):
    bsz, n, _ = q.shape
    tq = min(ATTN_Q_TILE, n)
    n_kv = len(ks)
    n_keys = sum(a.shape[2] for a in ks)
    hw = 2 * DIFF_HEAD_DIM
    in_specs = [pl.BlockSpec((1, tq, hw), lambda b, h, i: (b, i, h))]
    in_specs += [pl.BlockSpec((1, hw, a.shape[2]), lambda b, h, i: (b, h, 0)) for a in ks]
    in_specs += [pl.BlockSpec((1, a.shape[1], hw), lambda b, h, i: (b, 0, h)) for a in vs]
    in_specs += [pl.BlockSpec((1, DIFF_HEAD_DIM), lambda b, h, i: (0, 0))] * 4
    in_specs += [pl.BlockSpec((1, hw), lambda b, h, i: (0, 0))]
    return pl.pallas_call(
        functools.partial(_attn_kernel, n_kv=n_kv, lam_init=lam_init),
        grid=(bsz, DIFF_HEADS, n // tq),
        in_specs=in_specs,
        out_specs=pl.BlockSpec((1, tq, hw), lambda b, h, i: (b, i, h)),
        out_shape=jax.ShapeDtypeStruct((bsz, n, DIFF_WIDTH), BF16),
        scratch_shapes=([pltpu.VMEM((min(ATTN_UNIT, tq), n_keys), F32)] * 2
                        + [pltpu.VMEM((min(ATTN_UNIT, tq), n_keys), BF16)] * 2),
        compiler_params=_cparams(3),
        name="diff_attention",
    )(q, *ks, *vs, *[p.reshape(1, -1) for p in lam_params], subln_w.reshape(1, -1))


def _s5_matrices(lam_re, lam_im, log_step, b_re, b_im, c_re, c_im):
    t = S5_T
    step = jnp.exp(log_step.astype(F32))[..., None]
    lr, li = lam_re.astype(F32), lam_im.astype(F32)
    mag = jnp.exp(lr * step)
    a = (mag * jnp.cos(li * step)) + 1j * (mag * jnp.sin(li * step))
    lam = lr + 1j * li
    bb = ((a - 1.0) / lam)[..., None] * (b_re.astype(F32) + 1j * b_im.astype(F32))
    cc = c_re.astype(F32) + 1j * c_im.astype(F32)
    pw = [jnp.ones_like(a)]
    for _ in range(t):
        pw.append(pw[-1] * a)
    pw = jnp.stack(pw, axis=0)
    kern = jnp.real(jnp.einsum('dgcn,tdgn,dgnk->dgktc', cc, pw[:t], bb))
    gc = S5_GROUP
    fwd = kern[0].reshape(S5_GROUPS, gc, t * gc)
    bwd = kern[1][:, :, ::-1].reshape(S5_GROUPS, gc, t * gc)
    band = jnp.concatenate([bwd[..., :(t - 1) * gc], bwd[..., (t - 1) * gc:] + fwd[..., :gc],
                            fwd[..., gc:]], axis=-1)
    k_full = jnp.concatenate([band[..., (t - 1 - s) * gc:(2 * t - 1 - s) * gc] for s in range(t)],
                             axis=1)
    wf = pw[t - 1 - np.arange(t), 0][..., None] * bb[0][None]
    wb = pw[np.arange(t), 1][..., None] * bb[1][None]
    def rows_sc(w):
        return w.transpose(1, 0, 3, 2).reshape(S5_GROUPS, t * S5_GROUP, S5_STATE)
    w_state = jnp.concatenate([rows_sc(jnp.real(wf)), rows_sc(jnp.real(wb)),
                               rows_sc(jnp.imag(wf)), rows_sc(jnp.imag(wb))], axis=-1)
    cf = cc[0][None] * pw[1 + np.arange(t), 0][:, :, None, :]
    cbk = cc[1][None] * pw[t - np.arange(t), 1][:, :, None, :]
    def cols_tc(w):
        return w.transpose(1, 3, 0, 2).reshape(S5_GROUPS, S5_STATE, t * S5_GROUP)
    c_off = jnp.concatenate([cols_tc(jnp.real(cf)), cols_tc(jnp.real(cbk)),
                             cols_tc(-jnp.imag(cf)), cols_tc(-jnp.imag(cbk))], axis=1)
    at = pw[t]
    a_rows = jnp.stack([jnp.concatenate([jnp.real(at[0]), jnp.real(at[1])], axis=-1),
                        jnp.concatenate([jnp.imag(at[0]), jnp.imag(at[1])], axis=-1)], axis=1)
    a_chunk = jnp.pad(a_rows, ((0, 0), (0, SUBLANES - 2), (0, 0)))
    return k_full.astype(BF16), w_state.astype(BF16), c_off.astype(BF16), a_chunk.astype(F32)


def _s5_kernel(uc_ref, ul_ref, kf_ref, ws_ref, co_ref, a_ref, yc_ref, yl_ref, e_ref, h_ref):
    ncc, bsz, w = uc_ref.shape[1:]
    ncl = ul_ref.shape[1]
    nct = ncc + ncl
    uc = uc_ref[0].reshape(ncc * bsz, w)
    ul = ul_ref[0].reshape(ncl * bsz, w)
    e_ref[0:ncc] = jnp.dot(uc, ws_ref[0], preferred_element_type=F32).reshape(ncc, bsz, w)
    e_ref[ncc:nct] = jnp.dot(ul, ws_ref[0], preferred_element_type=F32).reshape(ncl, bsz, w)
    ar = a_ref[0, 0:1, :]
    ai = a_ref[0, 1:2, :]
    half = S5_STATE
    lane = lax.broadcasted_iota(jnp.int32, (bsz, LANES), 1)
    is_f = lane < half

    def step(i, carry):
        h_re, h_im = carry
        jf = i
        jb = jnp.where(i < ncc, ncc - 1 - i, nct - 1 - (i - ncc))
        h_ref[jf, :, 0:half] = h_re[:, 0:half]
        h_ref[jb, :, half:LANES] = h_re[:, half:]
        h_ref[jf, :, LANES:LANES + half] = h_im[:, 0:half]
        h_ref[jb, :, LANES + half:] = h_im[:, half:]
        e_re = jnp.where(is_f, e_ref[jf, :, 0:LANES], e_ref[jb, :, 0:LANES])
        e_im = jnp.where(is_f, e_ref[jf, :, LANES:], e_ref[jb, :, LANES:])
        return (ar * h_re - ai * h_im + e_re, ar * h_im + ai * h_re + e_im)

    zero = jnp.zeros((bsz, LANES), F32)
    lax.fori_loop(0, nct, step, (zero, zero))
    for u, y_ref, lo, nc in ((uc, yc_ref, 0, ncc), (ul, yl_ref, ncc, ncl)):
        hs = h_ref[lo:lo + nc].reshape(nc * bsz, w).astype(BF16)
        y = (jnp.dot(u, kf_ref[0], preferred_element_type=F32)
             + jnp.dot(hs, co_ref[0], preferred_element_type=F32))
        y_ref[0] = y.reshape(nc, bsz, w).astype(BF16)


def _s5_to_chunks(u):
    bsz, n, _ = u.shape
    u = u.reshape(bsz, n // S5_T, S5_T, S5_GROUPS, S5_GROUP).transpose(3, 1, 0, 2, 4)
    return u.reshape(S5_GROUPS, n // S5_T, bsz, S5_T * S5_GROUP)


def _s5_from_chunks(y, bsz):
    nchunk = y.shape[1]
    y = y.reshape(S5_GROUPS, nchunk, bsz, S5_T, S5_GROUP).transpose(2, 1, 3, 0, 4)
    return y.reshape(bsz, nchunk * S5_T, S5_WIDTH)


def _s5(u_l, u_c, mats):
    k_full, w_state, c_off, a_chunk = mats
    bsz = u_l.shape[0]
    uc, ul = _s5_to_chunks(u_c), _s5_to_chunks(u_l)
    ncc, ncl, w = uc.shape[1], ul.shape[1], ul.shape[3]
    nct = ncc + ncl
    blk = lambda g: (g, 0, 0, 0)
    mat = lambda g: (g, 0, 0)
    y_c, y_l = pl.pallas_call(
        _s5_kernel,
        grid=(S5_GROUPS,),
        in_specs=[pl.BlockSpec((1, ncc, bsz, w), blk), pl.BlockSpec((1, ncl, bsz, w), blk),
                  pl.BlockSpec((1, w, w), mat), pl.BlockSpec((1, w, w), mat),
                  pl.BlockSpec((1, w, w), mat), pl.BlockSpec((1, SUBLANES, LANES), mat)],
        out_specs=[pl.BlockSpec((1, ncc, bsz, w), blk), pl.BlockSpec((1, ncl, bsz, w), blk)],
        out_shape=[jax.ShapeDtypeStruct((S5_GROUPS, ncc, bsz, w), BF16),
                   jax.ShapeDtypeStruct((S5_GROUPS, ncl, bsz, w), BF16)],
        scratch_shapes=[pltpu.VMEM((nct, bsz, w), F32), pltpu.VMEM((nct, bsz, w), F32)],
        compiler_params=_cparams(1),
        name="s5_mixer",
    )(uc, ul, k_full, w_state, c_off, a_chunk)
    return _s5_from_chunks(y_l, bsz), _s5_from_chunks(y_c, bsz)


def _mix_ffn_kernel(*refs):
    rows = [refs[3 * j:3 * j + 3] for j in range(5)]
    (ga_ref, sh_ref, sc_ref, gf_ref, d_ref, gw_ref, gb_ref, wo_ref, nwm_ref, nw1_ref, nw2_ref,
     wg_ref, wu_ref, cw_ref, cb_ref, wd_ref, o_ref, act_ref) = refs[15:]
    i = pl.program_id(1)
    is_first = i == 0
    is_last = i == pl.num_programs(1) - 1
    tm = o_ref.shape[1]
    a, b, ys, u, x_ext = (jnp.concatenate([r[0] for r in trio], axis=0) for trio in rows)
    y = jax.nn.gelu(ys + d_ref[...] * u, approximate=True)
    s_gate = jnp.dot(y.astype(BF16), gw_ref[...], preferred_element_type=F32) + gb_ref[...]
    s = (y * jax.nn.sigmoid(s_gate)).astype(BF16)
    mix = jnp.dot(jnp.concatenate([a, b, s], axis=-1), wo_ref[...], preferred_element_type=F32)
    x1_ext = x_ext + ga_ref[0] * _rms(mix, nwm_ref[...])
    x1 = x1_ext[0:tm]
    h_ext = _rms(x1_ext, nw1_ref[...]) * (1.0 + sc_ref[0]) + sh_ref[0]
    row_ext = lax.broadcasted_iota(jnp.int32, (tm + 2 * HALO_ROWS, 1), 0)
    outside = (((row_ext >= tm) & (row_ext < tm + HALO_ROWS) & is_first)
               | ((row_ext >= tm + HALO_ROWS) & is_last))
    h_ext = jnp.where(outside, 0.0, h_ext).astype(BF16)
    h = h_ext[0:tm]
    row = lax.broadcasted_iota(jnp.int32, (tm, 1), 0)
    prev_row, next_row = tm + HALO_ROWS - 1, tm + HALO_ROWS
    for lo in range(0, wg_ref.shape[1], FFN_CHUNK):
        hi = lo + FFN_CHUNK
        g_ext = jnp.dot(h_ext, wg_ref[:, lo:hi], preferred_element_type=F32)
        g = g_ext[0:tm]
        g_prev = jnp.where(row == 0, g_ext[prev_row:prev_row + 1], pltpu.roll(g, 1, 0))
        g_next = jnp.where(row == tm - 1, g_ext[next_row:next_row + 1], pltpu.roll(g, tm - 1, 0))
        conv = (cw_ref[0:1, lo:hi] * g_prev + cw_ref[1:2, lo:hi] * g + cw_ref[2:3, lo:hi] * g_next
                + cb_ref[:, lo:hi])
        up = jnp.dot(h, wu_ref[:, lo:hi], preferred_element_type=F32)
        act_ref[:, lo:hi] = (_silu(conv) * up).astype(BF16)
    f = jnp.dot(act_ref[...], wd_ref[...], preferred_element_type=F32)
    o_ref[0] = x1 + gf_ref[0] * _rms(f, nw2_ref[...])


def _mix_ffn(a, b, ys, u, x, g_a, shift, scale, g_f, s5_d, glu_w, glu_b, w_out, norm_mix,
             norm_pre, norm_post, w_gate, w_up, conv_w, conv_b, w_down):
    bsz, n, d = x.shape
    tm = min(FFN_ROW_TILE, n)
    f = w_gate.shape[1]
    nbh = n // HALO_ROWS
    tbh = tm // HALO_ROWS
    per_b = lambda bi, i: (bi, 0, 0)
    const = lambda bi, i: (0, 0)
    once = dict(pipeline_mode=pl.Buffered(1))

    def row_specs(w):
        return [pl.BlockSpec((1, tm, w), lambda bi, i: (bi, i, 0)),
                pl.BlockSpec((1, HALO_ROWS, w), lambda bi, i: (bi, jnp.maximum(i * tbh - 1, 0), 0)),
                pl.BlockSpec((1, HALO_ROWS, w), lambda bi, i: (bi, jnp.minimum((i + 1) * tbh, nbh - 1), 0))]

    row_args, in_specs = [], []
    for arr in (a, b, ys, u, x):
        row_args += [arr] * 3
        in_specs += row_specs(arr.shape[2])
    in_specs += [pl.BlockSpec((1, 1, d), per_b)] * 4
    in_specs += [pl.BlockSpec((1, S5_WIDTH), const), pl.BlockSpec((S5_WIDTH, S5_WIDTH), const),
                 pl.BlockSpec((1, S5_WIDTH), const), pl.BlockSpec((d, d), const, **once),
                 pl.BlockSpec((1, d), const), pl.BlockSpec((1, d), const), pl.BlockSpec((1, d), const),
                 pl.BlockSpec((d, f), const, **once), pl.BlockSpec((d, f), const, **once),
                 pl.BlockSpec((FFN_CONV, f), const), pl.BlockSpec((1, f), const),
                 pl.BlockSpec((f, d), const, **once)]
    return pl.pallas_call(
        _mix_ffn_kernel,
        grid=(bsz, n // tm),
        in_specs=in_specs,
        out_specs=pl.BlockSpec((1, tm, d), lambda bi, i: (bi, i, 0)),
        out_shape=jax.ShapeDtypeStruct((bsz, n, d), F32),
        scratch_shapes=[pltpu.VMEM((tm, f), BF16)],
        compiler_params=_cparams(2),
        name="mix_ffn",
    )(*row_args, g_a, shift, scale, g_f, s5_d.reshape(1, -1), glu_w.astype(BF16), glu_b.reshape(1, -1),
      w_out.astype(BF16), norm_mix.reshape(1, -1), norm_pre.reshape(1, -1), norm_post.reshape(1, -1),
      w_gate.astype(BF16), w_up.astype(BF16), conv_w, conv_b.reshape(1, -1), w_down.astype(BF16))


def kernel(x, c, ctx, c_ctx, mod_w, mod_b, mix_norm_pre, mix_norm_post, ffn_norm_pre, ffn_norm_post, w_in, w_out, ssd_conv_w, ssd_conv_b, ssd_dt_bias, ssd_a_log, ssd_d, ssd_norm_w, diff_lam_q1, diff_lam_k1, diff_lam_q2, diff_lam_k2, diff_subln_w, s5_lam_re, s5_lam_im, s5_log_step, s5_b_re, s5_b_im, s5_c_re, s5_c_im, s5_d, s5_glu_w, s5_glu_b, ffn_w_gate, ffn_w_up, ffn_conv_w, ffn_conv_b, ffn_w_down):
    bsz, n_lat, d = x.shape
    n_layers = mod_w.shape[0]
    rope_tabs = _rope_tables(n_lat)
    cc = jnp.concatenate([c, jnp.broadcast_to(c_ctx[None, :], (SUBLANES, d))], axis=0)
    mod = _modulation(cc, mod_w, mod_b)

    for layer in range(n_layers):
        need_ctx = layer < n_layers - 1
        lam_init = 0.8 - 0.6 * math.exp(-0.3 * layer)
        mod_l = mod[layer, :bsz].reshape(bsz, N_MOD, 1, d)
        mod_c = jnp.broadcast_to(mod[layer, bsz].reshape(1, N_MOD, 1, d), (bsz, N_MOD, 1, d))
        sh_a, sc_a, g_a, sh_f, sc_f, g_f = (mod_l[:, i] for i in range(N_MOD))
        csh_a, csc_a, cg_a, csh_f, csc_f, cg_f = (mod_c[:, i] for i in range(N_MOD))

        w_merged = _merge_w_in(w_in[layer])
        in_p = (mix_norm_pre[layer], w_merged, ssd_conv_w[layer], ssd_conv_b[layer])
        zx_l, dt_l, q_l, k_l, v_l, u_l, u16_l = _in_proj(x, sh_a, sc_a, *in_p, rope_tabs)
        zx_c, dt_c, q_c, k_c, v_c, u_c, u16_c = _in_proj(ctx, csh_a, csc_a, *in_p, None)

        a_l, a_c = _ssd(zx_l, dt_l, zx_c, dt_c, ssd_dt_bias[layer], ssd_a_log[layer], ssd_d[layer],
                        ssd_norm_w[layer], need_ctx)

        lam_params = (diff_lam_q1[layer], diff_lam_k1[layer], diff_lam_q2[layer], diff_lam_k2[layer])
        b_l = _attention(q_l, (k_c, k_l), (v_c, v_l), lam_params, diff_subln_w[layer], lam_init)

        mats = _s5_matrices(s5_lam_re[layer], s5_lam_im[layer], s5_log_step[layer], s5_b_re[layer],
                            s5_b_im[layer], s5_c_re[layer], s5_c_im[layer])
        ys_l, ys_c = _s5(u16_l, u16_c, mats)

        mix_p = (s5_d[layer], s5_glu_w[layer], s5_glu_b[layer], w_out[layer], mix_norm_post[layer])
        ffn_p = (ffn_norm_pre[layer], ffn_norm_post[layer], ffn_w_gate[layer], ffn_w_up[layer],
                 ffn_conv_w[layer], ffn_conv_b[layer], ffn_w_down[layer])
        x = _mix_ffn(a_l, b_l, ys_l, u_l, x, g_a, sh_f, sc_f, g_f, *mix_p, *ffn_p)
        if need_ctx:
            b_c = _attention(q_c, (k_c,), (v_c,), lam_params, diff_subln_w[layer], lam_init)
            ctx = _mix_ffn(a_c, b_c, ys_c, u_c, ctx, cg_a, csh_f, csc_f, cg_f, *mix_p, *ffn_p)
    return x
```

```python
import functools
import math

import numpy as np
import jax
import jax.numpy as jnp
from jax import lax
from jax.experimental import pallas as pl
from jax.experimental.pallas import tpu as pltpu

F32 = jnp.float32
BF16 = jnp.bfloat16
HI = lax.Precision.HIGHEST

D_MODEL = 1024
N_LAYERS = 2
GRID_W = 64
EPS = 1e-6
N_MOD = 6
SSD_HEADS = 4
SSD_HEAD_DIM = 64
SSD_WIDTH = 256
SSD_GROUPS = 2
SSD_STATE = 64
SSD_CONV = 5
SSD_CHUNK = 128
SSD_ROWS = 64
CONV_ROWS = 64
SSD_XBC = 512
SSD_IN = 776
DIFF_HEADS = 4
DIFF_HEAD_DIM = 64
DIFF_WIDTH = 512
DIFF_IN = 1536
ROPE_THETA = 10000.0
ATTN_Q_TILE = 2048
ATTN_UNIT = 1024
ATTN_ROW_SPLIT = 128
S5_GROUP = 16
S5_WIDTH = 256
S5_GROUPS = 16
S5_STATE = 64
S5_T = 16
FFN_DIM = 2816
FFN_CHUNK = 256
FFN_ROW_TILE = 1024
HALO_ROWS = 16
FFN_CONV = 3

LANES = 128
SUBLANES = 8
W_IN_PAD = 2688
VMEM_LIMIT = 56 * 1024 * 1024


def _cparams(n_axes):
    return pltpu.CompilerParams(dimension_semantics=("parallel",) * n_axes,
                                vmem_limit_bytes=VMEM_LIMIT)


def _rms(x, w):
    return x * lax.rsqrt(jnp.mean(x * x, axis=-1, keepdims=True) + EPS) * w


def _norm_modulate(x, w, scale, shift):
    return x * lax.rsqrt(jnp.mean(x * x, axis=-1, keepdims=True) + EPS) * (w * (1.0 + scale)) + shift


def _silu(x):
    return x * jax.nn.sigmoid(x)


def _row_tile(n):
    return min(1024, n)


def _mod_kernel(c_ref, w_ref, b_ref, o_ref):
    a = _silu(c_ref[...])
    o_ref[0] = jnp.dot(a, w_ref[0], preferred_element_type=F32, precision=HI) + b_ref[0]


def _modulation(cc, mod_w, mod_b):
    nl, d, n = mod_w.shape
    r = cc.shape[0]
    tn = 512
    return pl.pallas_call(
        _mod_kernel,
        grid=(nl, n // tn),
        in_specs=[pl.BlockSpec((r, d), lambda l, j: (0, 0)),
                  pl.BlockSpec((1, d, tn), lambda l, j: (l, 0, j)),
                  pl.BlockSpec((1, 1, tn), lambda l, j: (l, 0, j))],
        out_specs=pl.BlockSpec((1, r, tn), lambda l, j: (l, 0, j)),
        out_shape=jax.ShapeDtypeStruct((nl, r, n), F32),
        compiler_params=_cparams(2),
        name="modulation",
    )(cc, mod_w, mod_b.reshape(nl, 1, n))


def _in_proj_kernel(*refs, rope):
    (x_ref, xp_ref, xn_ref, sh_ref, sc_ref, nw_ref, w_ref, cw_ref, cb_ref) = refs[:9]
    cos_ref, sin_ref = refs[9:11] if rope else (None, None)
    zx_ref, dt_ref, q_ref, k_ref, v_ref, u_ref, u16_ref = refs[-7:]
    i = pl.program_id(1)
    tm = x_ref.shape[1]
    x_ext = jnp.concatenate([xp_ref[0], x_ref[0], xn_ref[0]], axis=0)
    h_ext = _norm_modulate(x_ext, nw_ref[...], sc_ref[0], sh_ref[0])
    h_ext = jnp.concatenate([jnp.where(i == 0, 0.0, h_ext[0:HALO_ROWS]),
                             h_ext[HALO_ROWS:HALO_ROWS + tm],
                             jnp.where(i == pl.num_programs(1) - 1, 0.0, h_ext[HALO_ROWS + tm:])],
                            axis=0).astype(BF16)
    hb = h_ext[HALO_ROWS:HALO_ROWS + tm]

    def mm(lo, hi):
        return jnp.dot(hb, w_ref[:, lo:hi], preferred_element_type=F32)

    xbc = jnp.dot(h_ext, w_ref[:, SSD_WIDTH:SSD_WIDTH + SSD_XBC], preferred_element_type=F32)
    halo = (SSD_CONV - 1) // 2
    win = CONV_ROWS + 2 * SUBLANES
    for r in range(0, tm, CONV_ROWS):
        lo = HALO_ROWS - SUBLANES + r
        window = xbc[lo:lo + win]
        acc = jnp.broadcast_to(cb_ref[...], (CONV_ROWS, SSD_XBC))
        for k in range(SSD_CONV):
            shift = (halo - k) % win
            tap = window if shift == 0 else pltpu.roll(window, shift, 0)
            acc = acc + cw_ref[k:k + 1, :] * tap[SUBLANES:SUBLANES + CONV_ROWS]
        zx_ref[0, r:r + CONV_ROWS, SSD_WIDTH:] = _silu(acc)

    def rot(p):
        lane = lax.broadcasted_iota(jnp.int32, p.shape, 1)
        partner = jnp.where(lane % 32 < 16, pltpu.roll(p, LANES - 16, 1), pltpu.roll(p, 16, 1))
        return p * cos_ref[...] + partner * sin_ref[...]

    o_zx = SSD_WIDTH + SSD_XBC
    zx_ref[0, :, 0:SSD_WIDTH] = mm(0, SSD_WIDTH)
    q_all = mm(o_zx, o_zx + DIFF_WIDTH)
    k_all = mm(o_zx + DIFF_WIDTH, o_zx + 2 * DIFF_WIDTH)
    for j in range(DIFF_WIDTH // LANES):
        q = q_all[:, j * LANES:(j + 1) * LANES] * (DIFF_HEAD_DIM ** -0.5)
        k = k_all[:, j * LANES:(j + 1) * LANES]
        if rope:
            q, k = rot(q), rot(k)
        q_ref[0, :, j * LANES:(j + 1) * LANES] = q.astype(BF16)
        k_ref[0, j * LANES:(j + 1) * LANES, :] = k.T.astype(BF16)
    o_v = o_zx + 2 * DIFF_WIDTH
    v_ref[0] = mm(o_v, o_v + DIFF_WIDTH).astype(BF16)
    o_u = o_v + DIFF_WIDTH
    u_dt = mm(o_u, W_IN_PAD)
    u = u_dt[:, 0:S5_WIDTH]
    u_ref[0] = u
    u16_ref[0] = u.astype(BF16)
    dt_ref[0] = u_dt[:, S5_WIDTH:]


def _in_proj(x, shift, scale, norm_w, w_merged, conv_w, conv_b, rope_tabs):
    bsz, n, d = x.shape
    tm = _row_tile(n)
    nbh = n // HALO_ROWS
    tbh = tm // HALO_ROWS
    rope = rope_tabs is not None
    row = lambda b, i: (b, i, 0)
    per_b = lambda b, i: (b, 0, 0)
    const = lambda b, i: (0, 0)
    in_specs = [pl.BlockSpec((1, tm, d), row),
                pl.BlockSpec((1, HALO_ROWS, d), lambda b, i: (b, jnp.maximum(i * tbh - 1, 0), 0)),
                pl.BlockSpec((1, HALO_ROWS, d), lambda b, i: (b, jnp.minimum((i + 1) * tbh, nbh - 1), 0)),
                pl.BlockSpec((1, 1, d), per_b),
                pl.BlockSpec((1, 1, d), per_b),
                pl.BlockSpec((1, d), const),
                pl.BlockSpec((d, W_IN_PAD), const, pipeline_mode=pl.Buffered(1)),
                pl.BlockSpec((SSD_CONV, SSD_XBC), const),
                pl.BlockSpec((1, SSD_XBC), const)]
    args = [x, x, x, shift, scale, norm_w.reshape(1, d), w_merged, conv_w, conv_b.reshape(1, -1)]
    if rope:
        in_specs += [pl.BlockSpec((tm, LANES), lambda b, i: (i, 0))] * 2
        args += list(rope_tabs)
    widths = (SSD_WIDTH + SSD_XBC, LANES, DIFF_WIDTH, DIFF_WIDTH, DIFF_WIDTH, S5_WIDTH, S5_WIDTH)
    dtypes = (F32, F32, BF16, BF16, BF16, F32, BF16)
    out_specs = [pl.BlockSpec((1, tm, w), row) for w in widths]
    out_shape = [jax.ShapeDtypeStruct((bsz, n, w), t) for w, t in zip(widths, dtypes)]
    out_specs[3] = pl.BlockSpec((1, DIFF_WIDTH, tm), lambda b, i: (b, 0, i))
    out_shape[3] = jax.ShapeDtypeStruct((bsz, DIFF_WIDTH, n), BF16)
    return pl.pallas_call(
        functools.partial(_in_proj_kernel, rope=rope),
        grid=(bsz, n // tm),
        in_specs=in_specs,
        out_specs=out_specs,
        out_shape=out_shape,
        compiler_params=_cparams(2),
        name="in_proj_rope" if rope else "in_proj",
    )(*args)


def _merge_w_in(w):
    o1, o2 = SSD_IN, SSD_IN + DIFF_IN
    parts = [w[:, :SSD_WIDTH + SSD_XBC], w[:, o1:o2], w[:, o2:], w[:, SSD_WIDTH + SSD_XBC:o1]]
    m = jnp.concatenate(parts, axis=1)
    return jnp.pad(m, ((0, 0), (0, W_IN_PAD - m.shape[1]))).astype(BF16)


def _rope_tables(n):
    t = np.arange(n)
    pos = np.stack([t // GRID_W, t % GRID_W], axis=1).astype(np.float32)
    quarter = DIFF_HEAD_DIM // 4
    inv_freq = (np.float32(ROPE_THETA) ** (-np.arange(quarter, dtype=np.float32) / np.float32(quarter)))
    lane = np.arange(LANES)
    axis = (lane % DIFF_HEAD_DIM) // (DIFF_HEAD_DIM // 2)
    freq = lane % quarter
    ang = (pos[:, axis] * inv_freq.astype(np.float32)[freq][None, :]).astype(np.float32)
    sign = np.where(lane % (2 * quarter) < quarter, -1.0, 1.0).astype(np.float32)
    return jnp.asarray(np.cos(ang), F32), jnp.asarray(np.sin(ang) * sign[None, :], F32)


def _cumsum_rows(x, reverse):
    n = x.shape[0]
    row = lax.broadcasted_iota(jnp.int32, x.shape, 0)
    k = 1
    while k < n:
        if reverse:
            x = x + jnp.where(row < n - k, pltpu.roll(x, n - k, 0), 0.0)
        else:
            x = x + jnp.where(row >= k, pltpu.roll(x, k, 0), 0.0)
        k *= 2
    return x


def _ssd_chunk(xact_ref, dtt_ref, dta_ref, exp_ref, state_ref, y_ref, c, direction):
    q = SSD_CHUNK
    nh = SSD_HEADS
    r0 = pl.multiple_of(c * q, q)
    xa = xact_ref[pl.ds(r0, q), :]
    dta = dta_ref[pl.ds(r0, q), :]
    dt_t = dtt_ref[c]
    rows = lax.broadcasted_iota(jnp.int32, (q, q), 0)
    cols = lax.broadcasted_iota(jnp.int32, (q, q), 1)
    lane = cols
    mask = (rows >= cols) if direction == 0 else (rows <= cols)
    acs = _cumsum_rows(dta, reverse=direction == 1)
    acs_t = acs.T
    hi = acs.astype(BF16).astype(F32)
    r1 = acs - hi
    mid = r1.astype(BF16).astype(F32)
    lo = (r1 - mid).astype(BF16).astype(F32)
    parts = hi + pltpu.roll(mid, 8, 1) + pltpu.roll(lo, 16, 1)
    neg = -pltpu.roll(parts, 24, 1)
    lhs = jnp.where(lane < 24, parts, jnp.where(lane < 48, 1.0, 0.0)).astype(BF16)
    rhs = []
    for h in range(nh):
        j = direction * nh + h
        is_one = (lane == j) | (lane == 8 + j) | (lane == 16 + j)
        is_neg = (lane == 24 + j) | (lane == 32 + j) | (lane == 40 + j)
        rhs.append(jnp.where(is_one, 1.0, jnp.where(is_neg, neg, 0.0)).astype(BF16))
    diff = lax.dot_general(lhs, jnp.concatenate(rhs, axis=0), (((1,), (1,)), ((), ())),
                           preferred_element_type=F32)
    mask4 = jnp.concatenate([mask] * nh, axis=1)
    decay = jnp.exp(jnp.where(mask4, diff, -jnp.inf))
    acs_x = jnp.dot(parts.astype(BF16), exp_ref[direction], preferred_element_type=F32)
    last = q - 1 if direction == 0 else 0
    total_x = acs_x[last:last + 1, :]

    xb = xa[:, 0:SSD_WIDTH].astype(BF16)
    bm = xa[:, SSD_WIDTH:SSD_WIDTH + LANES]
    cm = xa[:, SSD_WIDTH + LANES:]
    bm_t = bm.T
    state = state_ref[direction]
    y_off = jnp.dot(cm.astype(BF16), state.astype(BF16), preferred_element_type=F32) * jnp.exp(acs_x)
    row_grp = lax.broadcasted_iota(jnp.int32, (LANES, SSD_WIDTH), 0) // SSD_STATE
    lane_head = lax.broadcasted_iota(jnp.int32, (LANES, SSD_WIDTH), 1) // SSD_HEAD_DIM
    x_head = lax.broadcasted_iota(jnp.int32, (q, SSD_WIDTH), 1) // SSD_HEAD_DIM
    b_parts, x_parts = [], []
    bm_sel = jnp.concatenate([jnp.where(lane // SSD_STATE == g, bm, 0.0) for g in range(SSD_GROUPS)],
                             axis=0).astype(BF16)
    cb_all = lax.dot_general(cm.astype(BF16), bm_sel, (((1,), (1,)), ((), ())),
                             preferred_element_type=F32)
    y_diag = []
    for g in range(SSD_GROUPS):
        cb = cb_all[:, g * q:(g + 1) * q]
        xg = xb[:, g * LANES:(g + 1) * LANES]
        pair = []
        for h in range(g * 2, g * 2 + 2):
            j = direction * nh + h
            row_dt = dt_t[j:j + 1, :]
            pair.append((cb * decay[:, h * q:(h + 1) * q] * row_dt).astype(BF16))
            w_row = row_dt * jnp.exp(acs_t[j:j + 1, last:last + 1] - acs_t[j:j + 1, :])
            b_parts.append((bm_t * w_row).astype(BF16))
            x_parts.append(jnp.where(x_head == h, xb, jnp.zeros_like(xb)))
        x_sel = jnp.concatenate([jnp.where(lane < SSD_HEAD_DIM, xg, jnp.zeros_like(xg)),
                                 jnp.where(lane >= SSD_HEAD_DIM, xg, jnp.zeros_like(xg))], axis=0)
        y_diag.append(jnp.dot(jnp.concatenate(pair, axis=1), x_sel, preferred_element_type=F32))
    y_ref[pl.ds(r0, q), :] += jnp.concatenate(y_diag, axis=-1) + y_off
    upd = jnp.dot(jnp.concatenate(b_parts, axis=1), jnp.concatenate(x_parts, axis=0),
                  preferred_element_type=F32)
    upd = jnp.where(row_grp == lane_head // (nh // SSD_GROUPS), upd, 0.0)
    state_ref[direction] = jnp.exp(total_x) * state + upd


def _ssd_kernel(*refs, n_ctx, n_lat, need_ctx):
    (zx_l, dt_l, zx_c, dt_c, dtb_ref, alog_ref, d_ref, nw_ref, exp_ref) = refs[:9]
    outs = refs[9:11] if need_ctx else (refs[9], None)
    xact_ref, dtt_ref, dta_ref, y_ref, state_ref = refs[-5:]
    out_l, out_c = outs

    for seg_ref, n, base in ((zx_c, n_ctx, 0), (zx_l, n_lat, n_ctx)):
        def copy_rows(i, carry, seg_ref=seg_ref, base=base):
            r = pl.multiple_of(i * SSD_CHUNK, SSD_CHUNK)
            act = seg_ref[0, pl.ds(r, SSD_CHUNK), SSD_WIDTH:]
            xact_ref[pl.ds(base + r, SSD_CHUNK), :] = act
            y_ref[pl.ds(base + r, SSD_CHUNK), :] = act[:, 0:SSD_WIDTH] * d_ref[...]
            return carry

        lax.fori_loop(0, n // SSD_CHUNK, copy_rows, 0)
    a_neg = -jnp.exp(alog_ref[...])
    head_lane = lax.broadcasted_iota(jnp.int32, (SSD_CHUNK, LANES), 1) < 2 * SSD_HEADS
    for seg_ref, n, base in ((dt_c, n_ctx, 0), (dt_l, n_lat, n_ctx)):
        for t0 in range(0, n, SSD_CHUNK):
            dt = jax.nn.softplus(seg_ref[0, t0:t0 + SSD_CHUNK, :] + dtb_ref[...])
            dtt_ref[(base + t0) // SSD_CHUNK] = dt.T
            dta_ref[base + t0:base + t0 + SSD_CHUNK, :] = jnp.where(head_lane, dt * a_neg, 0.0)
    n_tot = n_ctx + n_lat
    state_ref[...] = jnp.zeros(state_ref.shape, F32)

    ncc, nct = n_ctx // SSD_CHUNK, n_tot // SSD_CHUNK

    def step(i, carry):
        _ssd_chunk(xact_ref, dtt_ref, dta_ref, exp_ref, state_ref, y_ref, i, 0)
        cb_idx = jnp.where(i < ncc, ncc - 1 - i, nct - 1 - (i - ncc))
        _ssd_chunk(xact_ref, dtt_ref, dta_ref, exp_ref, state_ref, y_ref, cb_idx, 1)
        return carry

    lax.fori_loop(0, nct, step, 0, unroll=3)

    for seg_ref, o_ref, n, base in ((zx_c, out_c, n_ctx, 0), (zx_l, out_l, n_lat, n_ctx)):
        if o_ref is None:
            continue
        def gate_rows(i, carry, seg_ref=seg_ref, o_ref=o_ref, base=base):
            r = pl.multiple_of(i * SSD_ROWS, SSD_ROWS)
            z = seg_ref[0, pl.ds(r, SSD_ROWS), 0:SSD_WIDTH]
            v = y_ref[pl.ds(base + r, SSD_ROWS), :] * _silu(z)
            o_ref[0, pl.ds(r, SSD_ROWS), :] = _rms(v, nw_ref[...]).astype(BF16)
            return carry

        lax.fori_loop(0, n // SSD_ROWS, gate_rows, 0, unroll=4)


def _ssd(zx_l, dt_l, zx_c, dt_c, dt_bias, a_log, d_skip, norm_w, need_ctx):
    bsz, n_lat, wzx = zx_l.shape
    n_ctx = zx_c.shape[1]
    n_tot = n_ctx + n_lat
    pad8 = lambda v: jnp.pad(v.reshape(1, -1), ((0, 0), (0, LANES - v.size)))
    per_b = lambda b: (b, 0, 0)
    const = lambda b: (0, 0)
    k_idx = np.arange(LANES)[None, :, None]
    d_idx = np.arange(2)[:, None, None]
    h_idx = (np.arange(SSD_WIDTH) // SSD_HEAD_DIM)[None, None, :]
    expand = jnp.asarray((k_idx < 24) & (k_idx % 8 == d_idx * SSD_HEADS + h_idx), BF16)
    params = [pad8(dt_bias), pad8(a_log),
              jnp.repeat(d_skip, SSD_HEAD_DIM).reshape(1, -1), norm_w.reshape(1, -1)]
    in_specs = [pl.BlockSpec((1, n_lat, wzx), per_b), pl.BlockSpec((1, n_lat, LANES), per_b),
                pl.BlockSpec((1, n_ctx, wzx), per_b), pl.BlockSpec((1, n_ctx, LANES), per_b)]
    in_specs += [pl.BlockSpec(p.shape, const) for p in params]
    in_specs += [pl.BlockSpec(expand.shape, lambda b: (0, 0, 0))]
    out_specs = [pl.BlockSpec((1, n_lat, SSD_WIDTH), per_b)]
    out_shape = [jax.ShapeDtypeStruct((bsz, n_lat, SSD_WIDTH), BF16)]
    if need_ctx:
        out_specs.append(pl.BlockSpec((1, n_ctx, SSD_WIDTH), per_b))
        out_shape.append(jax.ShapeDtypeStruct((bsz, n_ctx, SSD_WIDTH), BF16))
    scratch = [pltpu.VMEM((n_tot, SSD_XBC), F32),
               pltpu.VMEM((n_tot // SSD_CHUNK, LANES, SSD_CHUNK), F32),
               pltpu.VMEM((n_tot, LANES), F32),
               pltpu.VMEM((n_tot, SSD_WIDTH), F32),
               pltpu.VMEM((2, SSD_GROUPS * SSD_STATE, SSD_WIDTH), F32)]
    outs = pl.pallas_call(
        functools.partial(_ssd_kernel, n_ctx=n_ctx, n_lat=n_lat, need_ctx=need_ctx),
        grid=(bsz,),
        in_specs=in_specs,
        out_specs=out_specs,
        out_shape=out_shape,
        scratch_shapes=scratch,
        compiler_params=_cparams(1),
        name="ssd_mixer",
    )(zx_l, dt_l, zx_c, dt_c, *params, expand)
    return (outs[0], outs[1]) if need_ctx else (outs[0], None)


def _attn_kernel(*refs, n_kv, lam_init):
    q_ref = refs[0]
    k_refs = refs[1:1 + n_kv]
    v_refs = refs[1 + n_kv:1 + 2 * n_kv]
    lq1, lk1, lq2, lk2, sw_ref, o_ref = refs[1 + 2 * n_kv:-4]
    s_refs, e_refs = refs[-4:-2], refs[-2:]
    tq, hw = q_ref.shape[1:]
    unit = s_refs[0].shape[0]
    lane = lax.broadcasted_iota(jnp.int32, (unit, hw), 1)
    lam = (jnp.exp(jnp.sum(lq1[...] * lk1[...], axis=-1, keepdims=True))
           - jnp.exp(jnp.sum(lq2[...] * lk2[...], axis=-1, keepdims=True)) + lam_init)
    offs = np.cumsum([0] + [k_ref.shape[2] for k_ref in k_refs])
    v_ext = [jnp.concatenate([v_ref[0], jnp.ones(v_ref.shape[1:], BF16)], axis=-1) for v_ref in v_refs]
    rows = max(ATTN_ROW_SPLIT, unit // 4)
    pieces = [(lo, min(lo + rows, unit)) for lo in range(0, unit, rows)]
    for base in range(0, tq, unit):
        q = q_ref[0, base:base + unit, :]
        for comp, s_ref in enumerate(s_refs):
            own = (lane >= DIFF_HEAD_DIM) if comp else (lane < DIFF_HEAD_DIM)
            qc = jnp.where(own, q, jnp.zeros_like(q))
            for i, k_ref in enumerate(k_refs):
                s_ref[:, offs[i]:offs[i + 1]] = jnp.dot(qc, k_ref[0], preferred_element_type=F32)
        outs = []
        for s_ref, e_ref in zip(s_refs, e_refs):
            parts = []
            for lo, hi in pieces:
                s = s_ref[lo:hi, :]
                e_ref[lo:hi, :] = jnp.exp(s - jnp.max(s, axis=-1, keepdims=True)).astype(BF16)
                ov = None
                for i in range(n_kv):
                    part = jnp.dot(e_ref[lo:hi, offs[i]:offs[i + 1]], v_ext[i], preferred_element_type=F32)
                    ov = part if ov is None else ov + part
                parts.append(ov[:, 0:hw] * (1.0 / ov[:, hw:hw + 1]))
            outs.append(parts)
        for (lo, hi), o0, o1 in zip(pieces, *outs):
            o = o0 - lam * o1
            o_ref[0, base + lo:base + hi, :] = (_rms(o, sw_ref[...]) * (1.0 - lam_init)).astype(BF16)


def _attention(q, ks, vs, lam_params, subln_w, lam_init):
    bsz, n, _ = q.shape
    tq = min(ATTN_Q_TILE, n)
    n_kv = len(ks)
    n_keys = sum(a.shape[2] for a in ks)
    hw = 2 * DIFF_HEAD_DIM
    in_specs = [pl.BlockSpec((1, tq, hw), lambda b, h, i: (b, i, h))]
    in_specs += [pl.BlockSpec((1, hw, a.shape[2]), lambda b, h, i: (b, h, 0)) for a in ks]
    in_specs += [pl.BlockSpec((1, a.shape[1], hw), lambda b, h, i: (b, 0, h)) for a in vs]
    in_specs += [pl.BlockSpec((1, DIFF_HEAD_DIM), lambda b, h, i: (0, 0))] * 4
    in_specs += [pl.BlockSpec((1, hw), lambda b, h, i: (0, 0))]
    return pl.pallas_call(
        functools.partial(_attn_kernel, n_kv=n_kv, lam_init=lam_init),
        grid=(bsz, DIFF_HEADS, n // tq),
        in_specs=in_specs,
        out_specs=pl.BlockSpec((1, tq, hw), lambda b, h, i: (b, i, h)),
        out_shape=jax.ShapeDtypeStruct((bsz, n, DIFF_WIDTH), BF16),
        scratch_shapes=([pltpu.VMEM((min(ATTN_UNIT, tq), n_keys), F32)] * 2
                        + [pltpu.VMEM((min(ATTN_UNIT, tq), n_keys), BF16)] * 2),
        compiler_params=_cparams(3),
        name="diff_attention",
    )(q, *ks, *vs, *[p.reshape(1, -1) for p in lam_params], subln_w.reshape(1, -1))


def _s5_matrices(lam_re, lam_im, log_step, b_re, b_im, c_re, c_im):
    t = S5_T
    step = jnp.exp(log_step.astype(F32))[..., None]
    lr, li = lam_re.astype(F32), lam_im.astype(F32)
    mag = jnp.exp(lr * step)
    a = (mag * jnp.cos(li * step)) + 1j * (mag * jnp.sin(li * step))
    lam = lr + 1j * li
    bb = ((a - 1.0) / lam)[..., None] * (b_re.astype(F32) + 1j * b_im.astype(F32))
    cc = c_re.astype(F32) + 1j * c_im.astype(F32)
    pw = [jnp.ones_like(a)]
    for _ in range(t):
        pw.append(pw[-1] * a)
    pw = jnp.stack(pw, axis=0)
    kern = jnp.real(jnp.einsum('dgcn,tdgn,dgnk->dgktc', cc, pw[:t], bb))
    gc = S5_GROUP
    fwd = kern[0].reshape(S5_GROUPS, gc, t * gc)
    bwd = kern[1][:, :, ::-1].reshape(S5_GROUPS, gc, t * gc)
    band = jnp.concatenate([bwd[..., :(t - 1) * gc], bwd[..., (t - 1) * gc:] + fwd[..., :gc],
                            fwd[..., gc:]], axis=-1)
    k_full = jnp.concatenate([band[..., (t - 1 - s) * gc:(2 * t - 1 - s) * gc] for s in range(t)],
                             axis=1)
    wf = pw[t - 1 - np.arange(t), 0][..., None] * bb[0][None]
    wb = pw[np.arange(t), 1][..., None] * bb[1][None]
    def rows_sc(w):
        return w.transpose(1, 0, 3, 2).reshape(S5_GROUPS, t * S5_GROUP, S5_STATE)
    w_state = jnp.concatenate([rows_sc(jnp.real(wf)), rows_sc(jnp.real(wb)),
                               rows_sc(jnp.imag(wf)), rows_sc(jnp.imag(wb))], axis=-1)
    cf = cc[0][None] * pw[1 + np.arange(t), 0][:, :, None, :]
    cbk = cc[1][None] * pw[t - np.arange(t), 1][:, :, None, :]
    def cols_tc(w):
        return w.transpose(1, 3, 0, 2).reshape(S5_GROUPS, S5_STATE, t * S5_GROUP)
    c_off = jnp.concatenate([cols_tc(jnp.real(cf)), cols_tc(jnp.real(cbk)),
                             cols_tc(-jnp.imag(cf)), cols_tc(-jnp.imag(cbk))], axis=1)
    at = pw[t]
    a_rows = jnp.stack([jnp.concatenate([jnp.real(at[0]), jnp.real(at[1])], axis=-1),
                        jnp.concatenate([jnp.imag(at[0]), jnp.imag(at[1])], axis=-1)], axis=1)
    a_chunk = jnp.pad(a_rows, ((0, 0), (0, SUBLANES - 2), (0, 0)))
    return k_full.astype(BF16), w_state.astype(BF16), c_off.astype(BF16), a_chunk.astype(F32)


def _s5_kernel(uc_ref, ul_ref, kf_ref, ws_ref, co_ref, a_ref, yc_ref, yl_ref, e_ref, h_ref):
    ncc, bsz, w = uc_ref.shape[1:]
    ncl = ul_ref.shape[1]
    nct = ncc + ncl
    uc = uc_ref[0].reshape(ncc * bsz, w)
    ul = ul_ref[0].reshape(ncl * bsz, w)
    e_ref[0:ncc] = jnp.dot(uc, ws_ref[0], preferred_element_type=F32).reshape(ncc, bsz, w)
    e_ref[ncc:nct] = jnp.dot(ul, ws_ref[0], preferred_element_type=F32).reshape(ncl, bsz, w)
    ar = a_ref[0, 0:1, :]
    ai = a_ref[0, 1:2, :]
    half = S5_STATE
    lane = lax.broadcasted_iota(jnp.int32, (bsz, LANES), 1)
    is_f = lane < half

    def step(i, carry):
        h_re, h_im = carry
        jf = i
        jb = jnp.where(i < ncc, ncc - 1 - i, nct - 1 - (i - ncc))
        h_ref[jf, :, 0:half] = h_re[:, 0:half]
        h_ref[jb, :, half:LANES] = h_re[:, half:]
        h_ref[jf, :, LANES:LANES + half] = h_im[:, 0:half]
        h_ref[jb, :, LANES + half:] = h_im[:, half:]
        e_re = jnp.where(is_f, e_ref[jf, :, 0:LANES], e_ref[jb, :, 0:LANES])
        e_im = jnp.where(is_f, e_ref[jf, :, LANES:], e_ref[jb, :, LANES:])
        return (ar * h_re - ai * h_im + e_re, ar * h_im + ai * h_re + e_im)

    zero = jnp.zeros((bsz, LANES), F32)
    lax.fori_loop(0, nct, step, (zero, zero))
    for u, y_ref, lo, nc in ((uc, yc_ref, 0, ncc), (ul, yl_ref, ncc, ncl)):
        hs = h_ref[lo:lo + nc].reshape(nc * bsz, w).astype(BF16)
        y = (jnp.dot(u, kf_ref[0], preferred_element_type=F32)
             + jnp.dot(hs, co_ref[0], preferred_element_type=F32))
        y_ref[0] = y.reshape(nc, bsz, w).astype(BF16)


def _s5_to_chunks(u):
    bsz, n, _ = u.shape
    u = u.reshape(bsz, n // S5_T, S5_T, S5_GROUPS, S5_GROUP).transpose(3, 1, 0, 2, 4)
    return u.reshape(S5_GROUPS, n // S5_T, bsz, S5_T * S5_GROUP)


def _s5_from_chunks(y, bsz):
    nchunk = y.shape[1]
    y = y.reshape(S5_GROUPS, nchunk, bsz, S5_T, S5_GROUP).transpose(2, 1, 3, 0, 4)
    return y.reshape(bsz, nchunk * S5_T, S5_WIDTH)


def _s5(u_l, u_c, mats):
    k_full, w_state, c_off, a_chunk = mats
    bsz = u_l.shape[0]
    uc, ul = _s5_to_chunks(u_c), _s5_to_chunks(u_l)
    ncc, ncl, w = uc.shape[1], ul.shape[1], ul.shape[3]
    nct = ncc + ncl
    blk = lambda g: (g, 0, 0, 0)
    mat = lambda g: (g, 0, 0)
    y_c, y_l = pl.pallas_call(
        _s5_kernel,
        grid=(S5_GROUPS,),
        in_specs=[pl.BlockSpec((1, ncc, bsz, w), blk), pl.BlockSpec((1, ncl, bsz, w), blk),
                  pl.BlockSpec((1, w, w), mat), pl.BlockSpec((1, w, w), mat),
                  pl.BlockSpec((1, w, w), mat), pl.BlockSpec((1, SUBLANES, LANES), mat)],
        out_specs=[pl.BlockSpec((1, ncc, bsz, w), blk), pl.BlockSpec((1, ncl, bsz, w), blk)],
        out_shape=[jax.ShapeDtypeStruct((S5_GROUPS, ncc, bsz, w), BF16),
                   jax.ShapeDtypeStruct((S5_GROUPS, ncl, bsz, w), BF16)],
        scratch_shapes=[pltpu.VMEM((nct, bsz, w), F32), pltpu.VMEM((nct, bsz, w), F32)],
        compiler_params=_cparams(1),
        name="s5_mixer",
    )(uc, ul, k_full, w_state, c_off, a_chunk)
    return _s5_from_chunks(y_l, bsz), _s5_from_chunks(y_c, bsz)


def _mix_ffn_kernel(*refs):
    rows = [refs[3 * j:3 * j + 3] for j in range(5)]
    (ga_ref, sh_ref, sc_ref, gf_ref, d_ref, gw_ref, gb_ref, wo_ref, nwm_ref, nw1_ref, nw2_ref,
     wg_ref, wu_ref, cw_ref, cb_ref, wd_ref, o_ref, act_ref) = refs[15:]
    i = pl.program_id(1)
    is_first = i == 0
    is_last = i == pl.num_programs(1) - 1
    tm = o_ref.shape[1]
    a, b, ys, u, x_ext = (jnp.concatenate([r[0] for r in trio], axis=0) for trio in rows)
    y = jax.nn.gelu(ys + d_ref[...] * u, approximate=True)
    s_gate = jnp.dot(y.astype(BF16), gw_ref[...], preferred_element_type=F32) + gb_ref[...]
    s = (y * jax.nn.sigmoid(s_gate)).astype(BF16)
    mix = jnp.dot(jnp.concatenate([a, b, s], axis=-1), wo_ref[...], preferred_element_type=F32)
    x1_ext = x_ext + ga_ref[0] * _rms(mix, nwm_ref[...])
    x1 = x1_ext[0:tm]
    h_ext = _norm_modulate(x1_ext, nw1_ref[...], sc_ref[0], sh_ref[0])
    h_ext = jnp.concatenate([h_ext[0:tm],
                             jnp.where(is_first, 0.0, h_ext[tm:tm + HALO_ROWS]),
                             jnp.where(is_last, 0.0, h_ext[tm + HALO_ROWS:])], axis=0).astype(BF16)
    h = h_ext[0:tm]
    row = lax.broadcasted_iota(jnp.int32, (tm, 1), 0)
    prev_row, next_row = tm + HALO_ROWS - 1, tm + HALO_ROWS
    for lo in range(0, wg_ref.shape[1], FFN_CHUNK):
        hi = lo + FFN_CHUNK
        g_ext = jnp.dot(h_ext, wg_ref[:, lo:hi], preferred_element_type=F32)
        g = g_ext[0:tm]
        g_prev = jnp.where(row == 0, g_ext[prev_row:prev_row + 1], pltpu.roll(g, 1, 0))
        g_next = jnp.where(row == tm - 1, g_ext[next_row:next_row + 1], pltpu.roll(g, tm - 1, 0))
        conv = (cw_ref[0:1, lo:hi] * g_prev + cw_ref[1:2, lo:hi] * g + cw_ref[2:3, lo:hi] * g_next
                + cb_ref[:, lo:hi])
        up = jnp.dot(h, wu_ref[:, lo:hi], preferred_element_type=F32)
        act_ref[:, lo:hi] = (_silu(conv) * up).astype(BF16)
    f = jnp.dot(act_ref[...], wd_ref[...], preferred_element_type=F32)
    o_ref[0] = x1 + gf_ref[0] * _rms(f, nw2_ref[...])


def _mix_ffn(a, b, ys, u, x, g_a, shift, scale, g_f, s5_d, glu_w, glu_b, w_out, norm_mix,
             norm_pre, norm_post, w_gate, w_up, conv_w, conv_b, w_down):
    bsz, n, d = x.shape
    tm = min(FFN_ROW_TILE, n)
    f = w_gate.shape[1]
    nbh = n // HALO_ROWS
    tbh = tm // HALO_ROWS
    per_b = lambda bi, i: (bi, 0, 0)
    const = lambda bi, i: (0, 0)
    once = dict(pipeline_mode=pl.Buffered(1))

    def row_specs(w):
        return [pl.BlockSpec((1, tm, w), lambda bi, i: (bi, i, 0)),
                pl.BlockSpec((1, HALO_ROWS, w), lambda bi, i: (bi, jnp.maximum(i * tbh - 1, 0), 0)),
                pl.BlockSpec((1, HALO_ROWS, w), lambda bi, i: (bi, jnp.minimum((i + 1) * tbh, nbh - 1), 0))]

    row_args, in_specs = [], []
    for arr in (a, b, ys, u, x):
        row_args += [arr] * 3
        in_specs += row_specs(arr.shape[2])
    in_specs += [pl.BlockSpec((1, 1, d), per_b)] * 4
    in_specs += [pl.BlockSpec((1, S5_WIDTH), const), pl.BlockSpec((S5_WIDTH, S5_WIDTH), const),
                 pl.BlockSpec((1, S5_WIDTH), const), pl.BlockSpec((d, d), const, **once),
                 pl.BlockSpec((1, d), const), pl.BlockSpec((1, d), const), pl.BlockSpec((1, d), const),
                 pl.BlockSpec((d, f), const, **once), pl.BlockSpec((d, f), const, **once),
                 pl.BlockSpec((FFN_CONV, f), const), pl.BlockSpec((1, f), const),
                 pl.BlockSpec((f, d), const, **once)]
    return pl.pallas_call(
        _mix_ffn_kernel,
        grid=(bsz, n // tm),
        in_specs=in_specs,
        out_specs=pl.BlockSpec((1, tm, d), lambda bi, i: (bi, i, 0)),
        out_shape=jax.ShapeDtypeStruct((bsz, n, d), F32),
        scratch_shapes=[pltpu.VMEM((tm, f), BF16)],
        compiler_params=_cparams(2),
        name="mix_ffn",
    )(*row_args, g_a, shift, scale, g_f, s5_d.reshape(1, -1), glu_w.astype(BF16), glu_b.reshape(1, -1),
      w_out.astype(BF16), norm_mix.reshape(1, -1), norm_pre.reshape(1, -1), norm_post.reshape(1, -1),
      w_gate.astype(BF16), w_up.astype(BF16), conv_w, conv_b.reshape(1, -1), w_down.astype(BF16))


def kernel(x, c, ctx, c_ctx, mod_w, mod_b, mix_norm_pre, mix_norm_post, ffn_norm_pre, ffn_norm_post, w_in, w_out, ssd_conv_w, ssd_conv_b, ssd_dt_bias, ssd_a_log, ssd_d, ssd_norm_w, diff_lam_q1, diff_lam_k1, diff_lam_q2, diff_lam_k2, diff_subln_w, s5_lam_re, s5_lam_im, s5_log_step, s5_b_re, s5_b_im, s5_c_re, s5_c_im, s5_d, s5_glu_w, s5_glu_b, ffn_w_gate, ffn_w_up, ffn_conv_w, ffn_conv_b, ffn_w_down):
    bsz, n_lat, d = x.shape
    n_layers = mod_w.shape[0]
    rope_tabs = _rope_tables(n_lat)
    cc = jnp.concatenate([c, jnp.broadcast_to(c_ctx[None, :], (SUBLANES, d))], axis=0)
    mod = _modulation(cc, mod_w, mod_b)

    for layer in range(n_layers):
        need_ctx = layer < n_layers - 1
        lam_init = 0.8 - 0.6 * math.exp(-0.3 * layer)
        mod_l = mod[layer, :bsz].reshape(bsz, N_MOD, 1, d)
        mod_c = jnp.broadcast_to(mod[layer, bsz].reshape(1, N_MOD, 1, d), (bsz, N_MOD, 1, d))
        sh_a, sc_a, g_a, sh_f, sc_f, g_f = (mod_l[:, i] for i in range(N_MOD))
        csh_a, csc_a, cg_a, csh_f, csc_f, cg_f = (mod_c[:, i] for i in range(N_MOD))

        w_merged = _merge_w_in(w_in[layer])
        in_p = (mix_norm_pre[layer], w_merged, ssd_conv_w[layer], ssd_conv_b[layer])
        zx_l, dt_l, q_l, k_l, v_l, u_l, u16_l = _in_proj(x, sh_a, sc_a, *in_p, rope_tabs)
        zx_c, dt_c, q_c, k_c, v_c, u_c, u16_c = _in_proj(ctx, csh_a, csc_a, *in_p, None)

        a_l, a_c = _ssd(zx_l, dt_l, zx_c, dt_c, ssd_dt_bias[layer], ssd_a_log[layer], ssd_d[layer],
                        ssd_norm_w[layer], need_ctx)

        lam_params = (diff_lam_q1[layer], diff_lam_k1[layer], diff_lam_q2[layer], diff_lam_k2[layer])
        b_l = _attention(q_l, (k_c, k_l), (v_c, v_l), lam_params, diff_subln_w[layer], lam_init)

        mats = _s5_matrices(s5_lam_re[layer], s5_lam_im[layer], s5_log_step[layer], s5_b_re[layer],
                            s5_b_im[layer], s5_c_re[layer], s5_c_im[layer])
        ys_l, ys_c = _s5(u16_l, u16_c, mats)

        mix_p = (s5_d[layer], s5_glu_w[layer], s5_glu_b[layer], w_out[layer], mix_norm_post[layer])
        ffn_p = (ffn_norm_pre[layer], ffn_norm_post[layer], ffn_w_gate[layer], ffn_w_up[layer],
                 ffn_conv_w[layer], ffn_conv_b[layer], ffn_w_down[layer])
        x = _mix_ffn(a_l, b_l, ys_l, u_l, x, g_a, sh_f, sc_f, g_f, *mix_p, *ffn_p)
        if need_ctx:
            b_c = _attention(q_c, (k_c,), (v_c,), lam_params, diff_subln_w[layer], lam_init)
            ctx = _mix_ffn(a_c, b_c, ys_c, u_c, ctx, cg_a, csh_f, csc_f, cg_f, *mix_p, *ffn_p)
    return x
```

```python
import functools
import math

import numpy as np
import jax
import jax.numpy as jnp
from jax import lax
from jax.experimental import pallas as pl
from jax.experimental.pallas import tpu as pltpu

F32 = jnp.float32
BF16 = jnp.bfloat16
HI = lax.Precision.HIGHEST

D_MODEL = 1024
N_LAYERS = 2
GRID_W = 64
EPS = 1e-6
N_MOD = 6
SSD_HEADS = 4
SSD_HEAD_DIM = 64
SSD_WIDTH = 256
SSD_GROUPS = 2
SSD_STATE = 64
SSD_CONV = 5
SSD_CHUNK = 128
SSD_ROWS = 64
CONV_ROWS = 64
SSD_XBC = 512
SSD_IN = 776
DIFF_HEADS = 4
DIFF_HEAD_DIM = 64
DIFF_WIDTH = 512
DIFF_IN = 1536
ROPE_THETA = 10000.0
ATTN_Q_TILE = 2048
ATTN_UNIT = 1024
ATTN_ROW_SPLIT = 128
S5_GROUP = 16
S5_WIDTH = 256
S5_GROUPS = 16
S5_STATE = 64
S5_T = 16
FFN_DIM = 2816
FFN_CHUNK = 256
FFN_ROW_TILE = 1024
HALO_ROWS = 16
FFN_CONV = 3

LANES = 128
SUBLANES = 8
W_IN_PAD = 2688
VMEM_LIMIT = 56 * 1024 * 1024


def _cparams(n_axes):
    return pltpu.CompilerParams(dimension_semantics=("parallel",) * n_axes,
                                vmem_limit_bytes=VMEM_LIMIT)


def _rms(x, w):
    return x * lax.rsqrt(jnp.mean(x * x, axis=-1, keepdims=True) + EPS) * w


def _norm_modulate(x, w, scale, shift):
    return x * lax.rsqrt(jnp.mean(x * x, axis=-1, keepdims=True) + EPS) * (w * (1.0 + scale)) + shift


def _silu(x):
    return x * jax.nn.sigmoid(x)


def _row_tile(n):
    return min(1024, n)


def _mod_kernel(c_ref, w_ref, b_ref, o_ref):
    a = _silu(c_ref[...])
    o_ref[0] = jnp.dot(a, w_ref[0], preferred_element_type=F32, precision=HI) + b_ref[0]


def _modulation(cc, mod_w, mod_b):
    nl, d, n = mod_w.shape
    r = cc.shape[0]
    tn = 512
    return pl.pallas_call(
        _mod_kernel,
        grid=(nl, n // tn),
        in_specs=[pl.BlockSpec((r, d), lambda l, j: (0, 0)),
                  pl.BlockSpec((1, d, tn), lambda l, j: (l, 0, j)),
                  pl.BlockSpec((1, 1, tn), lambda l, j: (l, 0, j))],
        out_specs=pl.BlockSpec((1, r, tn), lambda l, j: (l, 0, j)),
        out_shape=jax.ShapeDtypeStruct((nl, r, n), F32),
        compiler_params=_cparams(2),
        name="modulation",
    )(cc, mod_w, mod_b.reshape(nl, 1, n))


def _in_proj_kernel(*refs, rope):
    (x_ref, xp_ref, xn_ref, sh_ref, sc_ref, nw_ref, w_ref, cw_ref, cb_ref) = refs[:9]
    cos_ref, sin_ref = refs[9:11] if rope else (None, None)
    zx_ref, dt_ref, q_ref, k_ref, v_ref, u_ref, u16_ref = refs[-7:]
    i = pl.program_id(1)
    tm = x_ref.shape[1]
    x_ext = jnp.concatenate([xp_ref[0], x_ref[0], xn_ref[0]], axis=0)
    h_ext = _norm_modulate(x_ext, nw_ref[...], sc_ref[0], sh_ref[0])
    h_ext = jnp.concatenate([jnp.where(i == 0, 0.0, h_ext[0:HALO_ROWS]),
                             h_ext[HALO_ROWS:HALO_ROWS + tm],
                             jnp.where(i == pl.num_programs(1) - 1, 0.0, h_ext[HALO_ROWS + tm:])],
                            axis=0).astype(BF16)
    hb = h_ext[HALO_ROWS:HALO_ROWS + tm]

    def mm(lo, hi):
        return jnp.dot(hb, w_ref[:, lo:hi], preferred_element_type=F32)

    xbc = jnp.dot(h_ext, w_ref[:, SSD_WIDTH:SSD_WIDTH + SSD_XBC], preferred_element_type=F32)
    halo = (SSD_CONV - 1) // 2
    win = CONV_ROWS + 2 * SUBLANES
    for r in range(0, tm, CONV_ROWS):
        lo = HALO_ROWS - SUBLANES + r
        window = xbc[lo:lo + win]
        acc = jnp.broadcast_to(cb_ref[...], (CONV_ROWS, SSD_XBC))
        for k in range(SSD_CONV):
            shift = (halo - k) % win
            tap = window if shift == 0 else pltpu.roll(window, shift, 0)
            acc = acc + cw_ref[k:k + 1, :] * tap[SUBLANES:SUBLANES + CONV_ROWS]
        zx_ref[0, r:r + CONV_ROWS, SSD_WIDTH:] = _silu(acc)

    def rot(p):
        lane = lax.broadcasted_iota(jnp.int32, p.shape, 1)
        partner = jnp.where(lane % 32 < 16, pltpu.roll(p, LANES - 16, 1), pltpu.roll(p, 16, 1))
        return p * cos_ref[...] + partner * sin_ref[...]

    o_zx = SSD_WIDTH + SSD_XBC
    zx_ref[0, :, 0:SSD_WIDTH] = mm(0, SSD_WIDTH)
    q_all = mm(o_zx, o_zx + DIFF_WIDTH)
    k_all = mm(o_zx + DIFF_WIDTH, o_zx + 2 * DIFF_WIDTH)
    for j in range(DIFF_WIDTH // LANES):
        q = q_all[:, j * LANES:(j + 1) * LANES] * (DIFF_HEAD_DIM ** -0.5)
        k = k_all[:, j * LANES:(j + 1) * LANES]
        if rope:
            q, k = rot(q), rot(k)
        q_ref[0, :, j * LANES:(j + 1) * LANES] = q.astype(BF16)
        k_ref[0, j * LANES:(j + 1) * LANES, :] = k.T.astype(BF16)
    o_v = o_zx + 2 * DIFF_WIDTH
    v_ref[0] = mm(o_v, o_v + DIFF_WIDTH).astype(BF16)
    o_u = o_v + DIFF_WIDTH
    u_dt = mm(o_u, W_IN_PAD)
    u = u_dt[:, 0:S5_WIDTH]
    u_ref[0] = u
    u16_ref[0] = u.astype(BF16)
    dt_ref[0] = u_dt[:, S5_WIDTH:]


def _in_proj(x, shift, scale, norm_w, w_merged, conv_w, conv_b, rope_tabs):
    bsz, n, d = x.shape
    tm = _row_tile(n)
    nbh = n // HALO_ROWS
    tbh = tm // HALO_ROWS
    rope = rope_tabs is not None
    row = lambda b, i: (b, i, 0)
    per_b = lambda b, i: (b, 0, 0)
    const = lambda b, i: (0, 0)
    in_specs = [pl.BlockSpec((1, tm, d), row),
                pl.BlockSpec((1, HALO_ROWS, d), lambda b, i: (b, jnp.maximum(i * tbh - 1, 0), 0)),
                pl.BlockSpec((1, HALO_ROWS, d), lambda b, i: (b, jnp.minimum((i + 1) * tbh, nbh - 1), 0)),
                pl.BlockSpec((1, 1, d), per_b),
                pl.BlockSpec((1, 1, d), per_b),
                pl.BlockSpec((1, d), const),
                pl.BlockSpec((d, W_IN_PAD), const, pipeline_mode=pl.Buffered(1)),
                pl.BlockSpec((SSD_CONV, SSD_XBC), const),
                pl.BlockSpec((1, SSD_XBC), const)]
    args = [x, x, x, shift, scale, norm_w.reshape(1, d), w_merged, conv_w, conv_b.reshape(1, -1)]
    if rope:
        in_specs += [pl.BlockSpec((tm, LANES), lambda b, i: (i, 0))] * 2
        args += list(rope_tabs)
    widths = (SSD_WIDTH + SSD_XBC, LANES, DIFF_WIDTH, DIFF_WIDTH, DIFF_WIDTH, S5_WIDTH, S5_WIDTH)
    dtypes = (F32, F32, BF16, BF16, BF16, F32, BF16)
    out_specs = [pl.BlockSpec((1, tm, w), row) for w in widths]
    out_shape = [jax.ShapeDtypeStruct((bsz, n, w), t) for w, t in zip(widths, dtypes)]
    out_specs[3] = pl.BlockSpec((1, DIFF_WIDTH, tm), lambda b, i: (b, 0, i))
    out_shape[3] = jax.ShapeDtypeStruct((bsz, DIFF_WIDTH, n), BF16)
    return pl.pallas_call(
        functools.partial(_in_proj_kernel, rope=rope),
        grid=(bsz, n // tm),
        in_specs=in_specs,
        out_specs=out_specs,
        out_shape=out_shape,
        compiler_params=_cparams(2),
        name="in_proj_rope" if rope else "in_proj",
    )(*args)


def _merge_w_in(w):
    o1, o2 = SSD_IN, SSD_IN + DIFF_IN
    parts = [w[:, :SSD_WIDTH + SSD_XBC], w[:, o1:o2], w[:, o2:], w[:, SSD_WIDTH + SSD_XBC:o1]]
    m = jnp.concatenate(parts, axis=1)
    return jnp.pad(m, ((0, 0), (0, W_IN_PAD - m.shape[1]))).astype(BF16)


def _rope_tables(n):
    t = np.arange(n)
    pos = np.stack([t // GRID_W, t % GRID_W], axis=1).astype(np.float32)
    quarter = DIFF_HEAD_DIM // 4
    inv_freq = (np.float32(ROPE_THETA) ** (-np.arange(quarter, dtype=np.float32) / np.float32(quarter)))
    lane = np.arange(LANES)
    axis = (lane % DIFF_HEAD_DIM) // (DIFF_HEAD_DIM // 2)
    freq = lane % quarter
    ang = (pos[:, axis] * inv_freq.astype(np.float32)[freq][None, :]).astype(np.float32)
    sign = np.where(lane % (2 * quarter) < quarter, -1.0, 1.0).astype(np.float32)
    return jnp.asarray(np.cos(ang), F32), jnp.asarray(np.sin(ang) * sign[None, :], F32)


def _cumsum_rows(x, reverse):
    n = x.shape[0]
    row = lax.broadcasted_iota(jnp.int32, x.shape, 0)
    k = 1
    while k < n:
        if reverse:
            x = x + jnp.where(row < n - k, pltpu.roll(x, n - k, 0), 0.0)
        else:
            x = x + jnp.where(row >= k, pltpu.roll(x, k, 0), 0.0)
        k *= 2
    return x


def _ssd_chunk(xact_ref, dtt_ref, dta_ref, exp_ref, state_ref, y_ref, c, direction):
    q = SSD_CHUNK
    nh = SSD_HEADS
    r0 = pl.multiple_of(c * q, q)
    xa = xact_ref[pl.ds(r0, q), :]
    dta = dta_ref[pl.ds(r0, q), :]
    dt_t = dtt_ref[c]
    rows = lax.broadcasted_iota(jnp.int32, (q, q), 0)
    cols = lax.broadcasted_iota(jnp.int32, (q, q), 1)
    lane = cols
    mask = (rows >= cols) if direction == 0 else (rows <= cols)
    acs = _cumsum_rows(dta, reverse=direction == 1)
    acs_t = acs.T
    hi = acs.astype(BF16).astype(F32)
    r1 = acs - hi
    mid = r1.astype(BF16).astype(F32)
    lo = (r1 - mid).astype(BF16).astype(F32)
    parts = hi + pltpu.roll(mid, 8, 1) + pltpu.roll(lo, 16, 1)
    neg = -pltpu.roll(parts, 24, 1)
    lhs = jnp.where(lane < 24, parts, jnp.where(lane < 48, 1.0, 0.0)).astype(BF16)
    rhs = []
    for h in range(nh):
        j = direction * nh + h
        is_one = (lane == j) | (lane == 8 + j) | (lane == 16 + j)
        is_neg = (lane == 24 + j) | (lane == 32 + j) | (lane == 40 + j)
        rhs.append(jnp.where(is_one, 1.0, jnp.where(is_neg, neg, 0.0)).astype(BF16))
    diff = lax.dot_general(lhs, jnp.concatenate(rhs, axis=0), (((1,), (1,)), ((), ())),
                           preferred_element_type=F32)
    mask4 = jnp.concatenate([mask] * nh, axis=1)
    decay = jnp.exp(jnp.where(mask4, diff, -jnp.inf))
    acs_x = jnp.dot(parts.astype(BF16), exp_ref[direction], preferred_element_type=F32)
    last = q - 1 if direction == 0 else 0
    total_x = acs_x[last:last + 1, :]

    xb = xa[:, 0:SSD_WIDTH].astype(BF16)
    bm = xa[:, SSD_WIDTH:SSD_WIDTH + LANES]
    cm = xa[:, SSD_WIDTH + LANES:]
    bm_t = bm.T
    state = state_ref[direction]
    y_off = jnp.dot(cm.astype(BF16), state.astype(BF16), preferred_element_type=F32) * jnp.exp(acs_x)
    row_grp = lax.broadcasted_iota(jnp.int32, (LANES, SSD_WIDTH), 0) // SSD_STATE
    lane_head = lax.broadcasted_iota(jnp.int32, (LANES, SSD_WIDTH), 1) // SSD_HEAD_DIM
    x_head = lax.broadcasted_iota(jnp.int32, (q, SSD_WIDTH), 1) // SSD_HEAD_DIM
    b_parts, x_parts = [], []
    bm_sel = jnp.concatenate([jnp.where(lane // SSD_STATE == g, bm, 0.0) for g in range(SSD_GROUPS)],
                             axis=0).astype(BF16)
    cb_all = lax.dot_general(cm.astype(BF16), bm_sel, (((1,), (1,)), ((), ())),
                             preferred_element_type=F32)
    y_diag = []
    for g in range(SSD_GROUPS):
        cb = cb_all[:, g * q:(g + 1) * q]
        xg = xb[:, g * LANES:(g + 1) * LANES]
        pair = []
        for h in range(g * 2, g * 2 + 2):
            j = direction * nh + h
            row_dt = dt_t[j:j + 1, :]
            pair.append((cb * decay[:, h * q:(h + 1) * q] * row_dt).astype(BF16))
            w_row = row_dt * jnp.exp(acs_t[j:j + 1, last:last + 1] - acs_t[j:j + 1, :])
            b_parts.append((bm_t * w_row).astype(BF16))
            x_parts.append(jnp.where(x_head == h, xb, jnp.zeros_like(xb)))
        x_sel = jnp.concatenate([jnp.where(lane < SSD_HEAD_DIM, xg, jnp.zeros_like(xg)),
                                 jnp.where(lane >= SSD_HEAD_DIM, xg, jnp.zeros_like(xg))], axis=0)
        y_diag.append(jnp.dot(jnp.concatenate(pair, axis=1), x_sel, preferred_element_type=F32))
    y_ref[pl.ds(r0, q), :] += jnp.concatenate(y_diag, axis=-1) + y_off
    upd = jnp.dot(jnp.concatenate(b_parts, axis=1), jnp.concatenate(x_parts, axis=0),
                  preferred_element_type=F32)
    upd = jnp.where(row_grp == lane_head // (nh // SSD_GROUPS), upd, 0.0)
    state_ref[direction] = jnp.exp(total_x) * state + upd


def _ssd_kernel(*refs, n_ctx, n_lat, need_ctx):
    (zx_l, dt_l, zx_c, dt_c, dtb_ref, alog_ref, d_ref, nw_ref, exp_ref) = refs[:9]
    outs = refs[9:11] if need_ctx else (refs[9], None)
    xact_ref, dtt_ref, dta_ref, y_ref, state_ref = refs[-5:]
    out_l, out_c = outs

    for seg_ref, n, base in ((zx_c, n_ctx, 0), (zx_l, n_lat, n_ctx)):
        def copy_rows(i, carry, seg_ref=seg_ref, base=base):
            r = pl.multiple_of(i * SSD_CHUNK, SSD_CHUNK)
            act = seg_ref[0, pl.ds(r, SSD_CHUNK), SSD_WIDTH:]
            xact_ref[pl.ds(base + r, SSD_CHUNK), :] = act
            y_ref[pl.ds(base + r, SSD_CHUNK), :] = act[:, 0:SSD_WIDTH] * d_ref[...]
            return carry

        lax.fori_loop(0, n // SSD_CHUNK, copy_rows, 0)
    a_neg = -jnp.exp(alog_ref[...])
    head_lane = lax.broadcasted_iota(jnp.int32, (SSD_CHUNK, LANES), 1) < 2 * SSD_HEADS
    for seg_ref, n, base in ((dt_c, n_ctx, 0), (dt_l, n_lat, n_ctx)):
        for t0 in range(0, n, SSD_CHUNK):
            dt = jax.nn.softplus(seg_ref[0, t0:t0 + SSD_CHUNK, :] + dtb_ref[...])
            dtt_ref[(base + t0) // SSD_CHUNK] = dt.T
            dta_ref[base + t0:base + t0 + SSD_CHUNK, :] = jnp.where(head_lane, dt * a_neg, 0.0)
    n_tot = n_ctx + n_lat
    state_ref[...] = jnp.zeros(state_ref.shape, F32)

    ncc, nct = n_ctx // SSD_CHUNK, n_tot // SSD_CHUNK

    def step(i, carry):
        _ssd_chunk(xact_ref, dtt_ref, dta_ref, exp_ref, state_ref, y_ref, i, 0)
        cb_idx = jnp.where(i < ncc, ncc - 1 - i, nct - 1 - (i - ncc))
        _ssd_chunk(xact_ref, dtt_ref, dta_ref, exp_ref, state_ref, y_ref, cb_idx, 1)
        return carry

    lax.fori_loop(0, nct, step, 0, unroll=3)

    for seg_ref, o_ref, n, base in ((zx_c, out_c, n_ctx, 0), (zx_l, out_l, n_lat, n_ctx)):
        if o_ref is None:
            continue
        def gate_rows(i, carry, seg_ref=seg_ref, o_ref=o_ref, base=base):
            r = pl.multiple_of(i * SSD_ROWS, SSD_ROWS)
            z = seg_ref[0, pl.ds(r, SSD_ROWS), 0:SSD_WIDTH]
            v = y_ref[pl.ds(base + r, SSD_ROWS), :] * _silu(z)
            o_ref[0, pl.ds(r, SSD_ROWS), :] = _rms(v, nw_ref[...]).astype(BF16)
            return carry

        lax.fori_loop(0, n // SSD_ROWS, gate_rows, 0, unroll=4)


def _ssd(zx_l, dt_l, zx_c, dt_c, dt_bias, a_log, d_skip, norm_w, need_ctx):
    bsz, n_lat, wzx = zx_l.shape
    n_ctx = zx_c.shape[1]
    n_tot = n_ctx + n_lat
    pad8 = lambda v: jnp.pad(v.reshape(1, -1), ((0, 0), (0, LANES - v.size)))
    per_b = lambda b: (b, 0, 0)
    const = lambda b: (0, 0)
    k_idx = np.arange(LANES)[None, :, None]
    d_idx = np.arange(2)[:, None, None]
    h_idx = (np.arange(SSD_WIDTH) // SSD_HEAD_DIM)[None, None, :]
    expand = jnp.asarray((k_idx < 24) & (k_idx % 8 == d_idx * SSD_HEADS + h_idx), BF16)
    params = [pad8(dt_bias), pad8(a_log),
              jnp.repeat(d_skip, SSD_HEAD_DIM).reshape(1, -1), norm_w.reshape(1, -1)]
    in_specs = [pl.BlockSpec((1, n_lat, wzx), per_b), pl.BlockSpec((1, n_lat, LANES), per_b),
                pl.BlockSpec((1, n_ctx, wzx), per_b), pl.BlockSpec((1, n_ctx, LANES), per_b)]
    in_specs += [pl.BlockSpec(p.shape, const) for p in params]
    in_specs += [pl.BlockSpec(expand.shape, lambda b: (0, 0, 0))]
    out_specs = [pl.BlockSpec((1, n_lat, SSD_WIDTH), per_b)]
    out_shape = [jax.ShapeDtypeStruct((bsz, n_lat, SSD_WIDTH), BF16)]
    if need_ctx:
        out_specs.append(pl.BlockSpec((1, n_ctx, SSD_WIDTH), per_b))
        out_shape.append(jax.ShapeDtypeStruct((bsz, n_ctx, SSD_WIDTH), BF16))
    scratch = [pltpu.VMEM((n_tot, SSD_XBC), F32),
               pltpu.VMEM((n_tot // SSD_CHUNK, LANES, SSD_CHUNK), F32),
               pltpu.VMEM((n_tot, LANES), F32),
               pltpu.VMEM((n_tot, SSD_WIDTH), F32),
               pltpu.VMEM((2, SSD_GROUPS * SSD_STATE, SSD_WIDTH), F32)]
    outs = pl.pallas_call(
        functools.partial(_ssd_kernel, n_ctx=n_ctx, n_lat=n_lat, need_ctx=need_ctx),
        grid=(bsz,),
        in_specs=in_specs,
        out_specs=out_specs,
        out_shape=out_shape,
        scratch_shapes=scratch,
        compiler_params=_cparams(1),
        name="ssd_mixer",
    )(zx_l, dt_l, zx_c, dt_c, *params, expand)
    return (outs[0], outs[1]) if need_ctx else (outs[0], None)


def _attn_kernel(*refs, n_kv, lam_init):
    q_ref = refs[0]
    k_refs = refs[1:1 + n_kv]
    v_refs = refs[1 + n_kv:1 + 2 * n_kv]
    lq1, lk1, lq2, lk2, sw_ref, o_ref = refs[1 + 2 * n_kv:-4]
    s_refs, e_refs = refs[-4:-2], refs[-2:]
    tq, hw = q_ref.shape[1:]
    unit = s_refs[0].shape[0]
    lane = lax.broadcasted_iota(jnp.int32, (unit, hw), 1)
    lam = (jnp.exp(jnp.sum(lq1[...] * lk1[...], axis=-1, keepdims=True))
           - jnp.exp(jnp.sum(lq2[...] * lk2[...], axis=-1, keepdims=True)) + lam_init)
    offs = np.cumsum([0] + [k_ref.shape[2] for k_ref in k_refs])
    v_ext = [jnp.concatenate([v_ref[0], jnp.ones(v_ref.shape[1:], BF16)], axis=-1) for v_ref in v_refs]
    rows = max(ATTN_ROW_SPLIT, unit // 4)
    pieces = [(lo, min(lo + rows, unit)) for lo in range(0, unit, rows)]
    for base in range(0, tq, unit):
        q = q_ref[0, base:base + unit, :]
        for comp, s_ref in enumerate(s_refs):
            own = (lane >= DIFF_HEAD_DIM) if comp else (lane < DIFF_HEAD_DIM)
            qc = jnp.where(own, q, jnp.zeros_like(q))
            for i, k_ref in enumerate(k_refs):
                s_ref[:, offs[i]:offs[i + 1]] = jnp.dot(qc, k_ref[0], preferred_element_type=F32)
        outs = []
        for s_ref, e_ref in zip(s_refs, e_refs):
            parts = []
            for lo, hi in pieces:
                s = s_ref[lo:hi, :]
                e_ref[lo:hi, :] = jnp.exp(s - jnp.max(s, axis=-1, keepdims=True)).astype(BF16)
                ov = None
                for i in range(n_kv):
                    part = jnp.dot(e_ref[lo:hi, offs[i]:offs[i + 1]], v_ext[i], preferred_element_type=F32)
                    ov = part if ov is None else ov + part
                parts.append(ov[:, 0:hw] * (1.0 / ov[:, hw:hw + 1]))
            outs.append(parts)
        for (lo, hi), o0, o1 in zip(pieces, *outs):
            o = o0 - lam * o1
            o_ref[0, base + lo:base + hi, :] = (_rms(o, sw_ref[...]) * (1.0 - lam_init)).astype(BF16)


def _attention(q, ks, vs, lam_params, subln_w, lam_init):
    bsz, n, _ = q.shape
    tq = min(ATTN_Q_TILE, n)
    n_kv = len(ks)
    n_keys = sum(a.shape[2] for a in ks)
    hw = 2 * DIFF_HEAD_DIM
    in_specs = [pl.BlockSpec((1, tq, hw), lambda b, h, i: (b, i, h))]
    in_specs += [pl.BlockSpec((1, hw, a.shape[2]), lambda b, h, i: (b, h, 0)) for a in ks]
    in_specs += [pl.BlockSpec((1, a.shape[1], hw), lambda b, h, i: (b, 0, h)) for a in vs]
    in_specs += [pl.BlockSpec((1, DIFF_HEAD_DIM), lambda b, h, i: (0, 0))] * 4
    in_specs += [pl.BlockSpec((1, hw), lambda b, h, i: (0, 0))]
    return pl.pallas_call(
        functools.partial(_attn_kernel, n_kv=n_kv, lam_init=lam_init),
        grid=(bsz, DIFF_HEADS, n // tq),
        in_specs=in_specs,
        out_specs=pl.BlockSpec((1, tq, hw), lambda b, h, i: (b, i, h)),
        out_shape=jax.ShapeDtypeStruct((bsz, n, DIFF_WIDTH), BF16),
        scratch_shapes=([pltpu.VMEM((min(ATTN_UNIT, tq), n_keys), F32)] * 2
                        + [pltpu.VMEM((min(ATTN_UNIT, tq), n_keys), BF16)] * 2),
        compiler_params=_cparams(3),
        name="diff_attention",
    )(q, *ks, *vs, *[p.reshape(1, -1) for p in lam_params], subln_w.reshape(1, -1))


def _s5_ops_kernel(p_ref, bt_ref, cn_ref, ct_ref, kern_ref, ws_ref, co_ref, at_ref):
    lr, li = p_ref[0, 0, 0:1, :], p_ref[0, 0, 1:2, :]
    step = jnp.exp(p_ref[0, 0, 2:3, :])
    mag = jnp.exp(lr * step)
    ar, ai = mag * jnp.cos(li * step), mag * jnp.sin(li * step)
    den = lr * lr + li * li
    nr, ni = ar - 1.0, ai
    cr, ci = (nr * lr + ni * li) / den, (ni * lr - nr * li) / den
    bt_re, bt_im = bt_ref[0, 0, 0], bt_ref[0, 0, 1]
    bb_re, bb_im = cr * bt_re - ci * bt_im, cr * bt_im + ci * bt_re
    c_re, c_im = cn_ref[0, 0, 0], cn_ref[0, 0, 1]
    ct_re, ct_im = ct_ref[0, 0, 0], ct_ref[0, 0, 1]
    pr, pi = jnp.ones_like(ar), jnp.zeros_like(ar)
    for tau in range(S5_T):
        w_re, w_im = bb_re * pr - bb_im * pi, bb_re * pi + bb_im * pr
        ws_ref[0, 0, 0, tau] = w_re
        ws_ref[0, 0, 1, tau] = w_im
        kern_ref[0, 0, tau] = (jnp.dot(w_re, ct_re, preferred_element_type=F32, precision=HI)
                               - jnp.dot(w_im, ct_im, preferred_element_type=F32, precision=HI))
        pr, pi = pr * ar - pi * ai, pr * ai + pi * ar
        co_ref[0, 0, 0, tau] = c_re * pr - c_im * pi
        co_ref[0, 0, 1, tau] = -(c_re * pi + c_im * pr)
    at_ref[0, 0] = jnp.concatenate([pr, pi, jnp.zeros((SUBLANES - 2, S5_STATE), F32)], axis=0)


def _s5_matrices(lam_re, lam_im, log_step, b_re, b_im, c_re, c_im):
    t, gc, ng, ns = S5_T, S5_GROUP, S5_GROUPS, S5_STATE
    f32 = lambda v: v.astype(F32)
    p = jnp.concatenate([f32(lam_re)[:, :, None, :], f32(lam_im)[:, :, None, :],
                         jnp.broadcast_to(f32(log_step)[:, :, None, None], (2, ng, 1, ns)),
                         jnp.zeros((2, ng, SUBLANES - 3, ns), F32)], axis=2)
    bt = jnp.stack([f32(b_re), f32(b_im)], axis=2).swapaxes(-1, -2)
    cn = jnp.stack([f32(c_re), f32(c_im)], axis=2)
    ct = cn.swapaxes(-1, -2)
    blk = lambda *tail: pl.BlockSpec((1, 1) + tail, lambda d, g: (d, g) + (0,) * len(tail))
    kern, ws, co, at = pl.pallas_call(
        _s5_ops_kernel,
        grid=(2, ng),
        in_specs=[blk(SUBLANES, ns), blk(2, gc, ns), blk(2, gc, ns), blk(2, ns, gc)],
        out_specs=[blk(t, gc, gc), blk(2, t, gc, ns), blk(2, t, gc, ns), blk(SUBLANES, ns)],
        out_shape=[jax.ShapeDtypeStruct((2, ng, t, gc, gc), F32),
                   jax.ShapeDtypeStruct((2, ng, 2, t, gc, ns), F32),
                   jax.ShapeDtypeStruct((2, ng, 2, t, gc, ns), F32),
                   jax.ShapeDtypeStruct((2, ng, SUBLANES, ns), F32)],
        compiler_params=_cparams(2),
        name="s5_operators",
    )(p, bt, cn, ct)
    kern = kern.transpose(0, 1, 3, 2, 4)
    fwd = kern[0].reshape(S5_GROUPS, gc, t * gc)
    bwd = kern[1][:, :, ::-1].reshape(S5_GROUPS, gc, t * gc)
    band = jnp.concatenate([bwd[..., :(t - 1) * gc], bwd[..., (t - 1) * gc:] + fwd[..., :gc],
                            fwd[..., gc:]], axis=-1)
    k_full = jnp.concatenate([band[..., (t - 1 - s) * gc:(2 * t - 1 - s) * gc] for s in range(t)],
                             axis=1)
    wf, wb = ws[0][:, :, ::-1], ws[1]
    rows_sc = lambda w: w.reshape(ng, t * gc, ns)
    w_state = jnp.concatenate([rows_sc(wf[:, 0]), rows_sc(wb[:, 0]),
                               rows_sc(wf[:, 1]), rows_sc(wb[:, 1])], axis=-1)
    cf, cbk = co[0], co[1][:, :, ::-1]
    cols_tc = lambda w: w.transpose(0, 3, 1, 2).reshape(ng, ns, t * gc)
    c_off = jnp.concatenate([cols_tc(cf[:, 0]), cols_tc(cbk[:, 0]),
                             cols_tc(cf[:, 1]), cols_tc(cbk[:, 1])], axis=1)
    a_chunk = jnp.concatenate([at[0], at[1]], axis=-1)
    return k_full.astype(BF16), w_state.astype(BF16), c_off.astype(BF16), a_chunk


def _s5_kernel(uc_ref, ul_ref, kf_ref, ws_ref, co_ref, a_ref, yc_ref, yl_ref, e_ref, h_ref):
    ncc, bsz, w = uc_ref.shape[1:]
    ncl = ul_ref.shape[1]
    nct = ncc + ncl
    uc = uc_ref[0].reshape(ncc * bsz, w)
    ul = ul_ref[0].reshape(ncl * bsz, w)
    e_ref[0:ncc] = jnp.dot(uc, ws_ref[0], preferred_element_type=F32).reshape(ncc, bsz, w)
    e_ref[ncc:nct] = jnp.dot(ul, ws_ref[0], preferred_element_type=F32).reshape(ncl, bsz, w)
    ar = a_ref[0, 0:1, :]
    ai = a_ref[0, 1:2, :]
    half = S5_STATE
    lane = lax.broadcasted_iota(jnp.int32, (bsz, LANES), 1)
    is_f = lane < half

    def step(i, carry):
        h_re, h_im = carry
        jf = i
        jb = jnp.where(i < ncc, ncc - 1 - i, nct - 1 - (i - ncc))
        h_ref[jf, :, 0:half] = h_re[:, 0:half]
        h_ref[jb, :, half:LANES] = h_re[:, half:]
        h_ref[jf, :, LANES:LANES + half] = h_im[:, 0:half]
        h_ref[jb, :, LANES + half:] = h_im[:, half:]
        e_re = jnp.where(is_f, e_ref[jf, :, 0:LANES], e_ref[jb, :, 0:LANES])
        e_im = jnp.where(is_f, e_ref[jf, :, LANES:], e_ref[jb, :, LANES:])
        return (ar * h_re - ai * h_im + e_re, ar * h_im + ai * h_re + e_im)

    zero = jnp.zeros((bsz, LANES), F32)
    lax.fori_loop(0, nct, step, (zero, zero))
    for u, y_ref, lo, nc in ((uc, yc_ref, 0, ncc), (ul, yl_ref, ncc, ncl)):
        hs = h_ref[lo:lo + nc].reshape(nc * bsz, w).astype(BF16)
        y = (jnp.dot(u, kf_ref[0], preferred_element_type=F32)
             + jnp.dot(hs, co_ref[0], preferred_element_type=F32))
        y_ref[0] = y.reshape(nc, bsz, w).astype(BF16)


def _s5_to_chunks(u):
    bsz, n, _ = u.shape
    u = u.reshape(bsz, n // S5_T, S5_T, S5_GROUPS, S5_GROUP).transpose(3, 1, 0, 2, 4)
    return u.reshape(S5_GROUPS, n // S5_T, bsz, S5_T * S5_GROUP)


def _s5_from_chunks(y, bsz):
    nchunk = y.shape[1]
    y = y.reshape(S5_GROUPS, nchunk, bsz, S5_T, S5_GROUP).transpose(2, 1, 3, 0, 4)
    return y.reshape(bsz, nchunk * S5_T, S5_WIDTH)


def _s5(u_l, u_c, mats):
    k_full, w_state, c_off, a_chunk = mats
    bsz = u_l.shape[0]
    uc, ul = _s5_to_chunks(u_c), _s5_to_chunks(u_l)
    ncc, ncl, w = uc.shape[1], ul.shape[1], ul.shape[3]
    nct = ncc + ncl
    blk = lambda g: (g, 0, 0, 0)
    mat = lambda g: (g, 0, 0)
    y_c, y_l = pl.pallas_call(
        _s5_kernel,
        grid=(S5_GROUPS,),
        in_specs=[pl.BlockSpec((1, ncc, bsz, w), blk), pl.BlockSpec((1, ncl, bsz, w), blk),
                  pl.BlockSpec((1, w, w), mat), pl.BlockSpec((1, w, w), mat),
                  pl.BlockSpec((1, w, w), mat), pl.BlockSpec((1, SUBLANES, LANES), mat)],
        out_specs=[pl.BlockSpec((1, ncc, bsz, w), blk), pl.BlockSpec((1, ncl, bsz, w), blk)],
        out_shape=[jax.ShapeDtypeStruct((S5_GROUPS, ncc, bsz, w), BF16),
                   jax.ShapeDtypeStruct((S5_GROUPS, ncl, bsz, w), BF16)],
        scratch_shapes=[pltpu.VMEM((nct, bsz, w), F32), pltpu.VMEM((nct, bsz, w), F32)],
        compiler_params=_cparams(1),
        name="s5_mixer",
    )(uc, ul, k_full, w_state, c_off, a_chunk)
    return _s5_from_chunks(y_l, bsz), _s5_from_chunks(y_c, bsz)


def _mix_ffn_kernel(*refs):
    rows = [refs[3 * j:3 * j + 3] for j in range(5)]
    (ga_ref, sh_ref, sc_ref, gf_ref, d_ref, gw_ref, gb_ref, wo_ref, nwm_ref, nw1_ref, nw2_ref,
     wg_ref, wu_ref, cw_ref, cb_ref, wd_ref, o_ref, act_ref) = refs[15:]
    i = pl.program_id(1)
    is_first = i == 0
    is_last = i == pl.num_programs(1) - 1
    tm = o_ref.shape[1]
    a, b, ys, u, x_ext = (jnp.concatenate([r[0] for r in trio], axis=0) for trio in rows)
    y = jax.nn.gelu(ys + d_ref[...] * u, approximate=True)
    s_gate = jnp.dot(y.astype(BF16), gw_ref[...], preferred_element_type=F32) + gb_ref[...]
    s = (y * jax.nn.sigmoid(s_gate)).astype(BF16)
    mix = jnp.dot(jnp.concatenate([a, b, s], axis=-1), wo_ref[...], preferred_element_type=F32)
    x1_ext = x_ext + ga_ref[0] * _rms(mix, nwm_ref[...])
    x1 = x1_ext[0:tm]
    h_ext = _norm_modulate(x1_ext, nw1_ref[...], sc_ref[0], sh_ref[0])
    h_ext = jnp.concatenate([h_ext[0:tm],
                             jnp.where(is_first, 0.0, h_ext[tm:tm + HALO_ROWS]),
                             jnp.where(is_last, 0.0, h_ext[tm + HALO_ROWS:])], axis=0).astype(BF16)
    h = h_ext[0:tm]
    row = lax.broadcasted_iota(jnp.int32, (tm, 1), 0)
    prev_row, next_row = tm + HALO_ROWS - 1, tm + HALO_ROWS
    for lo in range(0, wg_ref.shape[1], FFN_CHUNK):
        hi = lo + FFN_CHUNK
        g_ext = jnp.dot(h_ext, wg_ref[:, lo:hi], preferred_element_type=F32)
        g = g_ext[0:tm]
        g_prev = jnp.where(row == 0, g_ext[prev_row:prev_row + 1], pltpu.roll(g, 1, 0))
        g_next = jnp.where(row == tm - 1, g_ext[next_row:next_row + 1], pltpu.roll(g, tm - 1, 0))
        conv = (cw_ref[0:1, lo:hi] * g_prev + cw_ref[1:2, lo:hi] * g + cw_ref[2:3, lo:hi] * g_next
                + cb_ref[:, lo:hi])
        up = jnp.dot(h, wu_ref[:, lo:hi], preferred_element_type=F32)
        act_ref[:, lo:hi] = (_silu(conv) * up).astype(BF16)
    f = jnp.dot(act_ref[...], wd_ref[...], preferred_element_type=F32)
    o_ref[0] = x1 + gf_ref[0] * _rms(f, nw2_ref[...])


def _mix_ffn(a, b, ys, u, x, g_a, shift, scale, g_f, s5_d, glu_w, glu_b, w_out, norm_mix,
             norm_pre, norm_post, w_gate, w_up, conv_w, conv_b, w_down):
    bsz, n, d = x.shape
    tm = min(FFN_ROW_TILE, n)
    f = w_gate.shape[1]
    nbh = n // HALO_ROWS
    tbh = tm // HALO_ROWS
    per_b = lambda bi, i: (bi, 0, 0)
    const = lambda bi, i: (0, 0)
    once = dict(pipeline_mode=pl.Buffered(1))

    def row_specs(w):
        return [pl.BlockSpec((1, tm, w), lambda bi, i: (bi, i, 0)),
                pl.BlockSpec((1, HALO_ROWS, w), lambda bi, i: (bi, jnp.maximum(i * tbh - 1, 0), 0)),
                pl.BlockSpec((1, HALO_ROWS, w), lambda bi, i: (bi, jnp.minimum((i + 1) * tbh, nbh - 1), 0))]

    row_args, in_specs = [], []
    for arr in (a, b, ys, u, x):
        row_args += [arr] * 3
        in_specs += row_specs(arr.shape[2])
    in_specs += [pl.BlockSpec((1, 1, d), per_b)] * 4
    in_specs += [pl.BlockSpec((1, S5_WIDTH), const), pl.BlockSpec((S5_WIDTH, S5_WIDTH), const),
                 pl.BlockSpec((1, S5_WIDTH), const), pl.BlockSpec((d, d), const, **once),
                 pl.BlockSpec((1, d), const), pl.BlockSpec((1, d), const), pl.BlockSpec((1, d), const),
                 pl.BlockSpec((d, f), const, **once), pl.BlockSpec((d, f), const, **once),
                 pl.BlockSpec((FFN_CONV, f), const), pl.BlockSpec((1, f), const),
                 pl.BlockSpec((f, d), const, **once)]
    return pl.pallas_call(
        _mix_ffn_kernel,
        grid=(bsz, n // tm),
        in_specs=in_specs,
        out_specs=pl.BlockSpec((1, tm, d), lambda bi, i: (bi, i, 0)),
        out_shape=jax.ShapeDtypeStruct((bsz, n, d), F32),
        scratch_shapes=[pltpu.VMEM((tm, f), BF16)],
        compiler_params=_cparams(2),
        name="mix_ffn",
    )(*row_args, g_a, shift, scale, g_f, s5_d.reshape(1, -1), glu_w.astype(BF16), glu_b.reshape(1, -1),
      w_out.astype(BF16), norm_mix.reshape(1, -1), norm_pre.reshape(1, -1), norm_post.reshape(1, -1),
      w_gate.astype(BF16), w_up.astype(BF16), conv_w, conv_b.reshape(1, -1), w_down.astype(BF16))


def kernel(x, c, ctx, c_ctx, mod_w, mod_b, mix_norm_pre, mix_norm_post, ffn_norm_pre, ffn_norm_post, w_in, w_out, ssd_conv_w, ssd_conv_b, ssd_dt_bias, ssd_a_log, ssd_d, ssd_norm_w, diff_lam_q1, diff_lam_k1, diff_lam_q2, diff_lam_k2, diff_subln_w, s5_lam_re, s5_lam_im, s5_log_step, s5_b_re, s5_b_im, s5_c_re, s5_c_im, s5_d, s5_glu_w, s5_glu_b, ffn_w_gate, ffn_w_up, ffn_conv_w, ffn_conv_b, ffn_w_down):
    bsz, n_lat, d = x.shape
    n_layers = mod_w.shape[0]
    rope_tabs = _rope_tables(n_lat)
    cc = jnp.concatenate([c, jnp.broadcast_to(c_ctx[None, :], (SUBLANES, d))], axis=0)
    mod = _modulation(cc, mod_w, mod_b)

    for layer in range(n_layers):
        need_ctx = layer < n_layers - 1
        lam_init = 0.8 - 0.6 * math.exp(-0.3 * layer)
        mod_l = mod[layer, :bsz].reshape(bsz, N_MOD, 1, d)
        mod_c = jnp.broadcast_to(mod[layer, bsz].reshape(1, N_MOD, 1, d), (bsz, N_MOD, 1, d))
        sh_a, sc_a, g_a, sh_f, sc_f, g_f = (mod_l[:, i] for i in range(N_MOD))
        csh_a, csc_a, cg_a, csh_f, csc_f, cg_f = (mod_c[:, i] for i in range(N_MOD))

        w_merged = _merge_w_in(w_in[layer])
        in_p = (mix_norm_pre[layer], w_merged, ssd_conv_w[layer], ssd_conv_b[layer])
        zx_l, dt_l, q_l, k_l, v_l, u_l, u16_l = _in_proj(x, sh_a, sc_a, *in_p, rope_tabs)
        zx_c, dt_c, q_c, k_c, v_c, u_c, u16_c = _in_proj(ctx, csh_a, csc_a, *in_p, None)

        a_l, a_c = _ssd(zx_l, dt_l, zx_c, dt_c, ssd_dt_bias[layer], ssd_a_log[layer], ssd_d[layer],
                        ssd_norm_w[layer], need_ctx)

        lam_params = (diff_lam_q1[layer], diff_lam_k1[layer], diff_lam_q2[layer], diff_lam_k2[layer])
        b_l = _attention(q_l, (k_c, k_l), (v_c, v_l), lam_params, diff_subln_w[layer], lam_init)

        mats = _s5_matrices(s5_lam_re[layer], s5_lam_im[layer], s5_log_step[layer], s5_b_re[layer],
                            s5_b_im[layer], s5_c_re[layer], s5_c_im[layer])
        ys_l, ys_c = _s5(u16_l, u16_c, mats)

        mix_p = (s5_d[layer], s5_glu_w[layer], s5_glu_b[layer], w_out[layer], mix_norm_post[layer])
        ffn_p = (ffn_norm_pre[layer], ffn_norm_post[layer], ffn_w_gate[layer], ffn_w_up[layer],
                 ffn_conv_w[layer], ffn_conv_b[layer], ffn_w_down[layer])
        x = _mix_ffn(a_l, b_l, ys_l, u_l, x, g_a, sh_f, sc_f, g_f, *mix_p, *ffn_p)
        if need_ctx:
            b_c = _attention(q_c, (k_c,), (v_c,), lam_params, diff_subln_w[layer], lam_init)
            ctx = _mix_ffn(a_c, b_c, ys_c, u_c, ctx, cg_a, csh_f, csc_f, cg_f, *mix_p, *ffn_p)
    return x
```

```python
import functools
import math

import numpy as np
import jax
import jax.numpy as jnp
from jax import lax
from jax.experimental import pallas as pl
from jax.experimental.pallas import tpu as pltpu

F32 = jnp.float32
BF16 = jnp.bfloat16
HI = lax.Precision.HIGHEST

D_MODEL = 1024
N_LAYERS = 2
GRID_W = 64
EPS = 1e-6
N_MOD = 6
SSD_HEADS = 4
SSD_HEAD_DIM = 64
SSD_WIDTH = 256
SSD_GROUPS = 2
SSD_STATE = 64
SSD_CONV = 5
SSD_CHUNK = 128
SSD_ROWS = 64
CONV_ROWS = 64
SSD_XBC = 512
SSD_IN = 776
DIFF_HEADS = 4
DIFF_HEAD_DIM = 64
DIFF_WIDTH = 512
DIFF_IN = 1536
ROPE_THETA = 10000.0
ATTN_Q_TILE = 2048
ATTN_UNIT = 1024
ATTN_ROW_SPLIT = 128
S5_GROUP = 16
S5_WIDTH = 256
S5_GROUPS = 16
S5_STATE = 64
S5_T = 16
FFN_DIM = 2816
FFN_CHUNK = 256
FFN_ROW_TILE = 1024
HALO_ROWS = 16
FFN_CONV = 3

LANES = 128
SUBLANES = 8
W_IN_PAD = 2688
VMEM_LIMIT = 56 * 1024 * 1024


def _cparams(n_axes):
    return pltpu.CompilerParams(dimension_semantics=("parallel",) * n_axes,
                                vmem_limit_bytes=VMEM_LIMIT)


def _rms(x, w):
    return x * lax.rsqrt(jnp.mean(x * x, axis=-1, keepdims=True) + EPS) * w


def _norm_modulate(x, w, scale, shift):
    return x * lax.rsqrt(jnp.mean(x * x, axis=-1, keepdims=True) + EPS) * (w * (1.0 + scale)) + shift


def _silu(x):
    return x * jax.nn.sigmoid(x)


def _row_tile(n):
    return min(1024, n)


def _mod_kernel(c_ref, w_ref, b_ref, o_ref):
    a = _silu(c_ref[...])
    o_ref[0] = jnp.dot(a, w_ref[0], preferred_element_type=F32, precision=HI) + b_ref[0]


def _modulation(cc, mod_w, mod_b):
    nl, d, n = mod_w.shape
    r = cc.shape[0]
    tn = 512
    return pl.pallas_call(
        _mod_kernel,
        grid=(nl, n // tn),
        in_specs=[pl.BlockSpec((r, d), lambda l, j: (0, 0)),
                  pl.BlockSpec((1, d, tn), lambda l, j: (l, 0, j)),
                  pl.BlockSpec((1, 1, tn), lambda l, j: (l, 0, j))],
        out_specs=pl.BlockSpec((1, r, tn), lambda l, j: (l, 0, j)),
        out_shape=jax.ShapeDtypeStruct((nl, r, n), F32),
        compiler_params=_cparams(2),
        name="modulation",
    )(cc, mod_w, mod_b.reshape(nl, 1, n))


def _in_proj_kernel(*refs, rope):
    (x_ref, xp_ref, xn_ref, sh_ref, sc_ref, nw_ref, w_ref, cw_ref, cb_ref) = refs[:9]
    cos_ref, sin_ref = refs[9:11] if rope else (None, None)
    zx_ref, dt_ref, q_ref, k_ref, v_ref, u_ref, u16_ref = refs[-7:]
    i = pl.program_id(1)
    tm = x_ref.shape[1]
    x_ext = jnp.concatenate([xp_ref[0], x_ref[0], xn_ref[0]], axis=0)
    h_ext = _norm_modulate(x_ext, nw_ref[...], sc_ref[0], sh_ref[0])
    h_ext = jnp.concatenate([jnp.where(i == 0, 0.0, h_ext[0:HALO_ROWS]),
                             h_ext[HALO_ROWS:HALO_ROWS + tm],
                             jnp.where(i == pl.num_programs(1) - 1, 0.0, h_ext[HALO_ROWS + tm:])],
                            axis=0).astype(BF16)
    hb = h_ext[HALO_ROWS:HALO_ROWS + tm]

    def mm(lo, hi):
        return jnp.dot(hb, w_ref[:, lo:hi], preferred_element_type=F32)

    xbc = jnp.dot(h_ext, w_ref[:, SSD_WIDTH:SSD_WIDTH + SSD_XBC], preferred_element_type=F32)
    halo = (SSD_CONV - 1) // 2
    win = CONV_ROWS + 2 * SUBLANES
    for r in range(0, tm, CONV_ROWS):
        lo = HALO_ROWS - SUBLANES + r
        window = xbc[lo:lo + win]
        acc = jnp.broadcast_to(cb_ref[...], (CONV_ROWS, SSD_XBC))
        for k in range(SSD_CONV):
            shift = (halo - k) % win
            tap = window if shift == 0 else pltpu.roll(window, shift, 0)
            acc = acc + cw_ref[k:k + 1, :] * tap[SUBLANES:SUBLANES + CONV_ROWS]
        zx_ref[0, r:r + CONV_ROWS, SSD_WIDTH:] = _silu(acc)

    def rot(p):
        lane = lax.broadcasted_iota(jnp.int32, p.shape, 1)
        partner = jnp.where(lane % 32 < 16, pltpu.roll(p, LANES - 16, 1), pltpu.roll(p, 16, 1))
        return p * cos_ref[...] + partner * sin_ref[...]

    o_zx = SSD_WIDTH + SSD_XBC
    zx_ref[0, :, 0:SSD_WIDTH] = mm(0, SSD_WIDTH)
    q_all = mm(o_zx, o_zx + DIFF_WIDTH)
    k_all = mm(o_zx + DIFF_WIDTH, o_zx + 2 * DIFF_WIDTH)
    for j in range(DIFF_WIDTH // LANES):
        q = q_all[:, j * LANES:(j + 1) * LANES] * (DIFF_HEAD_DIM ** -0.5)
        k = k_all[:, j * LANES:(j + 1) * LANES]
        if rope:
            q, k = rot(q), rot(k)
        q_ref[0, :, j * LANES:(j + 1) * LANES] = q.astype(BF16)
        k_ref[0, j * LANES:(j + 1) * LANES, :] = k.T.astype(BF16)
    o_v = o_zx + 2 * DIFF_WIDTH
    v_ref[0] = mm(o_v, o_v + DIFF_WIDTH).astype(BF16)
    o_u = o_v + DIFF_WIDTH
    u_dt = mm(o_u, W_IN_PAD)
    u = u_dt[:, 0:S5_WIDTH]
    u_ref[0] = u
    u16_ref[0] = u.astype(BF16)
    dt_ref[0] = u_dt[:, S5_WIDTH:]


def _in_proj(x, shift, scale, norm_w, w_merged, conv_w, conv_b, rope_tabs):
    bsz, n, d = x.shape
    tm = _row_tile(n)
    nbh = n // HALO_ROWS
    tbh = tm // HALO_ROWS
    rope = rope_tabs is not None
    row = lambda b, i: (b, i, 0)
    per_b = lambda b, i: (b, 0, 0)
    const = lambda b, i: (0, 0)
    in_specs = [pl.BlockSpec((1, tm, d), row),
                pl.BlockSpec((1, HALO_ROWS, d), lambda b, i: (b, jnp.maximum(i * tbh - 1, 0), 0)),
                pl.BlockSpec((1, HALO_ROWS, d), lambda b, i: (b, jnp.minimum((i + 1) * tbh, nbh - 1), 0)),
                pl.BlockSpec((1, 1, d), per_b),
                pl.BlockSpec((1, 1, d), per_b),
                pl.BlockSpec((1, d), const),
                pl.BlockSpec((d, W_IN_PAD), const, pipeline_mode=pl.Buffered(1)),
                pl.BlockSpec((SSD_CONV, SSD_XBC), const),
                pl.BlockSpec((1, SSD_XBC), const)]
    args = [x, x, x, shift, scale, norm_w.reshape(1, d), w_merged, conv_w, conv_b.reshape(1, -1)]
    if rope:
        in_specs += [pl.BlockSpec((tm, LANES), lambda b, i: (i, 0))] * 2
        args += list(rope_tabs)
    widths = (SSD_WIDTH + SSD_XBC, LANES, DIFF_WIDTH, DIFF_WIDTH, DIFF_WIDTH, S5_WIDTH, S5_WIDTH)
    dtypes = (F32, F32, BF16, BF16, BF16, F32, BF16)
    out_specs = [pl.BlockSpec((1, tm, w), row) for w in widths]
    out_shape = [jax.ShapeDtypeStruct((bsz, n, w), t) for w, t in zip(widths, dtypes)]
    out_specs[3] = pl.BlockSpec((1, DIFF_WIDTH, tm), lambda b, i: (b, 0, i))
    out_shape[3] = jax.ShapeDtypeStruct((bsz, DIFF_WIDTH, n), BF16)
    return pl.pallas_call(
        functools.partial(_in_proj_kernel, rope=rope),
        grid=(bsz, n // tm),
        in_specs=in_specs,
        out_specs=out_specs,
        out_shape=out_shape,
        compiler_params=_cparams(2),
        name="in_proj_rope" if rope else "in_proj",
    )(*args)


def _merge_w_in(w):
    o1, o2 = SSD_IN, SSD_IN + DIFF_IN
    parts = [w[:, :SSD_WIDTH + SSD_XBC], w[:, o1:o2], w[:, o2:], w[:, SSD_WIDTH + SSD_XBC:o1]]
    m = jnp.concatenate(parts, axis=1)
    return jnp.pad(m, ((0, 0), (0, W_IN_PAD - m.shape[1]))).astype(BF16)


def _rope_tables(n):
    t = np.arange(n)
    pos = np.stack([t // GRID_W, t % GRID_W], axis=1).astype(np.float32)
    quarter = DIFF_HEAD_DIM // 4
    inv_freq = (np.float32(ROPE_THETA) ** (-np.arange(quarter, dtype=np.float32) / np.float32(quarter)))
    lane = np.arange(LANES)
    axis = (lane % DIFF_HEAD_DIM) // (DIFF_HEAD_DIM // 2)
    freq = lane % quarter
    ang = (pos[:, axis] * inv_freq.astype(np.float32)[freq][None, :]).astype(np.float32)
    sign = np.where(lane % (2 * quarter) < quarter, -1.0, 1.0).astype(np.float32)
    return jnp.asarray(np.cos(ang), F32), jnp.asarray(np.sin(ang) * sign[None, :], F32)


def _cumsum_rows(x, reverse):
    n = x.shape[0]
    row = lax.broadcasted_iota(jnp.int32, x.shape, 0)
    k = 1
    while k < n:
        if reverse:
            x = x + jnp.where(row < n - k, pltpu.roll(x, n - k, 0), 0.0)
        else:
            x = x + jnp.where(row >= k, pltpu.roll(x, k, 0), 0.0)
        k *= 2
    return x


def _ssd_chunk(xact_ref, dtt_ref, dta_ref, exp_ref, state_ref, y_ref, c, direction):
    q = SSD_CHUNK
    nh = SSD_HEADS
    r0 = pl.multiple_of(c * q, q)
    xa = xact_ref[pl.ds(r0, q), :]
    dta = dta_ref[pl.ds(r0, q), :]
    dt_t = dtt_ref[c]
    rows = lax.broadcasted_iota(jnp.int32, (q, q), 0)
    cols = lax.broadcasted_iota(jnp.int32, (q, q), 1)
    lane = cols
    mask = (rows >= cols) if direction == 0 else (rows <= cols)
    acs = _cumsum_rows(dta, reverse=direction == 1)
    acs_t = acs.T
    hi = acs.astype(BF16).astype(F32)
    r1 = acs - hi
    mid = r1.astype(BF16).astype(F32)
    lo = (r1 - mid).astype(BF16).astype(F32)
    parts = hi + pltpu.roll(mid, 8, 1) + pltpu.roll(lo, 16, 1)
    neg = -pltpu.roll(parts, 24, 1)
    lhs = jnp.where(lane < 24, parts, jnp.where(lane < 48, 1.0, 0.0)).astype(BF16)
    rhs = []
    for h in range(nh):
        j = direction * nh + h
        is_one = (lane == j) | (lane == 8 + j) | (lane == 16 + j)
        is_neg = (lane == 24 + j) | (lane == 32 + j) | (lane == 40 + j)
        rhs.append(jnp.where(is_one, 1.0, jnp.where(is_neg, neg, 0.0)).astype(BF16))
    diff = lax.dot_general(lhs, jnp.concatenate(rhs, axis=0), (((1,), (1,)), ((), ())),
                           preferred_element_type=F32)
    mask4 = jnp.concatenate([mask] * nh, axis=1)
    decay = jnp.exp(jnp.where(mask4, diff, -jnp.inf))
    acs_x = jnp.dot(parts.astype(BF16), exp_ref[direction], preferred_element_type=F32)
    last = q - 1 if direction == 0 else 0
    total_x = acs_x[last:last + 1, :]

    xb = xa[:, 0:SSD_WIDTH].astype(BF16)
    bm = xa[:, SSD_WIDTH:SSD_WIDTH + LANES]
    cm = xa[:, SSD_WIDTH + LANES:]
    bm_t = bm.T
    state = state_ref[direction]
    y_off = jnp.dot(cm.astype(BF16), state.astype(BF16), preferred_element_type=F32) * jnp.exp(acs_x)
    row_grp = lax.broadcasted_iota(jnp.int32, (LANES, SSD_WIDTH), 0) // SSD_STATE
    lane_head = lax.broadcasted_iota(jnp.int32, (LANES, SSD_WIDTH), 1) // SSD_HEAD_DIM
    x_head = lax.broadcasted_iota(jnp.int32, (q, SSD_WIDTH), 1) // SSD_HEAD_DIM
    b_parts, x_parts = [], []
    bm_sel = jnp.concatenate([jnp.where(lane // SSD_STATE == g, bm, 0.0) for g in range(SSD_GROUPS)],
                             axis=0).astype(BF16)
    cb_all = lax.dot_general(cm.astype(BF16), bm_sel, (((1,), (1,)), ((), ())),
                             preferred_element_type=F32)
    y_diag = []
    for g in range(SSD_GROUPS):
        cb = cb_all[:, g * q:(g + 1) * q]
        xg = xb[:, g * LANES:(g + 1) * LANES]
        pair = []
        for h in range(g * 2, g * 2 + 2):
            j = direction * nh + h
            row_dt = dt_t[j:j + 1, :]
            pair.append((cb * decay[:, h * q:(h + 1) * q] * row_dt).astype(BF16))
            w_row = row_dt * jnp.exp(acs_t[j:j + 1, last:last + 1] - acs_t[j:j + 1, :])
            b_parts.append((bm_t * w_row).astype(BF16))
            x_parts.append(jnp.where(x_head == h, xb, jnp.zeros_like(xb)))
        x_sel = jnp.concatenate([jnp.where(lane < SSD_HEAD_DIM, xg, jnp.zeros_like(xg)),
                                 jnp.where(lane >= SSD_HEAD_DIM, xg, jnp.zeros_like(xg))], axis=0)
        y_diag.append(jnp.dot(jnp.concatenate(pair, axis=1), x_sel, preferred_element_type=F32))
    y_ref[pl.ds(r0, q), :] += jnp.concatenate(y_diag, axis=-1) + y_off
    upd = jnp.dot(jnp.concatenate(b_parts, axis=1), jnp.concatenate(x_parts, axis=0),
                  preferred_element_type=F32)
    upd = jnp.where(row_grp == lane_head // (nh // SSD_GROUPS), upd, 0.0)
    state_ref[direction] = jnp.exp(total_x) * state + upd


def _ssd_kernel(*refs, n_ctx, n_lat, need_ctx):
    (zx_l, dt_l, zx_c, dt_c, dtb_ref, alog_ref, d_ref, nw_ref, exp_ref) = refs[:9]
    outs = refs[9:11] if need_ctx else (refs[9], None)
    xact_ref, dtt_ref, dta_ref, y_ref, state_ref = refs[-5:]
    out_l, out_c = outs

    for seg_ref, n, base in ((zx_c, n_ctx, 0), (zx_l, n_lat, n_ctx)):
        def copy_rows(i, carry, seg_ref=seg_ref, base=base):
            r = pl.multiple_of(i * SSD_CHUNK, SSD_CHUNK)
            act = seg_ref[0, pl.ds(r, SSD_CHUNK), SSD_WIDTH:]
            xact_ref[pl.ds(base + r, SSD_CHUNK), :] = act
            y_ref[pl.ds(base + r, SSD_CHUNK), :] = act[:, 0:SSD_WIDTH] * d_ref[...]
            return carry

        lax.fori_loop(0, n // SSD_CHUNK, copy_rows, 0)
    a_neg = -jnp.exp(alog_ref[...])
    head_lane = lax.broadcasted_iota(jnp.int32, (SSD_CHUNK, LANES), 1) < 2 * SSD_HEADS
    for seg_ref, n, base in ((dt_c, n_ctx, 0), (dt_l, n_lat, n_ctx)):
        for t0 in range(0, n, SSD_CHUNK):
            dt = jax.nn.softplus(seg_ref[0, t0:t0 + SSD_CHUNK, :] + dtb_ref[...])
            dtt_ref[(base + t0) // SSD_CHUNK] = dt.T
            dta_ref[base + t0:base + t0 + SSD_CHUNK, :] = jnp.where(head_lane, dt * a_neg, 0.0)
    n_tot = n_ctx + n_lat
    state_ref[...] = jnp.zeros(state_ref.shape, F32)

    ncc, nct = n_ctx // SSD_CHUNK, n_tot // SSD_CHUNK

    def step(i, carry):
        _ssd_chunk(xact_ref, dtt_ref, dta_ref, exp_ref, state_ref, y_ref, i, 0)
        cb_idx = jnp.where(i < ncc, ncc - 1 - i, nct - 1 - (i - ncc))
        _ssd_chunk(xact_ref, dtt_ref, dta_ref, exp_ref, state_ref, y_ref, cb_idx, 1)
        return carry

    lax.fori_loop(0, nct, step, 0, unroll=3)

    for seg_ref, o_ref, n, base in ((zx_c, out_c, n_ctx, 0), (zx_l, out_l, n_lat, n_ctx)):
        if o_ref is None:
            continue
        def gate_rows(i, carry, seg_ref=seg_ref, o_ref=o_ref, base=base):
            r = pl.multiple_of(i * SSD_ROWS, SSD_ROWS)
            z = seg_ref[0, pl.ds(r, SSD_ROWS), 0:SSD_WIDTH]
            v = y_ref[pl.ds(base + r, SSD_ROWS), :] * _silu(z)
            o_ref[0, pl.ds(r, SSD_ROWS), :] = _rms(v, nw_ref[...]).astype(BF16)
            return carry

        lax.fori_loop(0, n // SSD_ROWS, gate_rows, 0, unroll=4)


def _ssd(zx_l, dt_l, zx_c, dt_c, dt_bias, a_log, d_skip, norm_w, need_ctx):
    bsz, n_lat, wzx = zx_l.shape
    n_ctx = zx_c.shape[1]
    n_tot = n_ctx + n_lat
    pad8 = lambda v: jnp.pad(v.reshape(1, -1), ((0, 0), (0, LANES - v.size)))
    per_b = lambda b: (b, 0, 0)
    const = lambda b: (0, 0)
    k_idx = np.arange(LANES)[None, :, None]
    d_idx = np.arange(2)[:, None, None]
    h_idx = (np.arange(SSD_WIDTH) // SSD_HEAD_DIM)[None, None, :]
    expand = jnp.asarray((k_idx < 24) & (k_idx % 8 == d_idx * SSD_HEADS + h_idx), BF16)
    params = [pad8(dt_bias), pad8(a_log),
              jnp.repeat(d_skip, SSD_HEAD_DIM).reshape(1, -1), norm_w.reshape(1, -1)]
    in_specs = [pl.BlockSpec((1, n_lat, wzx), per_b), pl.BlockSpec((1, n_lat, LANES), per_b),
                pl.BlockSpec((1, n_ctx, wzx), per_b), pl.BlockSpec((1, n_ctx, LANES), per_b)]
    in_specs += [pl.BlockSpec(p.shape, const) for p in params]
    in_specs += [pl.BlockSpec(expand.shape, lambda b: (0, 0, 0))]
    out_specs = [pl.BlockSpec((1, n_lat, SSD_WIDTH), per_b)]
    out_shape = [jax.ShapeDtypeStruct((bsz, n_lat, SSD_WIDTH), BF16)]
    if need_ctx:
        out_specs.append(pl.BlockSpec((1, n_ctx, SSD_WIDTH), per_b))
        out_shape.append(jax.ShapeDtypeStruct((bsz, n_ctx, SSD_WIDTH), BF16))
    scratch = [pltpu.VMEM((n_tot, SSD_XBC), F32),
               pltpu.VMEM((n_tot // SSD_CHUNK, LANES, SSD_CHUNK), F32),
               pltpu.VMEM((n_tot, LANES), F32),
               pltpu.VMEM((n_tot, SSD_WIDTH), F32),
               pltpu.VMEM((2, SSD_GROUPS * SSD_STATE, SSD_WIDTH), F32)]
    outs = pl.pallas_call(
        functools.partial(_ssd_kernel, n_ctx=n_ctx, n_lat=n_lat, need_ctx=need_ctx),
        grid=(bsz,),
        in_specs=in_specs,
        out_specs=out_specs,
        out_shape=out_shape,
        scratch_shapes=scratch,
        compiler_params=_cparams(1),
        name="ssd_mixer",
    )(zx_l, dt_l, zx_c, dt_c, *params, expand)
    return (outs[0], outs[1]) if need_ctx else (outs[0], None)


def _attn_kernel(*refs, n_kv, lam_init):
    q_ref = refs[0]
    k_refs = refs[1:1 + n_kv]
    v_refs = refs[1 + n_kv:1 + 2 * n_kv]
    lq1, lk1, lq2, lk2, sw_ref, o_ref = refs[1 + 2 * n_kv:-4]
    s_refs, e_refs = refs[-4:-2], refs[-2:]
    tq, hw = q_ref.shape[1:]
    unit = s_refs[0].shape[0]
    lane = lax.broadcasted_iota(jnp.int32, (unit, hw), 1)
    lam = (jnp.exp(jnp.sum(lq1[...] * lk1[...], axis=-1, keepdims=True))
           - jnp.exp(jnp.sum(lq2[...] * lk2[...], axis=-1, keepdims=True)) + lam_init)
    offs = np.cumsum([0] + [k_ref.shape[2] for k_ref in k_refs])
    v_ext = [jnp.concatenate([v_ref[0], jnp.ones(v_ref.shape[1:], BF16)], axis=-1) for v_ref in v_refs]
    rows = max(ATTN_ROW_SPLIT, unit // 4)
    pieces = [(lo, min(lo + rows, unit)) for lo in range(0, unit, rows)]
    for base in range(0, tq, unit):
        q = q_ref[0, base:base + unit, :]
        for comp, s_ref in enumerate(s_refs):
            own = (lane >= DIFF_HEAD_DIM) if comp else (lane < DIFF_HEAD_DIM)
            qc = jnp.where(own, q, jnp.zeros_like(q))
            for i, k_ref in enumerate(k_refs):
                s_ref[:, offs[i]:offs[i + 1]] = jnp.dot(qc, k_ref[0], preferred_element_type=F32)
        outs = []
        for s_ref, e_ref in zip(s_refs, e_refs):
            parts = []
            for lo, hi in pieces:
                s = s_ref[lo:hi, :]
                e_ref[lo:hi, :] = jnp.exp(s - jnp.max(s, axis=-1, keepdims=True)).astype(BF16)
                ov = None
                for i in range(n_kv):
                    part = jnp.dot(e_ref[lo:hi, offs[i]:offs[i + 1]], v_ext[i], preferred_element_type=F32)
                    ov = part if ov is None else ov + part
                parts.append(ov[:, 0:hw] * (1.0 / ov[:, hw:hw + 1]))
            outs.append(parts)
        for (lo, hi), o0, o1 in zip(pieces, *outs):
            o = o0 - lam * o1
            o_ref[0, base + lo:base + hi, :] = (_rms(o, sw_ref[...]) * (1.0 - lam_init)).astype(BF16)


def _attention(q, ks, vs, lam_params, subln_w, lam_init):
    bsz, n, _ = q.shape
    tq = min(ATTN_Q_TILE, n)
    n_kv = len(ks)
    n_keys = sum(a.shape[2] for a in ks)
    hw = 2 * DIFF_HEAD_DIM
    in_specs = [pl.BlockSpec((1, tq, hw), lambda b, h, i: (b, i, h))]
    in_specs += [pl.BlockSpec((1, hw, a.shape[2]), lambda b, h, i: (b, h, 0)) for a in ks]
    in_specs += [pl.BlockSpec((1, a.shape[1], hw), lambda b, h, i: (b, 0, h)) for a in vs]
    in_specs += [pl.BlockSpec((1, DIFF_HEAD_DIM), lambda b, h, i: (0, 0))] * 4
    in_specs += [pl.BlockSpec((1, hw), lambda b, h, i: (0, 0))]
    return pl.pallas_call(
        functools.partial(_attn_kernel, n_kv=n_kv, lam_init=lam_init),
        grid=(bsz, DIFF_HEADS, n // tq),
        in_specs=in_specs,
        out_specs=pl.BlockSpec((1, tq, hw), lambda b, h, i: (b, i, h)),
        out_shape=jax.ShapeDtypeStruct((bsz, n, DIFF_WIDTH), BF16),
        scratch_shapes=([pltpu.VMEM((min(ATTN_UNIT, tq), n_keys), F32)] * 2
                        + [pltpu.VMEM((min(ATTN_UNIT, tq), n_keys), BF16)] * 2),
        compiler_params=_cparams(3),
        name="diff_attention",
    )(q, *ks, *vs, *[p.reshape(1, -1) for p in lam_params], subln_w.reshape(1, -1))


def _s5_ops_kernel(p_ref, bt_ref, cn_ref, ct_ref, kern_ref, ws_ref, co_ref, at_ref):
    for g in range(p_ref.shape[1]):
        lr, li = p_ref[0, g, 0:1, :], p_ref[0, g, 1:2, :]
        step = jnp.exp(p_ref[0, g, 2:3, :])
        mag = jnp.exp(lr * step)
        ar, ai = mag * jnp.cos(li * step), mag * jnp.sin(li * step)
        den = lr * lr + li * li
        nr, ni = ar - 1.0, ai
        cr, ci = (nr * lr + ni * li) / den, (ni * lr - nr * li) / den
        bt_re, bt_im = bt_ref[0, g, 0], bt_ref[0, g, 1]
        bb_re, bb_im = cr * bt_re - ci * bt_im, cr * bt_im + ci * bt_re
        c_re, c_im = cn_ref[0, g, 0], cn_ref[0, g, 1]
        ct_re, ct_im = ct_ref[0, g, 0], ct_ref[0, g, 1]
        pr, pi = jnp.ones_like(ar), jnp.zeros_like(ar)
        for tau in range(S5_T):
            w_re, w_im = bb_re * pr - bb_im * pi, bb_re * pi + bb_im * pr
            ws_ref[0, g, 0, tau] = w_re
            ws_ref[0, g, 1, tau] = w_im
            pr, pi = pr * ar - pi * ai, pr * ai + pi * ar
            co_ref[0, g, 0, tau] = c_re * pr - c_im * pi
            co_ref[0, g, 1, tau] = -(c_re * pi + c_im * pr)
        at_ref[0, g] = jnp.concatenate([pr, pi, jnp.zeros((SUBLANES - 2, S5_STATE), F32)], axis=0)
        nt, gc, ns = ws_ref.shape[3:]
        w_all = jnp.concatenate([ws_ref[0, g, 0].reshape(nt * gc, ns),
                                 ws_ref[0, g, 1].reshape(nt * gc, ns)], axis=-1)
        kern = jnp.dot(w_all, jnp.concatenate([ct_re, -ct_im], axis=0),
                       preferred_element_type=F32, precision=HI)
        kern_ref[0, g] = kern.reshape(nt, gc, gc)


def _s5_matrices(lam_re, lam_im, log_step, b_re, b_im, c_re, c_im):
    t, gc, ng, ns = S5_T, S5_GROUP, S5_GROUPS, S5_STATE
    f32 = lambda v: v.astype(F32)
    p = jnp.concatenate([f32(lam_re)[:, :, None, :], f32(lam_im)[:, :, None, :],
                         jnp.broadcast_to(f32(log_step)[:, :, None, None], (2, ng, 1, ns)),
                         jnp.zeros((2, ng, SUBLANES - 3, ns), F32)], axis=2)
    bt = jnp.stack([f32(b_re), f32(b_im)], axis=2).swapaxes(-1, -2)
    cn = jnp.stack([f32(c_re), f32(c_im)], axis=2)
    ct = cn.swapaxes(-1, -2)
    blk = lambda *tail: pl.BlockSpec((1, ng) + tail, lambda d: (d, 0) + (0,) * len(tail))
    kern, ws, co, at = pl.pallas_call(
        _s5_ops_kernel,
        grid=(2,),
        in_specs=[blk(SUBLANES, ns), blk(2, gc, ns), blk(2, gc, ns), blk(2, ns, gc)],
        out_specs=[blk(t, gc, gc), blk(2, t, gc, ns), blk(2, t, gc, ns), blk(SUBLANES, ns)],
        out_shape=[jax.ShapeDtypeStruct((2, ng, t, gc, gc), F32),
                   jax.ShapeDtypeStruct((2, ng, 2, t, gc, ns), F32),
                   jax.ShapeDtypeStruct((2, ng, 2, t, gc, ns), F32),
                   jax.ShapeDtypeStruct((2, ng, SUBLANES, ns), F32)],
        compiler_params=_cparams(1),
        name="s5_operators",
    )(p, bt, cn, ct)
    kern = kern.transpose(0, 1, 3, 2, 4)
    fwd = kern[0].reshape(S5_GROUPS, gc, t * gc)
    bwd = kern[1][:, :, ::-1].reshape(S5_GROUPS, gc, t * gc)
    band = jnp.concatenate([bwd[..., :(t - 1) * gc], bwd[..., (t - 1) * gc:] + fwd[..., :gc],
                            fwd[..., gc:]], axis=-1)
    k_full = jnp.concatenate([band[..., (t - 1 - s) * gc:(2 * t - 1 - s) * gc] for s in range(t)],
                             axis=1)
    wf, wb = ws[0][:, :, ::-1], ws[1]
    rows_sc = lambda w: w.reshape(ng, t * gc, ns)
    w_state = jnp.concatenate([rows_sc(wf[:, 0]), rows_sc(wb[:, 0]),
                               rows_sc(wf[:, 1]), rows_sc(wb[:, 1])], axis=-1)
    cf, cbk = co[0], co[1][:, :, ::-1]
    cols_tc = lambda w: w.transpose(0, 3, 1, 2).reshape(ng, ns, t * gc)
    c_off = jnp.concatenate([cols_tc(cf[:, 0]), cols_tc(cbk[:, 0]),
                             cols_tc(cf[:, 1]), cols_tc(cbk[:, 1])], axis=1)
    a_chunk = jnp.concatenate([at[0], at[1]], axis=-1)
    return k_full.astype(BF16), w_state.astype(BF16), c_off.astype(BF16), a_chunk


def _s5_kernel(uc_ref, ul_ref, kf_ref, ws_ref, co_ref, a_ref, yc_ref, yl_ref, e_ref, h_ref):
    ncc, bsz, w = uc_ref.shape[1:]
    ncl = ul_ref.shape[1]
    nct = ncc + ncl
    uc = uc_ref[0].reshape(ncc * bsz, w)
    ul = ul_ref[0].reshape(ncl * bsz, w)
    e_ref[0:ncc] = jnp.dot(uc, ws_ref[0], preferred_element_type=F32).reshape(ncc, bsz, w)
    e_ref[ncc:nct] = jnp.dot(ul, ws_ref[0], preferred_element_type=F32).reshape(ncl, bsz, w)
    ar = a_ref[0, 0:1, :]
    ai = a_ref[0, 1:2, :]
    half = S5_STATE
    lane = lax.broadcasted_iota(jnp.int32, (bsz, LANES), 1)
    is_f = lane < half

    def step(i, carry):
        h_re, h_im = carry
        jf = i
        jb = jnp.where(i < ncc, ncc - 1 - i, nct - 1 - (i - ncc))
        h_ref[jf, :, 0:half] = h_re[:, 0:half]
        h_ref[jb, :, half:LANES] = h_re[:, half:]
        h_ref[jf, :, LANES:LANES + half] = h_im[:, 0:half]
        h_ref[jb, :, LANES + half:] = h_im[:, half:]
        e_re = jnp.where(is_f, e_ref[jf, :, 0:LANES], e_ref[jb, :, 0:LANES])
        e_im = jnp.where(is_f, e_ref[jf, :, LANES:], e_ref[jb, :, LANES:])
        return (ar * h_re - ai * h_im + e_re, ar * h_im + ai * h_re + e_im)

    zero = jnp.zeros((bsz, LANES), F32)
    lax.fori_loop(0, nct, step, (zero, zero))
    for u, y_ref, lo, nc in ((uc, yc_ref, 0, ncc), (ul, yl_ref, ncc, ncl)):
        hs = h_ref[lo:lo + nc].reshape(nc * bsz, w).astype(BF16)
        y = (jnp.dot(u, kf_ref[0], preferred_element_type=F32)
             + jnp.dot(hs, co_ref[0], preferred_element_type=F32))
        y_ref[0] = y.reshape(nc, bsz, w).astype(BF16)


def _s5_to_chunks(u):
    bsz, n, _ = u.shape
    u = u.reshape(bsz, n // S5_T, S5_T, S5_GROUPS, S5_GROUP).transpose(3, 1, 0, 2, 4)
    return u.reshape(S5_GROUPS, n // S5_T, bsz, S5_T * S5_GROUP)


def _s5_from_chunks(y, bsz):
    nchunk = y.shape[1]
    y = y.reshape(S5_GROUPS, nchunk, bsz, S5_T, S5_GROUP).transpose(2, 1, 3, 0, 4)
    return y.reshape(bsz, nchunk * S5_T, S5_WIDTH)


def _s5(u_l, u_c, mats):
    k_full, w_state, c_off, a_chunk = mats
    bsz = u_l.shape[0]
    uc, ul = _s5_to_chunks(u_c), _s5_to_chunks(u_l)
    ncc, ncl, w = uc.shape[1], ul.shape[1], ul.shape[3]
    nct = ncc + ncl
    blk = lambda g: (g, 0, 0, 0)
    mat = lambda g: (g, 0, 0)
    y_c, y_l = pl.pallas_call(
        _s5_kernel,
        grid=(S5_GROUPS,),
        in_specs=[pl.BlockSpec((1, ncc, bsz, w), blk), pl.BlockSpec((1, ncl, bsz, w), blk),
                  pl.BlockSpec((1, w, w), mat), pl.BlockSpec((1, w, w), mat),
                  pl.BlockSpec((1, w, w), mat), pl.BlockSpec((1, SUBLANES, LANES), mat)],
        out_specs=[pl.BlockSpec((1, ncc, bsz, w), blk), pl.BlockSpec((1, ncl, bsz, w), blk)],
        out_shape=[jax.ShapeDtypeStruct((S5_GROUPS, ncc, bsz, w), BF16),
                   jax.ShapeDtypeStruct((S5_GROUPS, ncl, bsz, w), BF16)],
        scratch_shapes=[pltpu.VMEM((nct, bsz, w), F32), pltpu.VMEM((nct, bsz, w), F32)],
        compiler_params=_cparams(1),
        name="s5_mixer",
    )(uc, ul, k_full, w_state, c_off, a_chunk)
    return _s5_from_chunks(y_l, bsz), _s5_from_chunks(y_c, bsz)


def _mix_ffn_kernel(*refs):
    rows = [refs[3 * j:3 * j + 3] for j in range(5)]
    (ga_ref, sh_ref, sc_ref, gf_ref, d_ref, gw_ref, gb_ref, wo_ref, nwm_ref, nw1_ref, nw2_ref,
     wg_ref, wu_ref, cw_ref, cb_ref, wd_ref, o_ref, act_ref) = refs[15:]
    i = pl.program_id(1)
    is_first = i == 0
    is_last = i == pl.num_programs(1) - 1
    tm = o_ref.shape[1]
    a, b, ys, u, x_ext = (jnp.concatenate([r[0] for r in trio], axis=0) for trio in rows)
    y = jax.nn.gelu(ys + d_ref[...] * u, approximate=True)
    s_gate = jnp.dot(y.astype(BF16), gw_ref[...], preferred_element_type=F32) + gb_ref[...]
    s = (y * jax.nn.sigmoid(s_gate)).astype(BF16)
    mix = jnp.dot(jnp.concatenate([a, b, s], axis=-1), wo_ref[...], preferred_element_type=F32)
    x1_ext = x_ext + ga_ref[0] * _rms(mix, nwm_ref[...])
    x1 = x1_ext[0:tm]
    h_ext = _norm_modulate(x1_ext, nw1_ref[...], sc_ref[0], sh_ref[0])
    h_ext = jnp.concatenate([h_ext[0:tm],
                             jnp.where(is_first, 0.0, h_ext[tm:tm + HALO_ROWS]),
                             jnp.where(is_last, 0.0, h_ext[tm + HALO_ROWS:])], axis=0).astype(BF16)
    h = h_ext[0:tm]
    row = lax.broadcasted_iota(jnp.int32, (tm, 1), 0)
    prev_row, next_row = tm + HALO_ROWS - 1, tm + HALO_ROWS
    for lo in range(0, wg_ref.shape[1], FFN_CHUNK):
        hi = lo + FFN_CHUNK
        g_ext = jnp.dot(h_ext, wg_ref[:, lo:hi], preferred_element_type=F32)
        g = g_ext[0:tm]
        g_prev = jnp.where(row == 0, g_ext[prev_row:prev_row + 1], pltpu.roll(g, 1, 0))
        g_next = jnp.where(row == tm - 1, g_ext[next_row:next_row + 1], pltpu.roll(g, tm - 1, 0))
        conv = (cw_ref[0:1, lo:hi] * g_prev + cw_ref[1:2, lo:hi] * g + cw_ref[2:3, lo:hi] * g_next
                + cb_ref[:, lo:hi])
        up = jnp.dot(h, wu_ref[:, lo:hi], preferred_element_type=F32)
        act_ref[:, lo:hi] = (_silu(conv) * up).astype(BF16)
    f = jnp.dot(act_ref[...], wd_ref[...], preferred_element_type=F32)
    o_ref[0] = x1 + gf_ref[0] * _rms(f, nw2_ref[...])


def _mix_ffn(a, b, ys, u, x, g_a, shift, scale, g_f, s5_d, glu_w, glu_b, w_out, norm_mix,
             norm_pre, norm_post, w_gate, w_up, conv_w, conv_b, w_down):
    bsz, n, d = x.shape
    tm = min(FFN_ROW_TILE, n)
    f = w_gate.shape[1]
    nbh = n // HALO_ROWS
    tbh = tm // HALO_ROWS
    per_b = lambda bi, i: (bi, 0, 0)
    const = lambda bi, i: (0, 0)
    once = dict(pipeline_mode=pl.Buffered(1))

    def row_specs(w):
        return [pl.BlockSpec((1, tm, w), lambda bi, i: (bi, i, 0)),
                pl.BlockSpec((1, HALO_ROWS, w), lambda bi, i: (bi, jnp.maximum(i * tbh - 1, 0), 0)),
                pl.BlockSpec((1, HALO_ROWS, w), lambda bi, i: (bi, jnp.minimum((i + 1) * tbh, nbh - 1), 0))]

    row_args, in_specs = [], []
    for arr in (a, b, ys, u, x):
        row_args += [arr] * 3
        in_specs += row_specs(arr.shape[2])
    in_specs += [pl.BlockSpec((1, 1, d), per_b)] * 4
    in_specs += [pl.BlockSpec((1, S5_WIDTH), const), pl.BlockSpec((S5_WIDTH, S5_WIDTH), const),
                 pl.BlockSpec((1, S5_WIDTH), const), pl.BlockSpec((d, d), const, **once),
                 pl.BlockSpec((1, d), const), pl.BlockSpec((1, d), const), pl.BlockSpec((1, d), const),
                 pl.BlockSpec((d, f), const, **once), pl.BlockSpec((d, f), const, **once),
                 pl.BlockSpec((FFN_CONV, f), const), pl.BlockSpec((1, f), const),
                 pl.BlockSpec((f, d), const, **once)]
    return pl.pallas_call(
        _mix_ffn_kernel,
        grid=(bsz, n // tm),
        in_specs=in_specs,
        out_specs=pl.BlockSpec((1, tm, d), lambda bi, i: (bi, i, 0)),
        out_shape=jax.ShapeDtypeStruct((bsz, n, d), F32),
        scratch_shapes=[pltpu.VMEM((tm, f), BF16)],
        compiler_params=_cparams(2),
        name="mix_ffn",
    )(*row_args, g_a, shift, scale, g_f, s5_d.reshape(1, -1), glu_w.astype(BF16), glu_b.reshape(1, -1),
      w_out.astype(BF16), norm_mix.reshape(1, -1), norm_pre.reshape(1, -1), norm_post.reshape(1, -1),
      w_gate.astype(BF16), w_up.astype(BF16), conv_w, conv_b.reshape(1, -1), w_down.astype(BF16))


def kernel(x, c, ctx, c_ctx, mod_w, mod_b, mix_norm_pre, mix_norm_post, ffn_norm_pre, ffn_norm_post, w_in, w_out, ssd_conv_w, ssd_conv_b, ssd_dt_bias, ssd_a_log, ssd_d, ssd_norm_w, diff_lam_q1, diff_lam_k1, diff_lam_q2, diff_lam_k2, diff_subln_w, s5_lam_re, s5_lam_im, s5_log_step, s5_b_re, s5_b_im, s5_c_re, s5_c_im, s5_d, s5_glu_w, s5_glu_b, ffn_w_gate, ffn_w_up, ffn_conv_w, ffn_conv_b, ffn_w_down):
    bsz, n_lat, d = x.shape
    n_layers = mod_w.shape[0]
    rope_tabs = _rope_tables(n_lat)
    cc = jnp.concatenate([c, jnp.broadcast_to(c_ctx[None, :], (SUBLANES, d))], axis=0)
    mod = _modulation(cc, mod_w, mod_b)

    for layer in range(n_layers):
        need_ctx = layer < n_layers - 1
        lam_init = 0.8 - 0.6 * math.exp(-0.3 * layer)
        mod_l = mod[layer, :bsz].reshape(bsz, N_MOD, 1, d)
        mod_c = jnp.broadcast_to(mod[layer, bsz].reshape(1, N_MOD, 1, d), (bsz, N_MOD, 1, d))
        sh_a, sc_a, g_a, sh_f, sc_f, g_f = (mod_l[:, i] for i in range(N_MOD))
        csh_a, csc_a, cg_a, csh_f, csc_f, cg_f = (mod_c[:, i] for i in range(N_MOD))

        w_merged = _merge_w_in(w_in[layer])
        in_p = (mix_norm_pre[layer], w_merged, ssd_conv_w[layer], ssd_conv_b[layer])
        zx_l, dt_l, q_l, k_l, v_l, u_l, u16_l = _in_proj(x, sh_a, sc_a, *in_p, rope_tabs)
        zx_c, dt_c, q_c, k_c, v_c, u_c, u16_c = _in_proj(ctx, csh_a, csc_a, *in_p, None)

        a_l, a_c = _ssd(zx_l, dt_l, zx_c, dt_c, ssd_dt_bias[layer], ssd_a_log[layer], ssd_d[layer],
                        ssd_norm_w[layer], need_ctx)

        lam_params = (diff_lam_q1[layer], diff_lam_k1[layer], diff_lam_q2[layer], diff_lam_k2[layer])
        b_l = _attention(q_l, (k_c, k_l), (v_c, v_l), lam_params, diff_subln_w[layer], lam_init)

        mats = _s5_matrices(s5_lam_re[layer], s5_lam_im[layer], s5_log_step[layer], s5_b_re[layer],
                            s5_b_im[layer], s5_c_re[layer], s5_c_im[layer])
        ys_l, ys_c = _s5(u16_l, u16_c, mats)

        mix_p = (s5_d[layer], s5_glu_w[layer], s5_glu_b[layer], w_out[layer], mix_norm_post[layer])
        ffn_p = (ffn_norm_pre[layer], ffn_norm_post[layer], ffn_w_gate[layer], ffn_w_up[layer],
                 ffn_conv_w[layer], ffn_conv_b[layer], ffn_w_down[layer])
        x = _mix_ffn(a_l, b_l, ys_l, u_l, x, g_a, sh_f, sc_f, g_f, *mix_p, *ffn_p)
        if need_ctx:
            b_c = _attention(q_c, (k_c,), (v_c,), lam_params, diff_subln_w[layer], lam_init)
            ctx = _mix_ffn(a_c, b_c, ys_c, u_c, ctx, cg_a, csh_f, csc_f, cg_f, *mix_p, *ffn_p)
    return x
```
